```python
import math, functools
import jax, jax.numpy as jnp
from jax import lax
import numpy as np

D_MODEL = 1024
BATCH = 4
SEQ = 4096
DEPTH = 4
DEC_BATCH = 32
DEC_SEQ = 1
PAST_LEN = 8192
PAGE_SIZE = 128

POOL_WIDTH = D_MODEL // 2
POOL_WINDOWS = (2, 4, 8, 16)
N_POOL_GROUPS = len(POOL_WINDOWS)
POOL_GROUP = POOL_WIDTH // N_POOL_GROUPS
POOL_HIST = max(POOL_WINDOWS) - 1
HEAD_DIM = 64
ATTN_PATTERNS = ((128, 1), (512, 4), (2048, 16))
N_GROUPS = len(ATTN_PATTERNS)
HEADS_PER_GROUP = 4
N_HEADS = N_GROUPS * HEADS_PER_GROUP
ATTN_WIDTH = N_HEADS * HEAD_DIM
ATTN_OUT = HEADS_PER_GROUP * HEAD_DIM
N_KEYS = 128
BLK = 128
D_FF = 4 * D_MODEL
ROPE_THETA = 10000.0
EPS = 1e-6
IN_SPLITS = [POOL_WIDTH, POOL_WIDTH + ATTN_WIDTH, POOL_WIDTH + 2 * ATTN_WIDTH,
             POOL_WIDTH + 3 * ATTN_WIDTH, POOL_WIDTH + 3 * ATTN_WIDTH + D_MODEL]
IN_WIDTH = POOL_WIDTH + 3 * ATTN_WIDTH + 2 * D_MODEL

kernel_name = 'hybrid_pool_dilated_attn_decoder_step'


def rmsnorm(x, gain):
    xf = x.astype(jnp.float32)
    y = xf * lax.rsqrt(jnp.mean(xf * xf, axis=-1, keepdims=True) + EPS)
    return (y * gain.astype(jnp.float32)).astype(x.dtype)


def modulate(x, gain, shift, scale):
    return rmsnorm(x, gain) * (1 + scale) + shift


def rope(x, pos):
    inv_freq = ROPE_THETA ** (-jnp.arange(0, HEAD_DIM, 2, dtype=jnp.float32) / HEAD_DIM)
    ang = pos[:, None] * inv_freq[None, :]
    cos = jnp.cos(ang)[None, :, None, :]
    sin = jnp.sin(ang)[None, :, None, :]
    xf = x.astype(jnp.float32)
    x1, x2 = xf[..., :HEAD_DIM // 2], xf[..., HEAD_DIM // 2:]
    return jnp.concatenate([x1 * cos - x2 * sin, x2 * cos + x1 * sin], axis=-1).astype(x.dtype)


def project_inputs(h, pos, w_in, q_norm_g, k_norm_g):
    b, l = h.shape[0], h.shape[1]
    z = h @ w_in
    u, q, k, v, a_pool, a_attn = jnp.split(z, IN_SPLITS, axis=-1)
    q = rope(rmsnorm(q.reshape(b, l, N_HEADS, HEAD_DIM), q_norm_g), pos)
    k = rope(rmsnorm(k.reshape(b, l, N_HEADS, HEAD_DIM), k_norm_g), pos)
    v = v.reshape(b, l, N_HEADS, HEAD_DIM)
    return u, q, k, v, a_pool, a_attn


def pool_mix(ext, pos, w_pool_grp, pool_scale):
    b = ext.shape[0]
    l = ext.shape[1] - POOL_HIST
    ef = ext.astype(jnp.float32)
    cs = jnp.concatenate([jnp.zeros((b, 1, POOL_WIDTH), jnp.float32), jnp.cumsum(ef, axis=1)], axis=1)
    u = ef[:, POOL_HIST:]
    outs = []
    for gi, w in enumerate(POOL_WINDOWS):
        sl = slice(gi * POOL_GROUP, (gi + 1) * POOL_GROUP)
        wsum = cs[:, POOL_HIST + 1:POOL_HIST + 1 + l, sl] - cs[:, POOL_HIST + 1 - w:POOL_HIST + 1 - w + l, sl]
        cnt = jnp.minimum(pos + 1.0, float(w))[None, :, None]
        outs.append(wsum / cnt - u[..., sl])
    p = jnp.stack(outs, axis=2)
    y = jnp.einsum('blgc,gce->blge', p, w_pool_grp.astype(jnp.float32)).reshape(b, l, POOL_WIDTH)
    return (y * pool_scale.astype(jnp.float32)).astype(ext.dtype)


def band_dilated_attn(q, k, v, dil):
    b, s, h, dh = q.shape
    span = dil * BLK
    sp = -(-s // span) * span
    nb = sp // span

    def to_blocks(t):
        t = jnp.pad(t.astype(jnp.float32), ((0, 0), (0, sp - s), (0, 0), (0, 0)))
        t = jnp.moveaxis(t.reshape(b, sp // dil, dil, h, dh), 2, 1)
        return t.reshape(b, dil, nb, BLK, h, dh)

    def with_prev(t):
        prev = jnp.concatenate([jnp.zeros_like(t[:, :, :1]), t[:, :, :-1]], axis=2)
        return jnp.concatenate([prev, t], axis=3)

    qb = to_blocks(q)
    kk = with_prev(to_blocks(k))
    vv = with_prev(to_blocks(v))
    sc = jnp.einsum('brnqhd,brnkhd->brnhqk', qb, kk) / math.sqrt(dh)
    qi = jnp.arange(BLK)[:, None]
    kj = jnp.arange(2 * BLK)[None, :]
    dist = BLK + qi - kj
    band = (dist >= 0) & (dist <= N_KEYS)
    exists = (jnp.arange(nb)[:, None, None] * BLK - BLK + kj[None] >= 0)
    mask = (band[None] & exists)[None, None, :, None]
    sc = jnp.where(mask, sc, -jnp.inf)
    m = jnp.max(sc, axis=-1, keepdims=True)
    p = jnp.exp(sc - m)
    den = jnp.sum(p, axis=-1)
    o = jnp.einsum('brnhqk,brnkhd->brnqhd', p, vv) / jnp.swapaxes(den, -1, -2)[..., None]
    lse = jnp.swapaxes(m[..., 0] + jnp.log(den), -1, -2)

    def from_blocks(t):
        rest = t.shape[4:]
        t = jnp.moveaxis(t.reshape((b, dil, nb * BLK) + rest), 1, 2)
        return t.reshape((b, sp) + rest)[:, :s]

    return from_blocks(o), from_blocks(lse)


def gathered_dilated_attn(q, k_ext, v_ext, dil, hist):
    t = q.shape[1]
    idx = hist + jnp.arange(t)[:, None] - dil * jnp.arange(N_KEYS + 1)[None, :]
    valid = idx >= 0
    idxc = jnp.maximum(idx, 0)
    kg = k_ext[:, idxc].astype(jnp.float32)
    vg = v_ext[:, idxc].astype(jnp.float32)
    sc = jnp.einsum('bthd,btkhd->bthk', q.astype(jnp.float32), kg) / math.sqrt(HEAD_DIM)
    sc = jnp.where(valid[None, :, None, :], sc, -jnp.inf)
    m = jnp.max(sc, axis=-1, keepdims=True)
    p = jnp.exp(sc - m)
    den = jnp.sum(p, axis=-1)
    o = jnp.einsum('bthk,btkhd->bthd', p, vg) / den[..., None]
    return o, m[..., 0] + jnp.log(den)


def prompt_attend(g, q, k, v):
    win, dil = ATTN_PATTERNS[g]
    o, lse = band_dilated_attn(q, k, v, dil)
    keep = min(win, k.shape[1])
    return o, lse, k[:, k.shape[1] - keep:], v[:, v.shape[1] - keep:]


def sample_attend(cache_ks, cache_vs, g, q, k, v):
    win, dil = ATTN_PATTERNS[g]
    ck, cv = cache_ks[g], cache_vs[g]
    k_ext = jnp.concatenate([ck, k.astype(ck.dtype)], axis=1)
    v_ext = jnp.concatenate([cv, v.astype(cv.dtype)], axis=1)
    o, lse = gathered_dilated_attn(q, k_ext, v_ext, dil, ck.shape[1])
    return o, lse, k, v


def combine_groups(outs, lses):
    w = jax.nn.softmax(jnp.stack(lses, axis=0), axis=0)
    o = jnp.sum(w[..., None] * jnp.stack(outs, axis=0), axis=0)
    return o.reshape(o.shape[0], o.shape[1], ATTN_OUT)


def trunk_layer(x, c, pos, pool_prev, attend, norm1_g, norm2_g, w_ada, b_ada, w_in, q_norm_g, k_norm_g,
                w_pool_grp, pool_scale, w_pool_br, w_attn_br, w_out, w_up, w_down):
    mod = (jax.nn.silu(c) @ w_ada + b_ada)[:, None, :]
    sh1, sc1, g1, sh2, sc2, g2 = jnp.split(mod, 6, axis=-1)
    h = modulate(x, norm1_g, sh1, sc1)
    u, q, k, v, a_pool, a_attn = project_inputs(h, pos, w_in, q_norm_g, k_norm_g)
    ext = jnp.concatenate([pool_prev.astype(u.dtype), u], axis=1)
    pool_y = pool_mix(ext, pos, w_pool_grp, pool_scale)
    outs, lses, rows = [], [], []
    for g in range(N_GROUPS):
        hs = slice(g * HEADS_PER_GROUP, (g + 1) * HEADS_PER_GROUP)
        o, lse, kr, vr = attend(g, q[:, :, hs], k[:, :, hs], v[:, :, hs])
        outs.append(o)
        lses.append(lse)
        rows.append((kr, vr))
    attn_y = combine_groups(outs, lses).astype(x.dtype)
    merged = jax.nn.sigmoid(a_pool) * (pool_y @ w_pool_br) + jax.nn.sigmoid(a_attn) * (attn_y @ w_attn_br)
    x = x + g1 * (merged @ w_out)
    h2 = modulate(x, norm2_g, sh2, sc2)
    x = x + g2 * (jnp.square(jax.nn.relu(h2 @ w_up)) @ w_down)
    return x, rows, ext[:, ext.shape[1] - POOL_HIST:]


def setup_inputs(seed: int = 0) -> dict:
    key = jax.random.key(seed)
    ks = jax.random.split(key, 32)
    nrm = jax.random.normal
    f32 = jnp.float32
    inp = {}
    inp['x_prompt'] = nrm(ks[0], (BATCH, SEQ, D_MODEL), f32)
    inp['x_sample'] = nrm(ks[1], (DEC_BATCH, DEC_SEQ, D_MODEL), f32)
    for i, (win, dil) in enumerate(ATTN_PATTERNS):
        buf = min(win, PAST_LEN)
        inp['cache_k_w%d' % win] = nrm(ks[2 + 2 * i], (DEPTH, DEC_BATCH, buf, HEADS_PER_GROUP, HEAD_DIM), f32)
        inp['cache_v_w%d' % win] = nrm(ks[3 + 2 * i], (DEPTH, DEC_BATCH, buf, HEADS_PER_GROUP, HEAD_DIM), f32)
    inp['state_pool'] = nrm(ks[8], (DEPTH, DEC_BATCH, POOL_HIST, POOL_WIDTH), f32)
    inp['c_prompt'] = nrm(ks[9], (BATCH, D_MODEL), f32)
    inp['c_sample'] = nrm(ks[10], (DEC_BATCH, D_MODEL), f32)
    inp['norm1_g'] = 1.0 + 0.02 * nrm(ks[11], (DEPTH, D_MODEL), f32)
    inp['norm2_g'] = 1.0 + 0.02 * nrm(ks[12], (DEPTH, D_MODEL), f32)
    inp['w_ada'] = 0.3 * D_MODEL ** -0.5 * nrm(ks[13], (DEPTH, D_MODEL, 6 * D_MODEL), f32)
    inp['b_ada'] = 0.02 * nrm(ks[14], (DEPTH, 6 * D_MODEL), f32)
    inp['w_in'] = D_MODEL ** -0.5 * nrm(ks[15], (DEPTH, D_MODEL, IN_WIDTH), f32)
    inp['q_norm_g'] = 1.0 + 0.02 * nrm(ks[16], (DEPTH, HEAD_DIM), f32)
    inp['k_norm_g'] = 1.0 + 0.02 * nrm(ks[17], (DEPTH, HEAD_DIM), f32)
    inp['w_pool_grp'] = POOL_GROUP ** -0.5 * nrm(ks[18], (DEPTH, N_POOL_GROUPS, POOL_GROUP, POOL_GROUP), f32)
    inp['pool_scale'] = 1.0 + 0.02 * nrm(ks[19], (DEPTH, POOL_WIDTH), f32)
    inp['w_pool_br'] = POOL_WIDTH ** -0.5 * nrm(ks[20], (DEPTH, POOL_WIDTH, D_MODEL), f32)
    inp['w_attn_br'] = ATTN_OUT ** -0.5 * nrm(ks[21], (DEPTH, ATTN_OUT, D_MODEL), f32)
    inp['w_out'] = D_MODEL ** -0.5 * nrm(ks[22], (DEPTH, D_MODEL, D_MODEL), f32)
    inp['w_up'] = D_MODEL ** -0.5 * nrm(ks[23], (DEPTH, D_MODEL, D_FF), f32)
    inp['w_down'] = D_FF ** -0.5 * nrm(ks[24], (DEPTH, D_FF, D_MODEL), f32)
    return inp


def reference(x_prompt, x_sample, cache_k_w128, cache_v_w128, cache_k_w512, cache_v_w512,
              cache_k_w2048, cache_v_w2048, state_pool, c_prompt, c_sample, norm1_g, norm2_g,
              w_ada, b_ada, w_in, q_norm_g, k_norm_g, w_pool_grp, pool_scale, w_pool_br,
              w_attn_br, w_out, w_up, w_down):
    pos_p = jnp.arange(x_prompt.shape[1], dtype=jnp.float32)
    pos_s = PAST_LEN + jnp.arange(x_sample.shape[1], dtype=jnp.float32)
    caches_k = (cache_k_w128, cache_k_w512, cache_k_w2048)
    caches_v = (cache_v_w128, cache_v_w512, cache_v_w2048)
    kp = [[] for _ in range(N_GROUPS)]
    vp = [[] for _ in range(N_GROUPS)]
    ks_ = [[] for _ in range(N_GROUPS)]
    vs_ = [[] for _ in range(N_GROUPS)]
    pool_p_rows, pool_s_rows = [], []
    xp, xs = x_prompt, x_sample
    for l in range(DEPTH):
        weights = (norm1_g[l], norm2_g[l], w_ada[l], b_ada[l], w_in[l], q_norm_g[l], k_norm_g[l],
                   w_pool_grp[l], pool_scale[l], w_pool_br[l], w_attn_br[l], w_out[l], w_up[l], w_down[l])
        pool0 = jnp.zeros((xp.shape[0], POOL_HIST, POOL_WIDTH), xp.dtype)
        xp, rows_p, pool_p = trunk_layer(xp, c_prompt, pos_p, pool0, prompt_attend, *weights)
        attend_s = functools.partial(sample_attend, tuple(ck[l] for ck in caches_k),
                                     tuple(cv[l] for cv in caches_v))
        xs, rows_s, pool_s = trunk_layer(xs, c_sample, pos_s, state_pool[l], attend_s, *weights)
        for g in range(N_GROUPS):
            kp[g].append(rows_p[g][0])
            vp[g].append(rows_p[g][1])
            ks_[g].append(rows_s[g][0])
            vs_[g].append(rows_s[g][1])
        pool_p_rows.append(pool_p)
        pool_s_rows.append(pool_s)
    new_k_w128_p, new_k_w512_p, new_k_w2048_p = [jnp.stack(r, axis=0) for r in kp]
    new_v_w128_p, new_v_w512_p, new_v_w2048_p = [jnp.stack(r, axis=0) for r in vp]
    new_k_w128_s, new_k_w512_s, new_k_w2048_s = [jnp.stack(r, axis=0) for r in ks_]
    new_v_w128_s, new_v_w512_s, new_v_w2048_s = [jnp.stack(r, axis=0) for r in vs_]
    new_pool_p = jnp.stack(pool_p_rows, axis=0)
    new_pool_s = jnp.stack(pool_s_rows, axis=0)
    return (xp, xs,
            new_k_w128_p, new_v_w128_p, new_k_w512_p, new_v_w512_p, new_k_w2048_p, new_v_w2048_p, new_pool_p,
            new_k_w128_s, new_v_w128_s, new_k_w512_s, new_v_w512_s, new_k_w2048_s, new_v_w2048_s, new_pool_s)
```

```python
import functools
import math

import jax
import jax.numpy as jnp
from jax import lax
from jax.experimental import pallas as pl
from jax.experimental.pallas import tpu as pltpu

F32 = jnp.float32
BF16 = jnp.bfloat16

D_MODEL = 1024
DEPTH = 4
PAST_LEN = 8192
POOL_WIDTH = 512
POOL_WINDOWS = (2, 4, 8, 16)
POOL_GROUP = 128
POOL_HIST = 15
HEAD_DIM = 64
HEADS_PER_GROUP = 4
GROUP_WIDTH = HEADS_PER_GROUP * HEAD_DIM
DILATIONS = (1, 4, 16)
WINDOWS = (128, 512, 2048)
N_GROUPS = 3
ATTN_WIDTH = N_GROUPS * GROUP_WIDTH
N_KEYS = 128
BLK = 128
D_FF = 4096
ROPE_THETA = 10000.0
EPS = 1e-6
Q_SCALE = 1.0 / math.sqrt(HEAD_DIM)

COL_U = 0
COL_Q = POOL_WIDTH
COL_K = COL_Q + ATTN_WIDTH
COL_V = COL_K + ATTN_WIDTH
COL_AP = COL_V + ATTN_WIDTH
COL_AA = COL_AP + D_MODEL
IN_WIDTH = COL_AA + D_MODEL

LANES = 128
PROMPT_TILE = 512
SUPER = BLK * DILATIONS[-1]
VMEM_LIMIT = 56 * 1024 * 1024


def _cparams(*sem):
    return pltpu.CompilerParams(dimension_semantics=sem, vmem_limit_bytes=VMEM_LIMIT)


def _const_spec(shape):
    nd = len(shape)
    return pl.BlockSpec(shape, lambda *_: (0,) * nd, pipeline_mode=pl.Buffered(1))


def _dot(a, b):
    return jnp.dot(a, b, preferred_element_type=F32)


def _rms_modulate(x, gain, shift, scale):
    ms = jnp.mean(x * x, axis=-1, keepdims=True)
    return (x * lax.rsqrt(ms + EPS) * gain) * (1.0 + scale) + shift


def _mod_kernel(c_ref, w_ref, b_ref, o_ref):
    c = c_ref[...]
    s = c * (1.0 / (1.0 + jnp.exp(-c)))
    o_ref[...] = _dot(s.astype(BF16), w_ref[...].astype(BF16)) + b_ref[...]


def _modulation(c_all, w_ada, b_ada):
    rows = c_all.shape[0]
    depth = w_ada.shape[0]
    tn = 1536
    return pl.pallas_call(
        _mod_kernel,
        grid=(depth, 6 * D_MODEL // tn),
        in_specs=[
            pl.BlockSpec((rows, D_MODEL), lambda l, j: (0, 0)),
            pl.BlockSpec((None, D_MODEL, tn), lambda l, j: (l, 0, j)),
            pl.BlockSpec((None, 1, tn), lambda l, j: (l, 0, j)),
        ],
        out_specs=pl.BlockSpec((None, rows, tn), lambda l, j: (l, 0, j)),
        out_shape=jax.ShapeDtypeStruct((depth, rows, 6 * D_MODEL), F32),
        compiler_params=_cparams("parallel", "parallel"),
        name="adaln_mod",
    )(c_all, w_ada, b_ada.reshape(depth, 1, 6 * D_MODEL))


def _head_norm_rope(z, gain_ref, seg_ref, cos_ref, sa_ref, sb_ref, out_scale, store):
    cos = cos_ref[...]
    sa = sa_ref[...]
    sb = sb_ref[...]
    for c in range(ATTN_WIDTH // GROUP_WIDTH):
        zc = z[:, c * GROUP_WIDTH:(c + 1) * GROUP_WIDTH]
        ms = _dot((zc * zc).astype(BF16), seg_ref[...])
        n = zc * lax.rsqrt(ms + EPS) * gain_ref[:, c * GROUP_WIDTH:(c + 1) * GROUP_WIDTH]
        for half in range(GROUP_WIDTH // LANES):
            xx = n[:, half * LANES:(half + 1) * LANES]
            r = xx * cos + pltpu.roll(xx, LANES - HEAD_DIM // 2, 1) * sa + pltpu.roll(xx, HEAD_DIM // 2, 1) * sb
            store(c * (GROUP_WIDTH // LANES) + half, r * out_scale if out_scale != 1.0 else r)


def _project(x_ref, mod_ref, n1_ref, w_ref, gq_ref, gk_ref, seg_ref, cos_ref, sa_ref, sb_ref,
             store_q, store_k, store_v):
    x = x_ref[...]
    h = _rms_modulate(x, n1_ref[...], mod_ref[:, 0:D_MODEL], mod_ref[:, D_MODEL:2 * D_MODEL])
    hb = h.astype(BF16)
    u = _dot(hb, w_ref[:, COL_U:COL_Q])
    a_pool = _dot(hb, w_ref[:, COL_AP:COL_AA])
    a_attn = _dot(hb, w_ref[:, COL_AA:IN_WIDTH])
    _head_norm_rope(_dot(hb, w_ref[:, COL_Q:COL_K]), gq_ref, seg_ref, cos_ref, sa_ref, sb_ref, Q_SCALE, store_q)
    _head_norm_rope(_dot(hb, w_ref[:, COL_K:COL_V]), gk_ref, seg_ref, cos_ref, sa_ref, sb_ref, 1.0, store_k)
    v = _dot(hb, w_ref[:, COL_V:COL_AP])
    for c in range(ATTN_WIDTH // LANES):
        store_v(c, v[:, c * LANES:(c + 1) * LANES])
    return u, a_pool, a_attn


def _chunk_store(ref):
    def store(c, val):
        ref[c] = val
    return store


def _lane_store(ref):
    def store(c, val):
        ref[:, c * LANES:(c + 1) * LANES] = val
    return store


def _prompt_proj_kernel(x_ref, mod_ref, n1_ref, w_ref, gq_ref, gk_ref, seg_ref, cos_ref, sa_ref, sb_ref,
                        u_ref, ap_ref, aa_ref,
                        q0, q1, q2, k0, k1, k2, v0, v1, v2,
                        kw0, kw1, kw2, vw0, vw1, vw2,
                        qs_ref, ks_ref, vs_ref):
    u, a_pool, a_attn = _project(x_ref, mod_ref, n1_ref, w_ref, gq_ref, gk_ref, seg_ref, cos_ref, sa_ref, sb_ref,
                                 _chunk_store(qs_ref), _chunk_store(ks_ref), _chunk_store(vs_ref))
    u_ref[...] = u
    ap_ref[...] = a_pool.astype(BF16)
    aa_ref[...] = a_attn.astype(BF16)
    tm = x_ref.shape[0]
    for src, dsts in ((qs_ref, (q0, q1, q2)), (ks_ref, (k0, k1, k2)), (vs_ref, (v0, v1, v2))):
        for g, dst in enumerate(dsts):
            d = DILATIONS[g]
            for r in range(d):
                rows = pl.ds(r, tm // d, stride=d) if d > 1 else slice(None)
                for half in range(GROUP_WIDTH // LANES):
                    c = g * (GROUP_WIDTH // LANES) + half
                    dst[r, :, half * LANES:(half + 1) * LANES] = src[c, rows, :].astype(BF16)
    for src, dsts in ((ks_ref, (kw0, kw1, kw2)), (vs_ref, (vw0, vw1, vw2))):
        for g, dst in enumerate(dsts):
            keep = dst.shape[0]
            for half in range(GROUP_WIDTH // LANES):
                c = g * (GROUP_WIDTH // LANES) + half
                dst[:, half * LANES:(half + 1) * LANES] = src[c, tm - keep:, :]


def _prompt_project(x, mod_p, n1, w_in, gq, gk, seg, cos, sa, sb):
    b, s, _ = x.shape
    tm = min(PROMPT_TILE, s)
    nt = s // tm
    row = lambda width: pl.BlockSpec((None, tm, width), lambda bi, i: (bi, i, 0))
    in_specs = [
        row(D_MODEL),
        pl.BlockSpec((None, 1, 6 * D_MODEL), lambda bi, i: (bi, 0, 0)),
        _const_spec((1, D_MODEL)),
        _const_spec((D_MODEL, IN_WIDTH)),
        _const_spec((1, ATTN_WIDTH)),
        _const_spec((1, ATTN_WIDTH)),
        _const_spec((GROUP_WIDTH, GROUP_WIDTH)),
        pl.BlockSpec((tm, LANES), lambda bi, i: (i, 0)),
        pl.BlockSpec((tm, LANES), lambda bi, i: (i, 0)),
        pl.BlockSpec((tm, LANES), lambda bi, i: (i, 0)),
    ]
    out_shape = [jax.ShapeDtypeStruct((b, s, POOL_WIDTH), F32),
                 jax.ShapeDtypeStruct((b, s, D_MODEL), BF16),
                 jax.ShapeDtypeStruct((b, s, D_MODEL), BF16)]
    out_specs = [row(POOL_WIDTH), row(D_MODEL), row(D_MODEL)]
    for _ in range(3):
        for d in DILATIONS:
            out_shape.append(jax.ShapeDtypeStruct((b, d, s // d, GROUP_WIDTH), BF16))
            out_specs.append(pl.BlockSpec((None, d, tm // d, GROUP_WIDTH), lambda bi, i: (bi, 0, i, 0)))
    for _ in range(2):
        for w in WINDOWS:
            keep = min(w, s)
            out_shape.append(jax.ShapeDtypeStruct((b, keep, GROUP_WIDTH), F32))
            if keep >= tm:
                first = (s - keep) // tm
                out_specs.append(pl.BlockSpec(
                    (None, tm, GROUP_WIDTH), lambda bi, i, first=first: (bi, jnp.maximum(i - first, 0), 0)))
            else:
                out_specs.append(pl.BlockSpec((None, keep, GROUP_WIDTH), lambda bi, i: (bi, 0, 0)))
    return pl.pallas_call(
        _prompt_proj_kernel,
        grid=(b, nt),
        in_specs=in_specs,
        out_specs=out_specs,
        out_shape=out_shape,
        scratch_shapes=[pltpu.VMEM((ATTN_WIDTH // LANES, tm, LANES), F32)] * 3,
        compiler_params=_cparams("parallel", "arbitrary"),
        name="prompt_project",
    )(x, mod_p, n1, w_in, gq, gk, seg, cos, sa, sb)


def _sample_proj_kernel(x_ref, mod_ref, n1_ref, w_ref, gq_ref, gk_ref, seg_ref, cos_ref, sa_ref, sb_ref,
                        u_ref, ap_ref, aa_ref, q_ref, k_ref, v_ref):
    u, a_pool, a_attn = _project(x_ref, mod_ref, n1_ref, w_ref, gq_ref, gk_ref, seg_ref, cos_ref, sa_ref, sb_ref,
                                 _lane_store(q_ref), _lane_store(k_ref), _lane_store(v_ref))
    u_ref[...] = u
    ap_ref[...] = a_pool
    aa_ref[...] = a_attn


def _sample_project(x, mod_s, n1, w_in, gq, gk, seg, cos, sa, sb):
    n = x.shape[0]
    full = lambda shape: pl.BlockSpec(shape, lambda i: (0,) * len(shape))
    ins = (x, mod_s, n1, w_in, gq, gk, seg, cos, sa, sb)
    widths = (POOL_WIDTH, D_MODEL, D_MODEL, ATTN_WIDTH, ATTN_WIDTH, ATTN_WIDTH)
    return pl.pallas_call(
        _sample_proj_kernel,
        grid=(1,),
        in_specs=[full(a.shape) for a in ins],
        out_specs=[full((n, w)) for w in widths],
        out_shape=[jax.ShapeDtypeStruct((n, w), F32) for w in widths],
        compiler_params=_cparams("arbitrary"),
        name="sample_project",
    )(*ins)


def _head_lane_masks():
    lane = lax.broadcasted_iota(jnp.int32, (1, GROUP_WIDTH), 1)
    return [(lane // HEAD_DIM) == h for h in range(HEADS_PER_GROUP)]


def _prompt_attn_kernel(*refs):
    ins, o_ref = refs[:15], refs[15]
    kk_refs, vv_refs = refs[16:19], refs[19:22]
    cls_o, cls_l, nat_o, nat_l = refs[22:26]
    t = pl.program_id(1)
    hm = _head_lane_masks()
    row = lax.broadcasted_iota(jnp.int32, (HEADS_PER_GROUP * BLK, 2 * BLK), 0)
    col = lax.broadcasted_iota(jnp.int32, (HEADS_PER_GROUP * BLK, 2 * BLK), 1)
    dist = BLK + (row % BLK) - col
    band = (dist >= 0) & (dist <= N_KEYS)

    for g in range(N_GROUPS):
        d = DILATIONS[g]
        q_ref, kc_ref, kp_ref, vc_ref, vp_ref = ins[5 * g:5 * g + 5]
        kk, vv = kk_refs[g], vv_refs[g]
        rows_per_class = kc_ref.shape[1]
        nb = rows_per_class // BLK
        kk[:, 0:BLK, :] = kp_ref[...]
        kk[:, BLK:, :] = kc_ref[...]
        vv[:, 0:BLK, :] = vp_ref[...]
        vv[:, BLK:, :] = vc_ref[...]
        halves = GROUP_WIDTH // LANES

        def unit(u, carry, q_ref=q_ref, kk=kk, vv=vv, nb=nb):
            r = u // nb
            j = u % nb
            base = pl.multiple_of(j * BLK, BLK)
            q = q_ref[r, pl.ds(base, BLK), :]
            keys = kk[r, pl.ds(base, 2 * BLK), :]
            vals = vv[r, pl.ds(base, 2 * BLK), :]
            zero = jnp.zeros_like(q)
            qst = jnp.concatenate([jnp.where(m, q, zero) for m in hm], axis=0)
            s = lax.dot_general(qst, keys, (((1,), (1,)), ((), ())), preferred_element_type=F32)
            first_col = jnp.where(jnp.logical_or(t > 0, j > 0), 0, BLK)
            s = jnp.where(band & (col >= first_col), s, -jnp.inf)
            m = jnp.max(s, axis=-1, keepdims=True)
            p = jnp.exp(s - m)
            den = jnp.sum(p, axis=-1, keepdims=True)
            o = _dot(p.astype(BF16), vals) / den
            lse = m + jnp.log(den)
            o_u = jnp.zeros((BLK, GROUP_WIDTH), F32)
            l_u = jnp.zeros((BLK, GROUP_WIDTH), F32)
            for h in range(HEADS_PER_GROUP):
                o_u = jnp.where(hm[h], o[h * BLK:(h + 1) * BLK], o_u)
                l_u = jnp.where(hm[h], lse[h * BLK:(h + 1) * BLK], l_u)
            out_rows = pl.ds(pl.multiple_of(u * BLK, BLK), BLK)
            cls_o[out_rows, :] = o_u
            cls_l[out_rows, :] = l_u
            return carry

        lax.fori_loop(0, d * nb, unit, 0)
        for r in range(d):
            rows = pl.ds(r, rows_per_class, stride=d) if d > 1 else slice(None)
            src_rows = slice(r * rows_per_class, (r + 1) * rows_per_class)
            for half in range(halves):
                lanes = slice(half * LANES, (half + 1) * LANES)
                nat_o[g * halves + half, rows, :] = cls_o[src_rows, lanes]
                nat_l[g * halves + half, rows, :] = cls_l[src_rows, lanes]

    chunk = 256
    halves = GROUP_WIDTH // LANES

    def merge(ci, carry):
        rows = pl.ds(pl.multiple_of(ci * chunk, chunk), chunk)
        for half in range(halves):
            ls = [nat_l[g * halves + half, rows, :] for g in range(N_GROUPS)]
            os_ = [nat_o[g * halves + half, rows, :] for g in range(N_GROUPS)]
            m = jnp.maximum(jnp.maximum(ls[0], ls[1]), ls[2])
            es = [jnp.exp(l - m) for l in ls]
            tot = es[0] + es[1] + es[2]
            acc = es[0] * os_[0] + es[1] * os_[1] + es[2] * os_[2]
            o_ref[rows, half * LANES:(half + 1) * LANES] = (acc / tot).astype(o_ref.dtype)
        return carry

    lax.fori_loop(0, o_ref.shape[0] // chunk, merge, 0)


def _prompt_attention(qd, kd, vd):
    b = qd[0].shape[0]
    s = qd[0].shape[2]
    sb = min(SUPER, s)
    nsb = s // sb
    ins, in_specs, scratch_k, scratch_v = [], [], [], []
    for g, d in enumerate(DILATIONS):
        rpc = sb // d
        nb = rpc // BLK
        cur = pl.BlockSpec((None, d, rpc, GROUP_WIDTH), lambda bi, t: (bi, 0, t, 0))
        prev = pl.BlockSpec((None, d, BLK, GROUP_WIDTH),
                            lambda bi, t, nb=nb: (bi, 0, jnp.maximum(t * nb - 1, 0), 0))
        ins += [qd[g], kd[g], kd[g], vd[g], vd[g]]
        in_specs += [cur, cur, prev, cur, prev]
        scratch_k.append(pltpu.VMEM((d, BLK + rpc, GROUP_WIDTH), BF16))
        scratch_v.append(pltpu.VMEM((d, BLK + rpc, GROUP_WIDTH), BF16))
    scratch = scratch_k + scratch_v + [
        pltpu.VMEM((sb, GROUP_WIDTH), F32), pltpu.VMEM((sb, GROUP_WIDTH), F32),
        pltpu.VMEM((N_GROUPS * GROUP_WIDTH // LANES, sb, LANES), F32),
        pltpu.VMEM((N_GROUPS * GROUP_WIDTH // LANES, sb, LANES), F32)]
    return pl.pallas_call(
        _prompt_attn_kernel,
        grid=(b, nsb),
        in_specs=in_specs,
        out_specs=pl.BlockSpec((None, sb, GROUP_WIDTH), lambda bi, t: (bi, t, 0)),
        out_shape=jax.ShapeDtypeStruct((b, s, GROUP_WIDTH), BF16),
        scratch_shapes=scratch,
        compiler_params=_cparams("parallel", "parallel"),
        name="prompt_attention",
    )(*ins)


def _sigmoid(a):
    return 1.0 / (1.0 + jnp.exp(-a))


def _merge_ffn(x, pool_p, attn_y, a_pool, a_attn, mod_ref, n2_ref, wpg_ref, ps_ref, wpb_ref, wab_ref,
               wo_ref, wup_ref, wdn_ref):
    ys = [_dot(p.astype(BF16), wpg_ref[g]) for g, p in enumerate(pool_p)]
    pool_y = jnp.concatenate(ys, axis=-1) * ps_ref[...]
    pb = _dot(pool_y.astype(BF16), wpb_ref[...])
    ab = _dot(attn_y.astype(BF16), wab_ref[...])
    merged = _sigmoid(a_pool.astype(F32)) * pb + _sigmoid(a_attn.astype(F32)) * ab
    g1 = mod_ref[:, 2 * D_MODEL:3 * D_MODEL]
    x1 = x + g1 * _dot(merged.astype(BF16), wo_ref[...])
    h2 = _rms_modulate(x1, n2_ref[...], mod_ref[:, 3 * D_MODEL:4 * D_MODEL], mod_ref[:, 4 * D_MODEL:5 * D_MODEL])
    h2b = h2.astype(BF16)
    ff_chunk = 1024
    y = jnp.zeros_like(x)
    for c in range(D_FF // ff_chunk):
        hid = jnp.maximum(_dot(h2b, wup_ref[:, c * ff_chunk:(c + 1) * ff_chunk]), 0.0)
        y = y + _dot((hid * hid).astype(BF16), wdn_ref[c * ff_chunk:(c + 1) * ff_chunk, :])
    g2 = mod_ref[:, 5 * D_MODEL:6 * D_MODEL]
    return x1 + g2 * y


def _prompt_ffn_kernel(x_ref, y_ref, u_ref, up_ref, ap_ref, aa_ref, mod_ref, n2_ref, wpg_ref, ps_ref,
                       wpb_ref, wab_ref, wo_ref, wup_ref, wdn_ref, o_ref, ext_ref):
    i = pl.program_id(1)
    tm = x_ref.shape[0]
    hist = up_ref.shape[0]
    ext_ref[0:hist, :] = jnp.where(i > 0, up_ref[...], 0.0)
    ext_ref[hist:, :] = u_ref[...]
    pos = (i * tm + lax.broadcasted_iota(jnp.int32, (tm, 1), 0)).astype(F32)
    pool_p = []
    for g, w in enumerate(POOL_WINDOWS):
        e = ext_ref[:, g * POOL_GROUP:(g + 1) * POOL_GROUP]
        acc = e
        span = 1
        while span < w:
            acc = acc + pltpu.roll(acc, span, 0)
            span *= 2
        cnt = jnp.minimum(pos + 1.0, float(w))
        pool_p.append(acc[hist:] / cnt - e[hist:])
    o_ref[...] = _merge_ffn(x_ref[...], pool_p, y_ref[...], ap_ref[...], aa_ref[...], mod_ref, n2_ref, wpg_ref,
                            ps_ref, wpb_ref, wab_ref, wo_ref, wup_ref, wdn_ref)


def _prompt_ffn(x, attn_y, u, a_pool, a_attn, mod_p, n2, wpg, ps, wpb, wab, wo, wup, wdn):
    b, s, _ = x.shape
    tm = min(PROMPT_TILE, s)
    hist = 16
    row = lambda width: pl.BlockSpec((None, tm, width), lambda bi, i: (bi, i, 0))
    in_specs = [
        row(D_MODEL), row(GROUP_WIDTH), row(POOL_WIDTH),
        pl.BlockSpec((None, hist, POOL_WIDTH), lambda bi, i: (bi, jnp.maximum(i * (tm // hist) - 1, 0), 0)),
        row(D_MODEL), row(D_MODEL),
        pl.BlockSpec((None, 1, 6 * D_MODEL), lambda bi, i: (bi, 0, 0)),
        _const_spec(n2.shape), _const_spec(wpg.shape), _const_spec(ps.shape), _const_spec(wpb.shape),
        _const_spec(wab.shape), _const_spec(wo.shape), _const_spec(wup.shape), _const_spec(wdn.shape),
    ]
    return pl.pallas_call(
        _prompt_ffn_kernel,
        grid=(b, s // tm),
        in_specs=in_specs,
        out_specs=row(D_MODEL),
        out_shape=jax.ShapeDtypeStruct((b, s, D_MODEL), F32),
        scratch_shapes=[pltpu.VMEM((hist + tm, POOL_WIDTH), F32)],
        compiler_params=_cparams("parallel", "parallel"),
        name="prompt_merge_ffn",
    )(x, attn_y, u, u, a_pool, a_attn, mod_p, n2, wpg, ps, wpb, wab, wo, wup, wdn)


def _sample_ffn_kernel(x_ref, y_ref, u_ref, st_ref, ap_ref, aa_ref, mod_ref, n2_ref, wpg_ref, ps_ref,
                       wpb_ref, wab_ref, wo_ref, wup_ref, wdn_ref, o_ref):
    n = x_ref.shape[0]
    u = u_ref[...]
    tail = jnp.zeros_like(u)
    sums = {}
    for back in range(1, POOL_HIST + 1):
        row = POOL_HIST - back
        tail = tail + st_ref[:, row * POOL_WIDTH:(row + 1) * POOL_WIDTH]
        if back + 1 in POOL_WINDOWS:
            sums[back + 1] = tail
    pool_p = []
    for g, w in enumerate(POOL_WINDOWS):
        cols = slice(g * POOL_GROUP, (g + 1) * POOL_GROUP)
        cnt = min(PAST_LEN + 1.0, float(w))
        pool_p.append((sums[w][:, cols] + u[:, cols]) / cnt - u[:, cols])
    o_ref[...] = _merge_ffn(x_ref[...], pool_p, y_ref[...], ap_ref[...], aa_ref[...], mod_ref, n2_ref, wpg_ref,
                            ps_ref, wpb_ref, wab_ref, wo_ref, wup_ref, wdn_ref)


def _sample_ffn(layer, x, attn_y, u, state2d, a_pool, a_attn, mod_s, n2, wpg, ps, wpb, wab, wo, wup, wdn):
    n = x.shape[0]
    full = lambda a: pl.BlockSpec(a.shape, lambda i: (0,) * a.ndim)
    ins = (x, attn_y, u, state2d, a_pool, a_attn, mod_s, n2, wpg, ps, wpb, wab, wo, wup, wdn)
    in_specs = [full(a) for a in ins]
    in_specs[3] = pl.BlockSpec((None,) + state2d.shape[1:], lambda i: (layer, 0, 0))
    for k in range(7, len(ins)):
        in_specs[k] = _const_spec(ins[k].shape)
    return pl.pallas_call(
        _sample_ffn_kernel,
        grid=(1,),
        in_specs=in_specs,
        out_specs=pl.BlockSpec((n, D_MODEL), lambda i: (0, 0)),
        out_shape=jax.ShapeDtypeStruct((n, D_MODEL), F32),
        compiler_params=_cparams("arbitrary"),
        name="sample_merge_ffn",
    )(*ins)


def _sample_attn_kernel(q_ref, k_ref, v_ref, ck0, cv0, ck1, cv1, ck2, cv2, sel_ref, selt_ref, o_ref):
    caches = ((ck0, cv0), (ck1, cv1), (ck2, cv2))
    bt = q_ref.shape[0]
    sel = sel_ref[...]
    selt = selt_ref[...]
    hm = _head_lane_masks()

    def per_head_lanes(a):
        out = jnp.zeros((bt, GROUP_WIDTH), F32)
        for h in range(HEADS_PER_GROUP):
            out = jnp.where(hm[h], a[:, h:h + 1], out)
        return out

    outs, lses = [], []
    for g, (ck, cv) in enumerate(caches):
        cols = slice(g * GROUP_WIDTH, (g + 1) * GROUP_WIDTH)
        q = q_ref[:, cols]
        k_new = k_ref[:, cols]
        v_new = v_ref[:, cols]
        keys = ck[...]
        vals = cv[...]
        nk = keys.shape[1]
        prod = (keys * q[:, None, :]).reshape(bt * nk, GROUP_WIDTH)
        s = _dot(prod.astype(BF16), sel).reshape(bt, nk, LANES)
        s_new = _dot((k_new * q).astype(BF16), sel)
        m = jnp.maximum(jnp.max(s, axis=1), s_new)
        p = jnp.exp(s - m[:, None, :])
        p_new = jnp.exp(s_new - m)
        den = jnp.sum(p, axis=1) + p_new
        pe = _dot(p.reshape(bt * nk, LANES).astype(BF16), selt).reshape(bt, nk, GROUP_WIDTH)
        acc = jnp.sum(pe * vals, axis=1) + _dot(p_new.astype(BF16), selt) * v_new
        den_e = per_head_lanes(den)
        outs.append(acc / den_e)
        lses.append(per_head_lanes(m) + jnp.log(den_e))
    mx = jnp.maximum(jnp.maximum(lses[0], lses[1]), lses[2])
    es = [jnp.exp(l - mx) for l in lses]
    o_ref[...] = (es[0] * outs[0] + es[1] * outs[1] + es[2] * outs[2]) / (es[0] + es[1] + es[2])


def _sample_attention(layer, q, k, v, cache_views, sel, selt):
    n = q.shape[0]
    bt = 8
    row = pl.BlockSpec((bt, ATTN_WIDTH), lambda i: (i, 0))
    in_specs = [row, row, row]
    for _ in cache_views:
        in_specs.append(pl.BlockSpec((None, bt, N_KEYS, GROUP_WIDTH), lambda i: (layer, i, 0, 0)))
    in_specs += [pl.BlockSpec(sel.shape, lambda i: (0, 0)), pl.BlockSpec(selt.shape, lambda i: (0, 0))]
    return pl.pallas_call(
        _sample_attn_kernel,
        grid=(n // bt,),
        in_specs=in_specs,
        out_specs=pl.BlockSpec((bt, GROUP_WIDTH), lambda i: (i, 0)),
        out_shape=jax.ShapeDtypeStruct((n, GROUP_WIDTH), F32),
        compiler_params=_cparams("parallel"),
        name="sample_attention",
    )(q, k, v, *cache_views, sel, selt)


def _rope_tables(pos):
    inv_freq = ROPE_THETA ** (-jnp.arange(0, HEAD_DIM, 2, dtype=F32) / HEAD_DIM)
    ang = pos[:, None] * inv_freq[None, :]
    cos, sin = jnp.cos(ang), jnp.sin(ang)
    zero = jnp.zeros_like(sin)
    reps = LANES // HEAD_DIM
    cos_t = jnp.tile(jnp.concatenate([cos, cos], axis=-1), (1, reps))
    sa_t = jnp.tile(jnp.concatenate([-sin, zero], axis=-1), (1, reps))
    sb_t = jnp.tile(jnp.concatenate([zero, sin], axis=-1), (1, reps))
    return cos_t, sa_t, sb_t


def _segment_matrices():
    lane = jnp.arange(GROUP_WIDTH)
    seg = (lane[:, None] // HEAD_DIM == lane[None, :] // HEAD_DIM).astype(F32) / HEAD_DIM
    sel = (lane[:, None] // HEAD_DIM == jnp.arange(LANES)[None, :]).astype(F32)
    return seg.astype(BF16), sel.astype(BF16), sel.T


def kernel(x_prompt, x_sample, cache_k_w128, cache_v_w128, cache_k_w512, cache_v_w512, cache_k_w2048,
           cache_v_w2048, state_pool, c_prompt, c_sample, norm1_g, norm2_g, w_ada, b_ada, w_in, q_norm_g,
           k_norm_g, w_pool_grp, pool_scale, w_pool_br, w_attn_br, w_out, w_up, w_down):
    depth = w_in.shape[0]
    b, s, _ = x_prompt.shape
    n = x_sample.shape[0]
    assert x_sample.shape[1] == 1 and s % min(SUPER, s) == 0 and s % min(PROMPT_TILE, s) == 0

    mod = _modulation(jnp.concatenate([c_prompt, c_sample], axis=0), w_ada, b_ada)
    seg, sel, selt = _segment_matrices()
    tab_p = _rope_tables(jnp.arange(s, dtype=F32))
    tab_s = _rope_tables(PAST_LEN + jnp.arange(1, dtype=F32))

    caches = []
    for ck, cv, d in zip((cache_k_w128, cache_k_w512, cache_k_w2048),
                         (cache_v_w128, cache_v_w512, cache_v_w2048), DILATIONS):
        for c in (ck, cv):
            assert c.shape[2] == N_KEYS * d
            caches.append(c.reshape(depth, n, N_KEYS, d * GROUP_WIDTH))
    state2d = state_pool.reshape(depth, n, POOL_HIST * POOL_WIDTH)

    bf = lambda a: a.astype(BF16)
    w_in_b, wpg_b, wpb_b, wab_b, wo_b, wup_b, wdn_b = map(
        bf, (w_in, w_pool_grp, w_pool_br, w_attn_br, w_out, w_up, w_down))

    xp = x_prompt
    xs = x_sample.reshape(n, D_MODEL)
    kp = [[] for _ in range(N_GROUPS)]
    vp = [[] for _ in range(N_GROUPS)]
    ks = [[] for _ in range(N_GROUPS)]
    vs = [[] for _ in range(N_GROUPS)]
    pool_p, pool_s = [], []
    for l in range(depth):
        n1 = norm1_g[l].reshape(1, D_MODEL)
        n2 = norm2_g[l].reshape(1, D_MODEL)
        gq = jnp.tile(q_norm_g[l], ATTN_WIDTH // HEAD_DIM).reshape(1, ATTN_WIDTH)
        gk = jnp.tile(k_norm_g[l], ATTN_WIDTH // HEAD_DIM).reshape(1, ATTN_WIDTH)
        ps = pool_scale[l].reshape(1, POOL_WIDTH)
        mod_p = mod[l, :b].reshape(b, 1, 6 * D_MODEL)
        mod_s = mod[l, b:]
        tail_w = (n2, wpg_b[l], ps, wpb_b[l], wab_b[l], wo_b[l], wup_b[l], wdn_b[l])

        outs = _prompt_project(xp, mod_p, n1, w_in_b[l], gq, gk, seg, *tab_p)
        u, a_pool, a_attn = outs[0:3]
        qd, kd, vd = outs[3:6], outs[6:9], outs[9:12]
        kws, vws = outs[12:15], outs[15:18]
        attn_y = _prompt_attention(qd, kd, vd)
        xp = _prompt_ffn(xp, attn_y, u, a_pool, a_attn, mod_p, *tail_w)
        for g in range(N_GROUPS):
            kp[g].append(kws[g].reshape(b, -1, HEADS_PER_GROUP, HEAD_DIM))
            vp[g].append(vws[g].reshape(b, -1, HEADS_PER_GROUP, HEAD_DIM))
        pool_p.append(u[:, s - POOL_HIST:, :])

        u_s, ap_s, aa_s, q_s, k_s, v_s = _sample_project(xs, mod_s, n1, w_in_b[l], gq, gk, seg, *tab_s)
        y_s = _sample_attention(l, q_s, k_s, v_s, caches, sel, selt)
        xs = _sample_ffn(l, xs, y_s, u_s, state2d, ap_s, aa_s, mod_s, *tail_w)
        for g in range(N_GROUPS):
            cols = slice(g * GROUP_WIDTH, (g + 1) * GROUP_WIDTH)
            ks[g].append(k_s[:, cols].reshape(n, 1, HEADS_PER_GROUP, HEAD_DIM))
            vs[g].append(v_s[:, cols].reshape(n, 1, HEADS_PER_GROUP, HEAD_DIM))
        pool_s.append(jnp.concatenate([state_pool[l][:, 1:], u_s[:, None, :]], axis=1))

    st = lambda rows: jnp.stack(rows, axis=0)
    return (xp, xs.reshape(n, 1, D_MODEL),
            st(kp[0]), st(vp[0]), st(kp[1]), st(vp[1]), st(kp[2]), st(vp[2]), st(pool_p),
            st(ks[0]), st(vs[0]), st(ks[1]), st(vs[1]), st(ks[2]), st(vs[2]), st(pool_s))
```

```python
import functools
import math

import jax
import jax.numpy as jnp
from jax import lax
from jax.experimental import pallas as pl
from jax.experimental.pallas import tpu as pltpu

F32 = jnp.float32
BF16 = jnp.bfloat16

D_MODEL = 1024
DEPTH = 4
PAST_LEN = 8192
POOL_WIDTH = 512
POOL_WINDOWS = (2, 4, 8, 16)
POOL_GROUP = 128
POOL_HIST = 15
HEAD_DIM = 64
HEADS_PER_GROUP = 4
GROUP_WIDTH = HEADS_PER_GROUP * HEAD_DIM
DILATIONS = (1, 4, 16)
WINDOWS = (128, 512, 2048)
N_GROUPS = 3
ATTN_WIDTH = N_GROUPS * GROUP_WIDTH
N_KEYS = 128
BLK = 128
D_FF = 4096
ROPE_THETA = 10000.0
EPS = 1e-6
Q_SCALE = 1.0 / math.sqrt(HEAD_DIM)

COL_U = 0
COL_Q = POOL_WIDTH
COL_K = COL_Q + ATTN_WIDTH
COL_V = COL_K + ATTN_WIDTH
COL_AP = COL_V + ATTN_WIDTH
COL_AA = COL_AP + D_MODEL
IN_WIDTH = COL_AA + D_MODEL

LANES = 128
PROMPT_TILE = 512
SUPER = BLK * DILATIONS[-1]
VMEM_LIMIT = 56 * 1024 * 1024


def _cparams(*sem):
    return pltpu.CompilerParams(dimension_semantics=sem, vmem_limit_bytes=VMEM_LIMIT)


def _const_spec(shape):
    nd = len(shape)
    return pl.BlockSpec(shape, lambda *_: (0,) * nd, pipeline_mode=pl.Buffered(1))


def _dot(a, b):
    return jnp.dot(a, b, preferred_element_type=F32)


def _rms_modulate(x, gain, shift, scale):
    ms = jnp.mean(x * x, axis=-1, keepdims=True)
    return (x * lax.rsqrt(ms + EPS) * gain) * (1.0 + scale) + shift


def _mod_kernel(c_ref, w_ref, b_ref, o_ref):
    c = c_ref[...]
    s = c * (1.0 / (1.0 + jnp.exp(-c)))
    o_ref[...] = _dot(s.astype(BF16), w_ref[...].astype(BF16)) + b_ref[...]


def _modulation(c_all, w_ada, b_ada):
    rows = c_all.shape[0]
    depth = w_ada.shape[0]
    tn = 1536
    return pl.pallas_call(
        _mod_kernel,
        grid=(depth, 6 * D_MODEL // tn),
        in_specs=[
            pl.BlockSpec((rows, D_MODEL), lambda l, j: (0, 0)),
            pl.BlockSpec((None, D_MODEL, tn), lambda l, j: (l, 0, j)),
            pl.BlockSpec((None, 1, tn), lambda l, j: (l, 0, j)),
        ],
        out_specs=pl.BlockSpec((None, rows, tn), lambda l, j: (l, 0, j)),
        out_shape=jax.ShapeDtypeStruct((depth, rows, 6 * D_MODEL), F32),
        compiler_params=_cparams("parallel", "parallel"),
        name="adaln_mod",
    )(c_all, w_ada, b_ada.reshape(depth, 1, 6 * D_MODEL))


def _head_norm_rope(z, gain_ref, seg_ref, cos_ref, sa_ref, sb_ref, out_scale, store):
    cos = cos_ref[...]
    sa = sa_ref[...]
    sb = sb_ref[...]
    for c in range(ATTN_WIDTH // GROUP_WIDTH):
        zc = z[:, c * GROUP_WIDTH:(c + 1) * GROUP_WIDTH]
        ms = _dot((zc * zc).astype(BF16), seg_ref[...])
        n = zc * lax.rsqrt(ms + EPS) * gain_ref[:, c * GROUP_WIDTH:(c + 1) * GROUP_WIDTH]
        for half in range(GROUP_WIDTH // LANES):
            xx = n[:, half * LANES:(half + 1) * LANES]
            r = xx * cos + pltpu.roll(xx, LANES - HEAD_DIM // 2, 1) * sa + pltpu.roll(xx, HEAD_DIM // 2, 1) * sb
            store(c * (GROUP_WIDTH // LANES) + half, r * out_scale if out_scale != 1.0 else r)


def _project(x_ref, mod_ref, n1_ref, w_ref, gq_ref, gk_ref, seg_ref, cos_ref, sa_ref, sb_ref,
             store_q, store_k, store_v):
    x = x_ref[...]
    h = _rms_modulate(x, n1_ref[...], mod_ref[:, 0:D_MODEL], mod_ref[:, D_MODEL:2 * D_MODEL])
    hb = h.astype(BF16)
    u = _dot(hb, w_ref[:, COL_U:COL_Q])
    a_pool = _dot(hb, w_ref[:, COL_AP:COL_AA])
    a_attn = _dot(hb, w_ref[:, COL_AA:IN_WIDTH])
    _head_norm_rope(_dot(hb, w_ref[:, COL_Q:COL_K]), gq_ref, seg_ref, cos_ref, sa_ref, sb_ref, Q_SCALE, store_q)
    _head_norm_rope(_dot(hb, w_ref[:, COL_K:COL_V]), gk_ref, seg_ref, cos_ref, sa_ref, sb_ref, 1.0, store_k)
    v = _dot(hb, w_ref[:, COL_V:COL_AP])
    for c in range(ATTN_WIDTH // LANES):
        store_v(c, v[:, c * LANES:(c + 1) * LANES])
    return u, a_pool, a_attn


def _chunk_store(ref):
    def store(c, val):
        ref[c] = val
    return store


def _lane_store(ref):
    def store(c, val):
        ref[:, c * LANES:(c + 1) * LANES] = val
    return store


def _prompt_proj_kernel(x_ref, mod_ref, n1_ref, w_ref, gq_ref, gk_ref, seg_ref, cos_ref, sa_ref, sb_ref,
                        u_ref, ap_ref, aa_ref,
                        q0, q1, q2, k0, k1, k2, v0, v1, v2,
                        kw0, kw1, kw2, vw0, vw1, vw2,
                        qs_ref, ks_ref, vs_ref):
    u, a_pool, a_attn = _project(x_ref, mod_ref, n1_ref, w_ref, gq_ref, gk_ref, seg_ref, cos_ref, sa_ref, sb_ref,
                                 _chunk_store(qs_ref), _chunk_store(ks_ref), _chunk_store(vs_ref))
    u_ref[...] = u
    ap_ref[...] = a_pool.astype(BF16)
    aa_ref[...] = a_attn.astype(BF16)
    tm = x_ref.shape[0]
    for src, dsts in ((qs_ref, (q0, q1, q2)), (ks_ref, (k0, k1, k2)), (vs_ref, (v0, v1, v2))):
        for g, dst in enumerate(dsts):
            d = DILATIONS[g]
            for r in range(d):
                rows = pl.ds(r, tm // d, stride=d) if d > 1 else slice(None)
                for half in range(GROUP_WIDTH // LANES):
                    c = g * (GROUP_WIDTH // LANES) + half
                    dst[r, :, half * LANES:(half + 1) * LANES] = src[c, rows, :].astype(BF16)
    for src, dsts in ((ks_ref, (kw0, kw1, kw2)), (vs_ref, (vw0, vw1, vw2))):
        for g, dst in enumerate(dsts):
            keep = dst.shape[0]
            for half in range(GROUP_WIDTH // LANES):
                c = g * (GROUP_WIDTH // LANES) + half
                dst[:, half * LANES:(half + 1) * LANES] = src[c, tm - keep:, :]


def _prompt_project(x, mod_p, n1, w_in, gq, gk, seg, cos, sa, sb):
    b, s, _ = x.shape
    tm = min(PROMPT_TILE, s)
    nt = s // tm
    row = lambda width: pl.BlockSpec((None, tm, width), lambda bi, i: (bi, i, 0))
    in_specs = [
        row(D_MODEL),
        pl.BlockSpec((None, 1, 6 * D_MODEL), lambda bi, i: (bi, 0, 0)),
        _const_spec((1, D_MODEL)),
        _const_spec((D_MODEL, IN_WIDTH)),
        _const_spec((1, ATTN_WIDTH)),
        _const_spec((1, ATTN_WIDTH)),
        _const_spec((GROUP_WIDTH, GROUP_WIDTH)),
        pl.BlockSpec((tm, LANES), lambda bi, i: (i, 0)),
        pl.BlockSpec((tm, LANES), lambda bi, i: (i, 0)),
        pl.BlockSpec((tm, LANES), lambda bi, i: (i, 0)),
    ]
    out_shape = [jax.ShapeDtypeStruct((b, s, POOL_WIDTH), F32),
                 jax.ShapeDtypeStruct((b, s, D_MODEL), BF16),
                 jax.ShapeDtypeStruct((b, s, D_MODEL), BF16)]
    out_specs = [row(POOL_WIDTH), row(D_MODEL), row(D_MODEL)]
    for _ in range(3):
        for d in DILATIONS:
            out_shape.append(jax.ShapeDtypeStruct((b, d, s // d, GROUP_WIDTH), BF16))
            out_specs.append(pl.BlockSpec((None, d, tm // d, GROUP_WIDTH), lambda bi, i: (bi, 0, i, 0)))
    for _ in range(2):
        for w in WINDOWS:
            keep = min(w, s)
            out_shape.append(jax.ShapeDtypeStruct((b, keep, GROUP_WIDTH), F32))
            if keep >= tm:
                first = (s - keep) // tm
                out_specs.append(pl.BlockSpec(
                    (None, tm, GROUP_WIDTH), lambda bi, i, first=first: (bi, jnp.maximum(i - first, 0), 0)))
            else:
                out_specs.append(pl.BlockSpec((None, keep, GROUP_WIDTH), lambda bi, i: (bi, 0, 0)))
    return pl.pallas_call(
        _prompt_proj_kernel,
        grid=(b, nt),
        in_specs=in_specs,
        out_specs=out_specs,
        out_shape=out_shape,
        scratch_shapes=[pltpu.VMEM((ATTN_WIDTH // LANES, tm, LANES), F32)] * 3,
        compiler_params=_cparams("parallel", "arbitrary"),
        name="prompt_project",
    )(x, mod_p, n1, w_in, gq, gk, seg, cos, sa, sb)


def _sample_proj_kernel(x_ref, mod_ref, n1_ref, w_ref, gq_ref, gk_ref, seg_ref, cos_ref, sa_ref, sb_ref,
                        u_ref, ap_ref, aa_ref, q_ref, k_ref, v_ref):
    u, a_pool, a_attn = _project(x_ref, mod_ref, n1_ref, w_ref, gq_ref, gk_ref, seg_ref, cos_ref, sa_ref, sb_ref,
                                 _lane_store(q_ref), _lane_store(k_ref), _lane_store(v_ref))
    u_ref[...] = u
    ap_ref[...] = a_pool
    aa_ref[...] = a_attn


def _sample_project(x, mod_s, n1, w_in, gq, gk, seg, cos, sa, sb):
    n = x.shape[0]
    full = lambda shape: pl.BlockSpec(shape, lambda i: (0,) * len(shape))
    ins = (x, mod_s, n1, w_in, gq, gk, seg, cos, sa, sb)
    widths = (POOL_WIDTH, D_MODEL, D_MODEL, ATTN_WIDTH, ATTN_WIDTH, ATTN_WIDTH)
    return pl.pallas_call(
        _sample_proj_kernel,
        grid=(1,),
        in_specs=[full(a.shape) for a in ins],
        out_specs=[full((n, w)) for w in widths],
        out_shape=[jax.ShapeDtypeStruct((n, w), F32) for w in widths],
        compiler_params=_cparams("arbitrary"),
        name="sample_project",
    )(*ins)


def _head_lane_masks():
    lane = lax.broadcasted_iota(jnp.int32, (1, GROUP_WIDTH), 1)
    return [(lane // HEAD_DIM) == h for h in range(HEADS_PER_GROUP)]


def _prompt_attn_kernel(*refs):
    ins, o_ref = refs[:15], refs[15]
    kk_refs, vv_refs = refs[16:19], refs[19:22]
    cls_o, cls_l, nat_o, nat_l = refs[22:26]
    t = pl.program_id(1)
    hm = _head_lane_masks()
    row = lax.broadcasted_iota(jnp.int32, (HEADS_PER_GROUP * BLK, 2 * BLK), 0)
    col = lax.broadcasted_iota(jnp.int32, (HEADS_PER_GROUP * BLK, 2 * BLK), 1)
    dist = BLK + (row % BLK) - col
    band = (dist >= 0) & (dist <= N_KEYS)

    for g in range(N_GROUPS):
        d = DILATIONS[g]
        q_ref, kc_ref, kp_ref, vc_ref, vp_ref = ins[5 * g:5 * g + 5]
        kk, vv = kk_refs[g], vv_refs[g]
        rows_per_class = kc_ref.shape[1]
        nb = rows_per_class // BLK
        kk[:, 0:BLK, :] = kp_ref[...]
        kk[:, BLK:, :] = kc_ref[...]
        vv[:, 0:BLK, :] = vp_ref[...]
        vv[:, BLK:, :] = vc_ref[...]
        halves = GROUP_WIDTH // LANES

        def unit(u, carry, q_ref=q_ref, kk=kk, vv=vv, nb=nb):
            r = u // nb
            j = u % nb
            base = pl.multiple_of(j * BLK, BLK)
            q = q_ref[r, pl.ds(base, BLK), :]
            keys = kk[r, pl.ds(base, 2 * BLK), :]
            vals = vv[r, pl.ds(base, 2 * BLK), :]
            zero = jnp.zeros_like(q)
            qst = jnp.concatenate([jnp.where(m, q, zero) for m in hm], axis=0)
            s = lax.dot_general(qst, keys, (((1,), (1,)), ((), ())), preferred_element_type=F32)
            first_col = jnp.where(jnp.logical_or(t > 0, j > 0), 0, BLK)
            s = jnp.where(band & (col >= first_col), s, -jnp.inf)
            m = jnp.max(s, axis=-1, keepdims=True)
            p = jnp.exp(s - m)
            den = jnp.sum(p, axis=-1, keepdims=True)
            o = _dot(p.astype(BF16), vals) / den
            lse = m + jnp.log(den)
            o_u = jnp.zeros((BLK, GROUP_WIDTH), F32)
            l_u = jnp.zeros((BLK, GROUP_WIDTH), F32)
            for h in range(HEADS_PER_GROUP):
                o_u = jnp.where(hm[h], o[h * BLK:(h + 1) * BLK], o_u)
                l_u = jnp.where(hm[h], lse[h * BLK:(h + 1) * BLK], l_u)
            out_rows = pl.ds(pl.multiple_of(u * BLK, BLK), BLK)
            cls_o[out_rows, :] = o_u
            cls_l[out_rows, :] = l_u
            return carry

        lax.fori_loop(0, d * nb, unit, 0)
        for r in range(d):
            rows = pl.ds(r, rows_per_class, stride=d) if d > 1 else slice(None)
            src_rows = slice(r * rows_per_class, (r + 1) * rows_per_class)
            for half in range(halves):
                lanes = slice(half * LANES, (half + 1) * LANES)
                nat_o[g * halves + half, rows, :] = cls_o[src_rows, lanes]
                nat_l[g * halves + half, rows, :] = cls_l[src_rows, lanes]

    chunk = 256
    halves = GROUP_WIDTH // LANES

    def merge(ci, carry):
        rows = pl.ds(pl.multiple_of(ci * chunk, chunk), chunk)
        for half in range(halves):
            ls = [nat_l[g * halves + half, rows, :] for g in range(N_GROUPS)]
            os_ = [nat_o[g * halves + half, rows, :] for g in range(N_GROUPS)]
            m = jnp.maximum(jnp.maximum(ls[0], ls[1]), ls[2])
            es = [jnp.exp(l - m) for l in ls]
            tot = es[0] + es[1] + es[2]
            acc = es[0] * os_[0] + es[1] * os_[1] + es[2] * os_[2]
            o_ref[rows, half * LANES:(half + 1) * LANES] = (acc / tot).astype(o_ref.dtype)
        return carry

    lax.fori_loop(0, o_ref.shape[0] // chunk, merge, 0)


def _prompt_attention(qd, kd, vd):
    b = qd[0].shape[0]
    s = qd[0].shape[2]
    sb = min(SUPER, s)
    nsb = s // sb
    ins, in_specs, scratch_k, scratch_v = [], [], [], []
    for g, d in enumerate(DILATIONS):
        rpc = sb // d
        nb = rpc // BLK
        cur = pl.BlockSpec((None, d, rpc, GROUP_WIDTH), lambda bi, t: (bi, 0, t, 0))
        prev = pl.BlockSpec((None, d, BLK, GROUP_WIDTH),
                            lambda bi, t, nb=nb: (bi, 0, jnp.maximum(t * nb - 1, 0), 0))
        ins += [qd[g], kd[g], kd[g], vd[g], vd[g]]
        in_specs += [cur, cur, prev, cur, prev]
        scratch_k.append(pltpu.VMEM((d, BLK + rpc, GROUP_WIDTH), BF16))
        scratch_v.append(pltpu.VMEM((d, BLK + rpc, GROUP_WIDTH), BF16))
    scratch = scratch_k + scratch_v + [
        pltpu.VMEM((sb, GROUP_WIDTH), F32), pltpu.VMEM((sb, GROUP_WIDTH), F32),
        pltpu.VMEM((N_GROUPS * GROUP_WIDTH // LANES, sb, LANES), F32),
        pltpu.VMEM((N_GROUPS * GROUP_WIDTH // LANES, sb, LANES), F32)]
    return pl.pallas_call(
        _prompt_attn_kernel,
        grid=(b, nsb),
        in_specs=in_specs,
        out_specs=pl.BlockSpec((None, sb, GROUP_WIDTH), lambda bi, t: (bi, t, 0)),
        out_shape=jax.ShapeDtypeStruct((b, s, GROUP_WIDTH), BF16),
        scratch_shapes=scratch,
        compiler_params=_cparams("parallel", "parallel"),
        name="prompt_attention",
    )(*ins)


def _sigmoid(a):
    return 1.0 / (1.0 + jnp.exp(-a))


def _merge_ffn(x, pool_p, attn_y, a_pool, a_attn, mod_ref, n2_ref, wpg_ref, ps_ref, wpb_ref, wab_ref,
               wo_ref, wup_ref, wdn_ref):
    ys = [_dot(p.astype(BF16), wpg_ref[g]) for g, p in enumerate(pool_p)]
    pool_y = jnp.concatenate(ys, axis=-1) * ps_ref[...]
    pb = _dot(pool_y.astype(BF16), wpb_ref[...])
    ab = _dot(attn_y.astype(BF16), wab_ref[...])
    merged = _sigmoid(a_pool.astype(F32)) * pb + _sigmoid(a_attn.astype(F32)) * ab
    g1 = mod_ref[:, 2 * D_MODEL:3 * D_MODEL]
    x1 = x + g1 * _dot(merged.astype(BF16), wo_ref[...])
    h2 = _rms_modulate(x1, n2_ref[...], mod_ref[:, 3 * D_MODEL:4 * D_MODEL], mod_ref[:, 4 * D_MODEL:5 * D_MODEL])
    h2b = h2.astype(BF16)
    ff_chunk = 1024
    y = jnp.zeros_like(x)
    for c in range(D_FF // ff_chunk):
        hid = jnp.maximum(_dot(h2b, wup_ref[:, c * ff_chunk:(c + 1) * ff_chunk]), 0.0)
        y = y + _dot((hid * hid).astype(BF16), wdn_ref[c * ff_chunk:(c + 1) * ff_chunk, :])
    g2 = mod_ref[:, 5 * D_MODEL:6 * D_MODEL]
    return x1 + g2 * y


def _prompt_ffn_kernel(x_ref, y_ref, u_ref, up_ref, ap_ref, aa_ref, mod_ref, n2_ref, wpg_ref, ps_ref,
                       wpb_ref, wab_ref, wo_ref, wup_ref, wdn_ref, o_ref, ext_ref):
    i = pl.program_id(1)
    tm = x_ref.shape[0]
    hist = up_ref.shape[0]
    ext_ref[0:hist, :] = jnp.where(i > 0, up_ref[...], 0.0)
    ext_ref[hist:, :] = u_ref[...]
    pos = (i * tm + lax.broadcasted_iota(jnp.int32, (tm, 1), 0)).astype(F32)
    pool_p = []
    for g, w in enumerate(POOL_WINDOWS):
        e = ext_ref[:, g * POOL_GROUP:(g + 1) * POOL_GROUP]
        acc = e
        span = 1
        while span < w:
            acc = acc + pltpu.roll(acc, span, 0)
            span *= 2
        cnt = jnp.minimum(pos + 1.0, float(w))
        pool_p.append(acc[hist:] / cnt - e[hist:])
    o_ref[...] = _merge_ffn(x_ref[...], pool_p, y_ref[...], ap_ref[...], aa_ref[...], mod_ref, n2_ref, wpg_ref,
                            ps_ref, wpb_ref, wab_ref, wo_ref, wup_ref, wdn_ref)


def _prompt_ffn(x, attn_y, u, a_pool, a_attn, mod_p, n2, wpg, ps, wpb, wab, wo, wup, wdn):
    b, s, _ = x.shape
    tm = min(PROMPT_TILE, s)
    hist = 16
    row = lambda width: pl.BlockSpec((None, tm, width), lambda bi, i: (bi, i, 0))
    in_specs = [
        row(D_MODEL), row(GROUP_WIDTH), row(POOL_WIDTH),
        pl.BlockSpec((None, hist, POOL_WIDTH), lambda bi, i: (bi, jnp.maximum(i * (tm // hist) - 1, 0), 0)),
        row(D_MODEL), row(D_MODEL),
        pl.BlockSpec((None, 1, 6 * D_MODEL), lambda bi, i: (bi, 0, 0)),
        _const_spec(n2.shape), _const_spec(wpg.shape), _const_spec(ps.shape), _const_spec(wpb.shape),
        _const_spec(wab.shape), _const_spec(wo.shape), _const_spec(wup.shape), _const_spec(wdn.shape),
    ]
    return pl.pallas_call(
        _prompt_ffn_kernel,
        grid=(b, s // tm),
        in_specs=in_specs,
        out_specs=row(D_MODEL),
        out_shape=jax.ShapeDtypeStruct((b, s, D_MODEL), F32),
        scratch_shapes=[pltpu.VMEM((hist + tm, POOL_WIDTH), F32)],
        compiler_params=_cparams("parallel", "parallel"),
        name="prompt_merge_ffn",
    )(x, attn_y, u, u, a_pool, a_attn, mod_p, n2, wpg, ps, wpb, wab, wo, wup, wdn)


def _sample_ffn_kernel(x_ref, y_ref, u_ref, st_ref, ap_ref, aa_ref, mod_ref, n2_ref, wpg_ref, ps_ref,
                       wpb_ref, wab_ref, wo_ref, wup_ref, wdn_ref, o_ref):
    n = x_ref.shape[0]
    u = u_ref[...]
    tail = jnp.zeros_like(u)
    sums = {}
    for back in range(1, POOL_HIST + 1):
        row = POOL_HIST - back
        tail = tail + st_ref[:, row * POOL_WIDTH:(row + 1) * POOL_WIDTH]
        if back + 1 in POOL_WINDOWS:
            sums[back + 1] = tail
    pool_p = []
    for g, w in enumerate(POOL_WINDOWS):
        cols = slice(g * POOL_GROUP, (g + 1) * POOL_GROUP)
        cnt = min(PAST_LEN + 1.0, float(w))
        pool_p.append((sums[w][:, cols] + u[:, cols]) / cnt - u[:, cols])
    o_ref[...] = _merge_ffn(x_ref[...], pool_p, y_ref[...], ap_ref[...], aa_ref[...], mod_ref, n2_ref, wpg_ref,
                            ps_ref, wpb_ref, wab_ref, wo_ref, wup_ref, wdn_ref)


def _sample_ffn(layer, x, attn_y, u, state2d, a_pool, a_attn, mod_s, n2, wpg, ps, wpb, wab, wo, wup, wdn):
    n = x.shape[0]
    full = lambda a: pl.BlockSpec(a.shape, lambda i: (0,) * a.ndim)
    ins = (x, attn_y, u, state2d, a_pool, a_attn, mod_s, n2, wpg, ps, wpb, wab, wo, wup, wdn)
    in_specs = [full(a) for a in ins]
    in_specs[3] = pl.BlockSpec((None,) + state2d.shape[1:], lambda i: (layer, 0, 0))
    for k in range(7, len(ins)):
        in_specs[k] = _const_spec(ins[k].shape)
    return pl.pallas_call(
        _sample_ffn_kernel,
        grid=(1,),
        in_specs=in_specs,
        out_specs=pl.BlockSpec((n, D_MODEL), lambda i: (0, 0)),
        out_shape=jax.ShapeDtypeStruct((n, D_MODEL), F32),
        compiler_params=_cparams("arbitrary"),
        name="sample_merge_ffn",
    )(*ins)


def _sample_attn_kernel(q_ref, k_ref, v_ref, ck0, cv0, ck1, cv1, ck2, cv2, o_ref):
    bt = o_ref.shape[0]
    caches = ((ck0, cv0), (ck1, cv1), (ck2, cv2))

    def one_head(i, carry):
        b = i // HEADS_PER_GROUP
        h = i % HEADS_PER_GROUP
        outs, lses = [], []
        for g, (ck, cv) in enumerate(caches):
            d = DILATIONS[g]
            q = q_ref[b, g, h]
            s = jnp.sum(ck[b, h] * q, axis=0, keepdims=True)
            if d > 1:
                row = lax.broadcasted_iota(jnp.int32, s.shape, 1)
                s = jnp.where(row % d == 0, s, -jnp.inf)
            s_new = jnp.sum(k_ref[b, g, h] * q, axis=0, keepdims=True)
            m = jnp.maximum(jnp.max(s, axis=1, keepdims=True), s_new)
            p = jnp.exp(s - m)
            p_new = jnp.exp(s_new - m)
            den = jnp.sum(p, axis=1, keepdims=True) + p_new
            acc = jnp.sum(cv[b, h] * p, axis=1, keepdims=True) + p_new * v_ref[b, g, h]
            outs.append(acc / den)
            lses.append(m + jnp.log(den))
        mx = jnp.maximum(jnp.maximum(lses[0], lses[1]), lses[2])
        es = [jnp.exp(l - mx) for l in lses]
        o_ref[b, h] = (es[0] * outs[0] + es[1] * outs[1] + es[2] * outs[2]) / (es[0] + es[1] + es[2])
        return carry

    lax.fori_loop(0, bt * HEADS_PER_GROUP, one_head, 0)


def _sample_attention(layer, q, k, v, cache_views):
    n = q.shape[0]
    bt = 2
    new = pl.BlockSpec((bt, N_GROUPS, HEADS_PER_GROUP, HEAD_DIM, 1), lambda i: (i, 0, 0, 0, 0))
    in_specs = [new, new, new]
    for c in cache_views:
        in_specs.append(pl.BlockSpec((None, bt) + c.shape[2:], lambda i: (layer, i, 0, 0, 0)))
    return pl.pallas_call(
        _sample_attn_kernel,
        grid=(n // bt,),
        in_specs=in_specs,
        out_specs=pl.BlockSpec((bt, HEADS_PER_GROUP, HEAD_DIM, 1), lambda i: (i, 0, 0, 0)),
        out_shape=jax.ShapeDtypeStruct((n, HEADS_PER_GROUP, HEAD_DIM, 1), F32),
        compiler_params=_cparams("parallel"),
        name="sample_attention",
    )(q, k, v, *cache_views)


def _rope_tables(pos):
    inv_freq = ROPE_THETA ** (-jnp.arange(0, HEAD_DIM, 2, dtype=F32) / HEAD_DIM)
    ang = pos[:, None] * inv_freq[None, :]
    cos, sin = jnp.cos(ang), jnp.sin(ang)
    zero = jnp.zeros_like(sin)
    reps = LANES // HEAD_DIM
    cos_t = jnp.tile(jnp.concatenate([cos, cos], axis=-1), (1, reps))
    sa_t = jnp.tile(jnp.concatenate([-sin, zero], axis=-1), (1, reps))
    sb_t = jnp.tile(jnp.concatenate([zero, sin], axis=-1), (1, reps))
    return cos_t, sa_t, sb_t


def _segment_matrix():
    lane = jnp.arange(GROUP_WIDTH)
    seg = (lane[:, None] // HEAD_DIM == lane[None, :] // HEAD_DIM).astype(F32) / HEAD_DIM
    return seg.astype(BF16)


def kernel(x_prompt, x_sample, cache_k_w128, cache_v_w128, cache_k_w512, cache_v_w512, cache_k_w2048,
           cache_v_w2048, state_pool, c_prompt, c_sample, norm1_g, norm2_g, w_ada, b_ada, w_in, q_norm_g,
           k_norm_g, w_pool_grp, pool_scale, w_pool_br, w_attn_br, w_out, w_up, w_down):
    depth = w_in.shape[0]
    b, s, _ = x_prompt.shape
    n = x_sample.shape[0]
    assert x_sample.shape[1] == 1 and s % min(SUPER, s) == 0 and s % min(PROMPT_TILE, s) == 0

    mod = _modulation(jnp.concatenate([c_prompt, c_sample], axis=0), w_ada, b_ada)
    seg = _segment_matrix()
    tab_p = _rope_tables(jnp.arange(s, dtype=F32))
    tab_s = _rope_tables(PAST_LEN + jnp.arange(1, dtype=F32))

    caches = []
    for ck, cv, d in zip((cache_k_w128, cache_k_w512, cache_k_w2048),
                         (cache_v_w128, cache_v_w512, cache_v_w2048), DILATIONS):
        for c in (ck, cv):
            assert c.shape[2] == N_KEYS * d
            caches.append(jnp.transpose(c, (0, 1, 3, 4, 2)))
    state2d = state_pool.reshape(depth, n, POOL_HIST * POOL_WIDTH)

    bf = lambda a: a.astype(BF16)
    w_in_b, wpg_b, wpb_b, wab_b, wo_b, wup_b, wdn_b = map(
        bf, (w_in, w_pool_grp, w_pool_br, w_attn_br, w_out, w_up, w_down))

    xp = x_prompt
    xs = x_sample.reshape(n, D_MODEL)
    kp = [[] for _ in range(N_GROUPS)]
    vp = [[] for _ in range(N_GROUPS)]
    ks = [[] for _ in range(N_GROUPS)]
    vs = [[] for _ in range(N_GROUPS)]
    pool_p, pool_s = [], []
    for l in range(depth):
        n1 = norm1_g[l].reshape(1, D_MODEL)
        n2 = norm2_g[l].reshape(1, D_MODEL)
        gq = jnp.tile(q_norm_g[l], ATTN_WIDTH // HEAD_DIM).reshape(1, ATTN_WIDTH)
        gk = jnp.tile(k_norm_g[l], ATTN_WIDTH // HEAD_DIM).reshape(1, ATTN_WIDTH)
        ps = pool_scale[l].reshape(1, POOL_WIDTH)
        mod_p = mod[l, :b].reshape(b, 1, 6 * D_MODEL)
        mod_s = mod[l, b:]
        tail_w = (n2, wpg_b[l], ps, wpb_b[l], wab_b[l], wo_b[l], wup_b[l], wdn_b[l])

        outs = _prompt_project(xp, mod_p, n1, w_in_b[l], gq, gk, seg, *tab_p)
        u, a_pool, a_attn = outs[0:3]
        qd, kd, vd = outs[3:6], outs[6:9], outs[9:12]
        kws, vws = outs[12:15], outs[15:18]
        attn_y = _prompt_attention(qd, kd, vd)
        xp = _prompt_ffn(xp, attn_y, u, a_pool, a_attn, mod_p, *tail_w)
        for g in range(N_GROUPS):
            kp[g].append(kws[g].reshape(b, -1, HEADS_PER_GROUP, HEAD_DIM))
            vp[g].append(vws[g].reshape(b, -1, HEADS_PER_GROUP, HEAD_DIM))
        pool_p.append(u[:, s - POOL_HIST:, :])

        u_s, ap_s, aa_s, q_s, k_s, v_s = _sample_project(xs, mod_s, n1, w_in_b[l], gq, gk, seg, *tab_s)
        qh, kh, vh = (a.reshape(n, N_GROUPS, HEADS_PER_GROUP, HEAD_DIM) for a in (q_s, k_s, v_s))
        y_s = _sample_attention(l, qh[..., None], kh[..., None], vh[..., None], caches).reshape(n, GROUP_WIDTH)
        xs = _sample_ffn(l, xs, y_s, u_s, state2d, ap_s, aa_s, mod_s, *tail_w)
        for g in range(N_GROUPS):
            ks[g].append(kh[:, g:g + 1])
            vs[g].append(vh[:, g:g + 1])
        pool_s.append(jnp.concatenate([state_pool[l][:, 1:], u_s[:, None, :]], axis=1))

    st = lambda rows: jnp.stack(rows, axis=0)
    return (xp, xs.reshape(n, 1, D_MODEL),
            st(kp[0]), st(vp[0]), st(kp[1]), st(vp[1]), st(kp[2]), st(vp[2]), st(pool_p),
            st(ks[0]), st(vs[0]), st(ks[1]), st(vs[1]), st(ks[2]), st(vs[2]), st(pool_s))
```

```python
import functools
import math

import jax
import jax.numpy as jnp
from jax import lax
from jax.experimental import pallas as pl
from jax.experimental.pallas import tpu as pltpu

F32 = jnp.float32
BF16 = jnp.bfloat16

D_MODEL = 1024
DEPTH = 4
PAST_LEN = 8192
POOL_WIDTH = 512
POOL_WINDOWS = (2, 4, 8, 16)
POOL_GROUP = 128
POOL_HIST = 15
HEAD_DIM = 64
HEADS_PER_GROUP = 4
GROUP_WIDTH = HEADS_PER_GROUP * HEAD_DIM
DILATIONS = (1, 4, 16)
WINDOWS = (128, 512, 2048)
N_GROUPS = 3
ATTN_WIDTH = N_GROUPS * GROUP_WIDTH
N_KEYS = 128
BLK = 128
D_FF = 4096
ROPE_THETA = 10000.0
EPS = 1e-6
Q_SCALE = 1.0 / math.sqrt(HEAD_DIM)

COL_U = 0
COL_Q = POOL_WIDTH
COL_K = COL_Q + ATTN_WIDTH
COL_V = COL_K + ATTN_WIDTH
COL_AP = COL_V + ATTN_WIDTH
COL_AA = COL_AP + D_MODEL
IN_WIDTH = COL_AA + D_MODEL

LANES = 128
PROMPT_TILE = 512
SUPER = BLK * DILATIONS[-1]
VMEM_LIMIT = 56 * 1024 * 1024


def _cparams(*sem):
    return pltpu.CompilerParams(dimension_semantics=sem, vmem_limit_bytes=VMEM_LIMIT)


def _const_spec(shape):
    nd = len(shape)
    return pl.BlockSpec(shape, lambda *_: (0,) * nd, pipeline_mode=pl.Buffered(1))


def _layer_spec(arr, layer):
    nd = arr.ndim - 1
    return pl.BlockSpec((None,) + arr.shape[1:], lambda *_: (layer,) + (0,) * nd, pipeline_mode=pl.Buffered(1))


def _dot(a, b):
    return jnp.dot(a, b, preferred_element_type=F32)


def _rms_modulate(x, gain, shift, scale):
    ms = jnp.mean(x * x, axis=-1, keepdims=True)
    return (x * lax.rsqrt(ms + EPS) * gain) * (1.0 + scale) + shift


def _mod_kernel(cp_ref, cs_ref, w_ref, b_ref, op_ref, os_ref):
    w = w_ref[...].astype(BF16)

    def mod(c):
        s = c * (1.0 / (1.0 + jnp.exp(-c)))
        return _dot(s.astype(BF16), w) + b_ref[...]

    mp = mod(cp_ref[...])
    for bi in range(op_ref.shape[0]):
        op_ref[bi] = mp[bi:bi + 1, :]
    os_ref[...] = mod(cs_ref[...])


def _modulation(c_prompt, c_sample, w_ada, b_ada):
    b, n = c_prompt.shape[0], c_sample.shape[0]
    depth = w_ada.shape[0]
    tn = 1536
    return pl.pallas_call(
        _mod_kernel,
        grid=(depth, 6 * D_MODEL // tn),
        in_specs=[
            pl.BlockSpec((b, D_MODEL), lambda l, j: (0, 0)),
            pl.BlockSpec((n, D_MODEL), lambda l, j: (0, 0)),
            pl.BlockSpec((None, D_MODEL, tn), lambda l, j: (l, 0, j)),
            pl.BlockSpec((None, 1, tn), lambda l, j: (l, 0, j)),
        ],
        out_specs=[pl.BlockSpec((None, b, 1, tn), lambda l, j: (l, 0, 0, j)),
                   pl.BlockSpec((None, n, tn), lambda l, j: (l, 0, j))],
        out_shape=[jax.ShapeDtypeStruct((depth, b, 1, 6 * D_MODEL), F32),
                   jax.ShapeDtypeStruct((depth, n, 6 * D_MODEL), F32)],
        compiler_params=_cparams("parallel", "parallel"),
        name="adaln_mod",
    )(c_prompt, c_sample, w_ada, b_ada.reshape(depth, 1, 6 * D_MODEL))


def _head_norm_rope(z, gain_ref, seg_ref, cos_ref, sa_ref, sb_ref, out_scale, store):
    cos = cos_ref[...]
    sa = sa_ref[...]
    sb = sb_ref[...]
    for c in range(ATTN_WIDTH // GROUP_WIDTH):
        zc = z[:, c * GROUP_WIDTH:(c + 1) * GROUP_WIDTH]
        ms = _dot((zc * zc).astype(BF16), seg_ref[...])
        n = zc * lax.rsqrt(ms + EPS) * gain_ref[:, c * GROUP_WIDTH:(c + 1) * GROUP_WIDTH]
        for half in range(GROUP_WIDTH // LANES):
            xx = n[:, half * LANES:(half + 1) * LANES]
            r = xx * cos + pltpu.roll(xx, LANES - HEAD_DIM // 2, 1) * sa + pltpu.roll(xx, HEAD_DIM // 2, 1) * sb
            store(c * (GROUP_WIDTH // LANES) + half, r * out_scale if out_scale != 1.0 else r)


def _project(x_ref, mod_ref, n1_ref, w_ref, gq_ref, gk_ref, seg_ref, cos_ref, sa_ref, sb_ref,
             store_q, store_k, store_v):
    x = x_ref[...]
    h = _rms_modulate(x, n1_ref[...], mod_ref[:, 0:D_MODEL], mod_ref[:, D_MODEL:2 * D_MODEL])
    hb = h.astype(BF16)
    u = _dot(hb, w_ref[:, COL_U:COL_Q])
    a_pool = _dot(hb, w_ref[:, COL_AP:COL_AA])
    a_attn = _dot(hb, w_ref[:, COL_AA:IN_WIDTH])
    _head_norm_rope(_dot(hb, w_ref[:, COL_Q:COL_K]), gq_ref, seg_ref, cos_ref, sa_ref, sb_ref, Q_SCALE, store_q)
    _head_norm_rope(_dot(hb, w_ref[:, COL_K:COL_V]), gk_ref, seg_ref, cos_ref, sa_ref, sb_ref, 1.0, store_k)
    v = _dot(hb, w_ref[:, COL_V:COL_AP])
    for c in range(ATTN_WIDTH // LANES):
        store_v(c, v[:, c * LANES:(c + 1) * LANES])
    return u, a_pool, a_attn


def _chunk_store(ref):
    def store(c, val):
        ref[c] = val
    return store


def _lane_store(ref, rows):
    def store(c, val):
        ref[0:rows, c * LANES:(c + 1) * LANES] = val
    return store


def _prompt_proj_kernel(x_ref, mod_ref, n1_ref, w_ref, gq_ref, gk_ref, seg_ref, cos_ref, sa_ref, sb_ref,
                        u_ref, ap_ref, aa_ref,
                        q0, q1, q2, k0, k1, k2, v0, v1, v2,
                        kw0, kw1, kw2, vw0, vw1, vw2,
                        qs_ref, ks_ref, vs_ref):
    u, a_pool, a_attn = _project(x_ref, mod_ref, n1_ref, w_ref, gq_ref, gk_ref, seg_ref, cos_ref, sa_ref, sb_ref,
                                 _chunk_store(qs_ref), _chunk_store(ks_ref), _chunk_store(vs_ref))
    u_ref[...] = u
    ap_ref[...] = a_pool.astype(BF16)
    aa_ref[...] = a_attn.astype(BF16)
    tm = x_ref.shape[0]
    for src, dsts in ((qs_ref, (q0, q1, q2)), (ks_ref, (k0, k1, k2)), (vs_ref, (v0, v1, v2))):
        for g, dst in enumerate(dsts):
            d = DILATIONS[g]
            for r in range(d):
                rows = pl.ds(r, tm // d, stride=d) if d > 1 else slice(None)
                for half in range(GROUP_WIDTH // LANES):
                    c = g * (GROUP_WIDTH // LANES) + half
                    dst[r, :, half * LANES:(half + 1) * LANES] = src[c, rows, :].astype(BF16)
    for src, dsts in ((ks_ref, (kw0, kw1, kw2)), (vs_ref, (vw0, vw1, vw2))):
        for g, dst in enumerate(dsts):
            keep = dst.shape[0]
            for half in range(GROUP_WIDTH // LANES):
                c = g * (GROUP_WIDTH // LANES) + half
                dst[:, half * LANES:(half + 1) * LANES] = src[c, tm - keep:, :]


def _prompt_project(layer, x, mod_p, n1, w_in, gq, gk, seg, cos, sa, sb):
    b, s, _ = x.shape
    tm = min(PROMPT_TILE, s)
    nt = s // tm
    row = lambda width: pl.BlockSpec((None, tm, width), lambda bi, i: (bi, i, 0))
    in_specs = [
        row(D_MODEL),
        pl.BlockSpec((None, None, 1, 6 * D_MODEL), lambda bi, i: (layer, bi, 0, 0)),
        _layer_spec(n1, layer),
        _layer_spec(w_in, layer),
        _layer_spec(gq, layer),
        _layer_spec(gk, layer),
        _const_spec((GROUP_WIDTH, GROUP_WIDTH)),
        pl.BlockSpec((tm, LANES), lambda bi, i: (i, 0)),
        pl.BlockSpec((tm, LANES), lambda bi, i: (i, 0)),
        pl.BlockSpec((tm, LANES), lambda bi, i: (i, 0)),
    ]
    out_shape = [jax.ShapeDtypeStruct((b, s, POOL_WIDTH), F32),
                 jax.ShapeDtypeStruct((b, s, D_MODEL), BF16),
                 jax.ShapeDtypeStruct((b, s, D_MODEL), BF16)]
    out_specs = [row(POOL_WIDTH), row(D_MODEL), row(D_MODEL)]
    for _ in range(3):
        for d in DILATIONS:
            out_shape.append(jax.ShapeDtypeStruct((b, d, s // d, GROUP_WIDTH), BF16))
            out_specs.append(pl.BlockSpec((None, d, tm // d, GROUP_WIDTH), lambda bi, i: (bi, 0, i, 0)))
    for _ in range(2):
        for w in WINDOWS:
            keep = min(w, s)
            out_shape.append(jax.ShapeDtypeStruct((b, keep, GROUP_WIDTH), F32))
            if keep >= tm:
                first = (s - keep) // tm
                out_specs.append(pl.BlockSpec(
                    (None, tm, GROUP_WIDTH), lambda bi, i, first=first: (bi, jnp.maximum(i - first, 0), 0)))
            else:
                out_specs.append(pl.BlockSpec((None, keep, GROUP_WIDTH), lambda bi, i: (bi, 0, 0)))
    return pl.pallas_call(
        _prompt_proj_kernel,
        grid=(b, nt),
        in_specs=in_specs,
        out_specs=out_specs,
        out_shape=out_shape,
        scratch_shapes=[pltpu.VMEM((ATTN_WIDTH // LANES, tm, LANES), F32)] * 3,
        compiler_params=_cparams("parallel", "arbitrary"),
        name="prompt_project",
    )(x, mod_p, n1, w_in, gq, gk, seg, cos, sa, sb)


def _sample_proj_kernel(x_ref, mod_ref, n1_ref, w_ref, gq_ref, gk_ref, seg_ref, cos_ref, sa_ref, sb_ref,
                        u_ref, ap_ref, aa_ref, k_ref, v_ref, qt_ref, kt_ref, vt_ref, q_sc, k_sc, v_sc):
    n = x_ref.shape[0]
    for sc in (q_sc, k_sc, v_sc):
        sc[...] = jnp.zeros_like(sc)
    u, a_pool, a_attn = _project(x_ref, mod_ref, n1_ref, w_ref, gq_ref, gk_ref, seg_ref, cos_ref, sa_ref, sb_ref,
                                 _lane_store(q_sc, n), _lane_store(k_sc, n), _lane_store(v_sc, n))
    u_ref[...] = u
    ap_ref[...] = a_pool
    aa_ref[...] = a_attn
    k_ref[...] = k_sc[0:n, :]
    v_ref[...] = v_sc[0:n, :]
    for sc, dst in ((q_sc, qt_ref), (k_sc, kt_ref), (v_sc, vt_ref)):
        for c in range(ATTN_WIDTH // LANES):
            dst[c * LANES:(c + 1) * LANES, :] = sc[:, c * LANES:(c + 1) * LANES].T


def _sample_project(layer, x, mod_s, n1, w_in, gq, gk, seg, cos, sa, sb):
    n = x.shape[0]
    assert n <= LANES
    full = lambda shape: pl.BlockSpec(shape, lambda i: (0,) * len(shape))
    in_specs = [full(x.shape), _layer_spec(mod_s, layer), _layer_spec(n1, layer), _layer_spec(w_in, layer),
                _layer_spec(gq, layer), _layer_spec(gk, layer), full(seg.shape), full(cos.shape), full(sa.shape),
                full(sb.shape)]
    shapes = [(n, POOL_WIDTH), (n, D_MODEL), (n, D_MODEL), (n, ATTN_WIDTH), (n, ATTN_WIDTH)] + [(ATTN_WIDTH, LANES)] * 3
    return pl.pallas_call(
        _sample_proj_kernel,
        grid=(1,),
        in_specs=in_specs,
        out_specs=[full(sh) for sh in shapes],
        out_shape=[jax.ShapeDtypeStruct(sh, F32) for sh in shapes],
        scratch_shapes=[pltpu.VMEM((LANES, ATTN_WIDTH), F32)] * 3,
        compiler_params=_cparams("arbitrary"),
        name="sample_project",
    )(x, mod_s, n1, w_in, gq, gk, seg, cos, sa, sb)


def _head_lane_masks():
    lane = lax.broadcasted_iota(jnp.int32, (1, GROUP_WIDTH), 1)
    return [(lane // HEAD_DIM) == h for h in range(HEADS_PER_GROUP)]


def _prompt_attn_kernel(*refs):
    ins, o_ref = refs[:15], refs[15]
    kk_refs, vv_refs = refs[16:19], refs[19:22]
    cls_o, cls_l, nat_o, nat_l = refs[22:26]
    t = pl.program_id(1)
    hm = _head_lane_masks()
    row = lax.broadcasted_iota(jnp.int32, (HEADS_PER_GROUP * BLK, 2 * BLK), 0)
    col = lax.broadcasted_iota(jnp.int32, (HEADS_PER_GROUP * BLK, 2 * BLK), 1)
    dist = BLK + (row % BLK) - col
    band = (dist >= 0) & (dist <= N_KEYS)

    for g in range(N_GROUPS):
        d = DILATIONS[g]
        q_ref, kc_ref, kp_ref, vc_ref, vp_ref = ins[5 * g:5 * g + 5]
        kk, vv = kk_refs[g], vv_refs[g]
        rows_per_class = kc_ref.shape[1]
        nb = rows_per_class // BLK
        kk[:, 0:BLK, :] = kp_ref[...]
        kk[:, BLK:, :] = kc_ref[...]
        vv[:, 0:BLK, :] = vp_ref[...]
        vv[:, BLK:, :] = vc_ref[...]
        halves = GROUP_WIDTH // LANES

        def unit(u, carry, q_ref=q_ref, kk=kk, vv=vv, nb=nb):
            r = u // nb
            j = u % nb
            base = pl.multiple_of(j * BLK, BLK)
            q = q_ref[r, pl.ds(base, BLK), :]
            keys = kk[r, pl.ds(base, 2 * BLK), :]
            vals = vv[r, pl.ds(base, 2 * BLK), :]
            zero = jnp.zeros_like(q)
            qst = jnp.concatenate([jnp.where(m, q, zero) for m in hm], axis=0)
            s = lax.dot_general(qst, keys, (((1,), (1,)), ((), ())), preferred_element_type=F32)
            first_col = jnp.where(jnp.logical_or(t > 0, j > 0), 0, BLK)
            s = jnp.where(band & (col >= first_col), s, -jnp.inf)
            m = jnp.max(s, axis=-1, keepdims=True)
            p = jnp.exp(s - m)
            den = jnp.sum(p, axis=-1, keepdims=True)
            o = _dot(p.astype(BF16), vals) / den
            lse = m + jnp.log(den)
            o_u = jnp.zeros((BLK, GROUP_WIDTH), F32)
            l_u = jnp.zeros((BLK, GROUP_WIDTH), F32)
            for h in range(HEADS_PER_GROUP):
                o_u = jnp.where(hm[h], o[h * BLK:(h + 1) * BLK], o_u)
                l_u = jnp.where(hm[h], lse[h * BLK:(h + 1) * BLK], l_u)
            out_rows = pl.ds(pl.multiple_of(u * BLK, BLK), BLK)
            cls_o[out_rows, :] = o_u
            cls_l[out_rows, :] = l_u
            return carry

        lax.fori_loop(0, d * nb, unit, 0, unroll=2)
        for r in range(d):
            rows = pl.ds(r, rows_per_class, stride=d) if d > 1 else slice(None)
            src_rows = slice(r * rows_per_class, (r + 1) * rows_per_class)
            for half in range(halves):
                lanes = slice(half * LANES, (half + 1) * LANES)
                nat_o[g * halves + half, rows, :] = cls_o[src_rows, lanes]
                nat_l[g * halves + half, rows, :] = cls_l[src_rows, lanes]

    chunk = 256
    halves = GROUP_WIDTH // LANES

    def merge(ci, carry):
        rows = pl.ds(pl.multiple_of(ci * chunk, chunk), chunk)
        for half in range(halves):
            ls = [nat_l[g * halves + half, rows, :] for g in range(N_GROUPS)]
            os_ = [nat_o[g * halves + half, rows, :] for g in range(N_GROUPS)]
            m = jnp.maximum(jnp.maximum(ls[0], ls[1]), ls[2])
            es = [jnp.exp(l - m) for l in ls]
            tot = es[0] + es[1] + es[2]
            acc = es[0] * os_[0] + es[1] * os_[1] + es[2] * os_[2]
            o_ref[rows, half * LANES:(half + 1) * LANES] = (acc / tot).astype(o_ref.dtype)
        return carry

    lax.fori_loop(0, o_ref.shape[0] // chunk, merge, 0)


def _prompt_attention(qd, kd, vd):
    b = qd[0].shape[0]
    s = qd[0].shape[2]
    sb = min(SUPER, s)
    nsb = s // sb
    ins, in_specs, scratch_k, scratch_v = [], [], [], []
    for g, d in enumerate(DILATIONS):
        rpc = sb // d
        nb = rpc // BLK
        cur = pl.BlockSpec((None, d, rpc, GROUP_WIDTH), lambda bi, t: (bi, 0, t, 0))
        prev = pl.BlockSpec((None, d, BLK, GROUP_WIDTH),
                            lambda bi, t, nb=nb: (bi, 0, jnp.maximum(t * nb - 1, 0), 0))
        ins += [qd[g], kd[g], kd[g], vd[g], vd[g]]
        in_specs += [cur, cur, prev, cur, prev]
        scratch_k.append(pltpu.VMEM((d, BLK + rpc, GROUP_WIDTH), BF16))
        scratch_v.append(pltpu.VMEM((d, BLK + rpc, GROUP_WIDTH), BF16))
    scratch = scratch_k + scratch_v + [
        pltpu.VMEM((sb, GROUP_WIDTH), F32), pltpu.VMEM((sb, GROUP_WIDTH), F32),
        pltpu.VMEM((N_GROUPS * GROUP_WIDTH // LANES, sb, LANES), F32),
        pltpu.VMEM((N_GROUPS * GROUP_WIDTH // LANES, sb, LANES), F32)]
    return pl.pallas_call(
        _prompt_attn_kernel,
        grid=(b, nsb),
        in_specs=in_specs,
        out_specs=pl.BlockSpec((None, sb, GROUP_WIDTH), lambda bi, t: (bi, t, 0)),
        out_shape=jax.ShapeDtypeStruct((b, s, GROUP_WIDTH), BF16),
        scratch_shapes=scratch,
        compiler_params=_cparams("parallel", "parallel"),
        name="prompt_attention",
    )(*ins)


def _sigmoid(a):
    return 1.0 / (1.0 + jnp.exp(-a))


def _merge_ffn(x, pool_p, attn_y, a_pool, a_attn, mod_ref, n2_ref, wpg_ref, ps_ref, wpb_ref, wab_ref,
               wo_ref, wup_ref, wdn_ref):
    ys = [_dot(p.astype(BF16), wpg_ref[g]) for g, p in enumerate(pool_p)]
    pool_y = jnp.concatenate(ys, axis=-1) * ps_ref[...]
    pb = _dot(pool_y.astype(BF16), wpb_ref[...])
    ab = _dot(attn_y.astype(BF16), wab_ref[...])
    merged = _sigmoid(a_pool.astype(F32)) * pb + _sigmoid(a_attn.astype(F32)) * ab
    g1 = mod_ref[:, 2 * D_MODEL:3 * D_MODEL]
    x1 = x + g1 * _dot(merged.astype(BF16), wo_ref[...])
    h2 = _rms_modulate(x1, n2_ref[...], mod_ref[:, 3 * D_MODEL:4 * D_MODEL], mod_ref[:, 4 * D_MODEL:5 * D_MODEL])
    h2b = h2.astype(BF16)
    ff_chunk = 1024
    y = jnp.zeros_like(x)
    for c in range(D_FF // ff_chunk):
        hid = jnp.maximum(_dot(h2b, wup_ref[:, c * ff_chunk:(c + 1) * ff_chunk]), 0.0)
        y = y + _dot((hid * hid).astype(BF16), wdn_ref[c * ff_chunk:(c + 1) * ff_chunk, :])
    g2 = mod_ref[:, 5 * D_MODEL:6 * D_MODEL]
    return x1 + g2 * y


def _prompt_ffn_kernel(x_ref, y_ref, u_ref, up_ref, ap_ref, aa_ref, mod_ref, n2_ref, wpg_ref, ps_ref,
                       wpb_ref, wab_ref, wo_ref, wup_ref, wdn_ref, o_ref, ext_ref):
    i = pl.program_id(1)
    tm = x_ref.shape[0]
    hist = up_ref.shape[0]
    ext_ref[0:hist, :] = jnp.where(i > 0, up_ref[...], 0.0)
    ext_ref[hist:, :] = u_ref[...]
    pos = (i * tm + lax.broadcasted_iota(jnp.int32, (tm, 1), 0)).astype(F32)
    pool_p = []
    for g, w in enumerate(POOL_WINDOWS):
        e = ext_ref[:, g * POOL_GROUP:(g + 1) * POOL_GROUP]
        acc = e
        span = 1
        while span < w:
            acc = acc + pltpu.roll(acc, span, 0)
            span *= 2
        cnt = jnp.minimum(pos + 1.0, float(w))
        pool_p.append(acc[hist:] / cnt - e[hist:])
    o_ref[...] = _merge_ffn(x_ref[...], pool_p, y_ref[...], ap_ref[...], aa_ref[...], mod_ref, n2_ref, wpg_ref,
                            ps_ref, wpb_ref, wab_ref, wo_ref, wup_ref, wdn_ref)


def _prompt_ffn(layer, x, attn_y, u, a_pool, a_attn, mod_p, n2, wpg, ps, wpb, wab, wo, wup, wdn):
    b, s, _ = x.shape
    tm = min(PROMPT_TILE, s)
    hist = 16
    row = lambda width: pl.BlockSpec((None, tm, width), lambda bi, i: (bi, i, 0))
    in_specs = [
        row(D_MODEL), row(GROUP_WIDTH), row(POOL_WIDTH),
        pl.BlockSpec((None, hist, POOL_WIDTH), lambda bi, i: (bi, jnp.maximum(i * (tm // hist) - 1, 0), 0)),
        row(D_MODEL), row(D_MODEL),
        pl.BlockSpec((None, None, 1, 6 * D_MODEL), lambda bi, i: (layer, bi, 0, 0)),
    ] + [_layer_spec(w, layer) for w in (n2, wpg, ps, wpb, wab, wo, wup, wdn)]
    return pl.pallas_call(
        _prompt_ffn_kernel,
        grid=(b, s // tm),
        in_specs=in_specs,
        out_specs=row(D_MODEL),
        out_shape=jax.ShapeDtypeStruct((b, s, D_MODEL), F32),
        scratch_shapes=[pltpu.VMEM((hist + tm, POOL_WIDTH), F32)],
        compiler_params=_cparams("parallel", "parallel"),
        name="prompt_merge_ffn",
    )(x, attn_y, u, u, a_pool, a_attn, mod_p, n2, wpg, ps, wpb, wab, wo, wup, wdn)


def _sample_ffn_kernel(x_ref, yt_ref, u_ref, st_ref, ap_ref, aa_ref, mod_ref, n2_ref, wpg_ref, ps_ref,
                       wpb_ref, wab_ref, wo_ref, wup_ref, wdn_ref, o_ref):
    n = x_ref.shape[0]
    u = u_ref[...]
    attn_y = yt_ref[...].T[0:n, :]
    tail = jnp.zeros_like(u)
    sums = {}
    for back in range(1, POOL_HIST + 1):
        row = POOL_HIST - back
        tail = tail + st_ref[:, row * POOL_WIDTH:(row + 1) * POOL_WIDTH]
        if back + 1 in POOL_WINDOWS:
            sums[back + 1] = tail
    pool_p = []
    for g, w in enumerate(POOL_WINDOWS):
        cols = slice(g * POOL_GROUP, (g + 1) * POOL_GROUP)
        cnt = min(PAST_LEN + 1.0, float(w))
        pool_p.append((sums[w][:, cols] + u[:, cols]) / cnt - u[:, cols])
    o_ref[...] = _merge_ffn(x_ref[...], pool_p, attn_y, ap_ref[...], aa_ref[...], mod_ref, n2_ref, wpg_ref,
                            ps_ref, wpb_ref, wab_ref, wo_ref, wup_ref, wdn_ref)


def _sample_ffn(layer, x, attn_yt, u, state2d, a_pool, a_attn, mod_s, n2, wpg, ps, wpb, wab, wo, wup, wdn):
    n = x.shape[0]
    full = lambda a: pl.BlockSpec(a.shape, lambda i: (0,) * a.ndim)
    ins = (x, attn_yt, u, state2d, a_pool, a_attn, mod_s, n2, wpg, ps, wpb, wab, wo, wup, wdn)
    in_specs = [full(a) for a in ins[:3]] + [_layer_spec(state2d, layer)] + [full(a) for a in ins[4:6]]
    in_specs += [_layer_spec(a, layer) for a in ins[6:]]
    return pl.pallas_call(
        _sample_ffn_kernel,
        grid=(1,),
        in_specs=in_specs,
        out_specs=pl.BlockSpec((n, D_MODEL), lambda i: (0, 0)),
        out_shape=jax.ShapeDtypeStruct((n, D_MODEL), F32),
        compiler_params=_cparams("arbitrary"),
        name="sample_merge_ffn",
    )(*ins)


def _sample_attn_kernel(q_ref, k_ref, v_ref, ck0, cv0, ck1, cv1, ck2, cv2, o_ref):
    step = pl.program_id(0)
    bt = ck0.shape[0]
    caches = ((ck0, cv0), (ck1, cv1), (ck2, cv2))
    lane = lax.broadcasted_iota(jnp.int32, (HEAD_DIM, LANES), 1)

    @pl.when(step == 0)
    def _():
        o_ref[...] = jnp.zeros_like(o_ref)

    def one_head(i, carry):
        b = i // HEADS_PER_GROUP
        h = i % HEADS_PER_GROUP
        mine = lane == step * bt + b
        column = lambda ref, g: jnp.sum(jnp.where(mine, ref[g, h], 0.0), axis=1, keepdims=True)
        outs, lses = [], []
        for g, (ck, cv) in enumerate(caches):
            d = DILATIONS[g]
            q = column(q_ref, g)
            s = jnp.sum(ck[b, h] * q, axis=0, keepdims=True)
            if d > 1:
                row = lax.broadcasted_iota(jnp.int32, s.shape, 1)
                s = jnp.where(row % d == 0, s, -jnp.inf)
            s_new = jnp.sum(column(k_ref, g) * q, axis=0, keepdims=True)
            m = jnp.maximum(jnp.max(s, axis=1, keepdims=True), s_new)
            p = jnp.exp(s - m)
            p_new = jnp.exp(s_new - m)
            den = jnp.sum(p, axis=1, keepdims=True) + p_new
            acc = jnp.sum(cv[b, h] * p, axis=1, keepdims=True) + p_new * column(v_ref, g)
            outs.append(acc / den)
            lses.append(m + jnp.log(den))
        mx = jnp.maximum(jnp.maximum(lses[0], lses[1]), lses[2])
        es = [jnp.exp(l - mx) for l in lses]
        merged = (es[0] * outs[0] + es[1] * outs[1] + es[2] * outs[2]) / (es[0] + es[1] + es[2])
        o_ref[h] = jnp.where(mine, merged, o_ref[h])
        return carry

    lax.fori_loop(0, bt * HEADS_PER_GROUP, one_head, 0)


def _sample_attention(layer, n, qt, kt, vt, cache_views):
    bt = 2
    by_head = (N_GROUPS, HEADS_PER_GROUP, HEAD_DIM, LANES)
    new = pl.BlockSpec(by_head, lambda i: (0, 0, 0, 0))
    in_specs = [new, new, new]
    for c in cache_views:
        in_specs.append(pl.BlockSpec((None, bt) + c.shape[2:], lambda i: (layer, i, 0, 0, 0)))
    out = pl.pallas_call(
        _sample_attn_kernel,
        grid=(n // bt,),
        in_specs=in_specs,
        out_specs=pl.BlockSpec(by_head[1:], lambda i: (0, 0, 0)),
        out_shape=jax.ShapeDtypeStruct(by_head[1:], F32),
        compiler_params=_cparams("arbitrary"),
        name="sample_attention",
    )(qt.reshape(by_head), kt.reshape(by_head), vt.reshape(by_head), *cache_views)
    return out.reshape(GROUP_WIDTH, LANES)


def _rope_tables(pos):
    inv_freq = ROPE_THETA ** (-jnp.arange(0, HEAD_DIM, 2, dtype=F32) / HEAD_DIM)
    ang = pos[:, None] * inv_freq[None, :]
    cos, sin = jnp.cos(ang), jnp.sin(ang)
    zero = jnp.zeros_like(sin)
    reps = LANES // HEAD_DIM
    cos_t = jnp.tile(jnp.concatenate([cos, cos], axis=-1), (1, reps))
    sa_t = jnp.tile(jnp.concatenate([-sin, zero], axis=-1), (1, reps))
    sb_t = jnp.tile(jnp.concatenate([zero, sin], axis=-1), (1, reps))
    return cos_t, sa_t, sb_t


def _segment_matrix():
    lane = jnp.arange(GROUP_WIDTH)
    seg = (lane[:, None] // HEAD_DIM == lane[None, :] // HEAD_DIM).astype(F32) / HEAD_DIM
    return seg.astype(BF16)


def kernel(x_prompt, x_sample, cache_k_w128, cache_v_w128, cache_k_w512, cache_v_w512, cache_k_w2048,
           cache_v_w2048, state_pool, c_prompt, c_sample, norm1_g, norm2_g, w_ada, b_ada, w_in, q_norm_g,
           k_norm_g, w_pool_grp, pool_scale, w_pool_br, w_attn_br, w_out, w_up, w_down):
    depth = w_in.shape[0]
    b, s, _ = x_prompt.shape
    n = x_sample.shape[0]
    assert x_sample.shape[1] == 1 and s % min(SUPER, s) == 0 and s % min(PROMPT_TILE, s) == 0

    mod_p, mod_s = _modulation(c_prompt, c_sample, w_ada, b_ada)
    seg = _segment_matrix()
    tab_p = _rope_tables(jnp.arange(s, dtype=F32))
    tab_s = _rope_tables(PAST_LEN + jnp.arange(1, dtype=F32))

    caches = []
    for ck, cv, d in zip((cache_k_w128, cache_k_w512, cache_k_w2048),
                         (cache_v_w128, cache_v_w512, cache_v_w2048), DILATIONS):
        for c in (ck, cv):
            assert c.shape[2] == N_KEYS * d
            caches.append(jnp.transpose(c, (0, 1, 3, 4, 2)))
    state2d = state_pool.reshape(depth, n, POOL_HIST * POOL_WIDTH)

    bf = lambda a: a.astype(BF16)
    w_in_b, wpg_b, wpb_b, wab_b, wo_b, wup_b, wdn_b = map(
        bf, (w_in, w_pool_grp, w_pool_br, w_attn_br, w_out, w_up, w_down))

    n1 = norm1_g.reshape(depth, 1, D_MODEL)
    n2 = norm2_g.reshape(depth, 1, D_MODEL)
    gq = jnp.tile(q_norm_g, (1, ATTN_WIDTH // HEAD_DIM)).reshape(depth, 1, ATTN_WIDTH)
    gk = jnp.tile(k_norm_g, (1, ATTN_WIDTH // HEAD_DIM)).reshape(depth, 1, ATTN_WIDTH)
    ps = pool_scale.reshape(depth, 1, POOL_WIDTH)
    tail_w = (n2, wpg_b, ps, wpb_b, wab_b, wo_b, wup_b, wdn_b)

    xp = x_prompt
    xs = x_sample.reshape(n, D_MODEL)
    kp = [[] for _ in range(N_GROUPS)]
    vp = [[] for _ in range(N_GROUPS)]
    ks = [[] for _ in range(N_GROUPS)]
    vs = [[] for _ in range(N_GROUPS)]
    pool_p, pool_s = [], []
    for l in range(depth):
        outs = _prompt_project(l, xp, mod_p, n1, w_in_b, gq, gk, seg, *tab_p)
        u, a_pool, a_attn = outs[0:3]
        qd, kd, vd = outs[3:6], outs[6:9], outs[9:12]
        kws, vws = outs[12:15], outs[15:18]
        attn_y = _prompt_attention(qd, kd, vd)
        xp = _prompt_ffn(l, xp, attn_y, u, a_pool, a_attn, mod_p, *tail_w)
        for g in range(N_GROUPS):
            kp[g].append(kws[g].reshape(b, -1, HEADS_PER_GROUP, HEAD_DIM))
            vp[g].append(vws[g].reshape(b, -1, HEADS_PER_GROUP, HEAD_DIM))
        pool_p.append(u[:, s - POOL_HIST:, :])

        u_s, ap_s, aa_s, k_s, v_s, qt, kt, vt = _sample_project(l, xs, mod_s, n1, w_in_b, gq, gk, seg, *tab_s)
        yt_s = _sample_attention(l, n, qt, kt, vt, caches)
        xs = _sample_ffn(l, xs, yt_s, u_s, state2d, ap_s, aa_s, mod_s, *tail_w)
        kh, vh = (a.reshape(n, N_GROUPS, HEADS_PER_GROUP, HEAD_DIM) for a in (k_s, v_s))
        for g in range(N_GROUPS):
            ks[g].append(kh[:, g:g + 1])
            vs[g].append(vh[:, g:g + 1])
        pool_s.append(jnp.concatenate([state_pool[l][:, 1:], u_s[:, None, :]], axis=1))

    st = lambda rows: jnp.stack(rows, axis=0)
    return (xp, xs.reshape(n, 1, D_MODEL),
            st(kp[0]), st(vp[0]), st(kp[1]), st(vp[1]), st(kp[2]), st(vp[2]), st(pool_p),
            st(ks[0]), st(vs[0]), st(ks[1]), st(vs[1]), st(ks[2]), st(vs[2]), st(pool_s))
```

```python
import functools
import math

import jax
import jax.numpy as jnp
from jax import lax
from jax.experimental import pallas as pl
from jax.experimental.pallas import tpu as pltpu

F32 = jnp.float32
BF16 = jnp.bfloat16

D_MODEL = 1024
DEPTH = 4
PAST_LEN = 8192
POOL_WIDTH = 512
POOL_WINDOWS = (2, 4, 8, 16)
POOL_GROUP = 128
POOL_HIST = 15
HEAD_DIM = 64
HEADS_PER_GROUP = 4
GROUP_WIDTH = HEADS_PER_GROUP * HEAD_DIM
DILATIONS = (1, 4, 16)
WINDOWS = (128, 512, 2048)
N_GROUPS = 3
ATTN_WIDTH = N_GROUPS * GROUP_WIDTH
N_KEYS = 128
BLK = 128
D_FF = 4096
ROPE_THETA = 10000.0
EPS = 1e-6
Q_SCALE = 1.0 / math.sqrt(HEAD_DIM)
LOG2E = math.log2(math.e)
LN2 = math.log(2.0)

COL_U = 0
COL_Q = POOL_WIDTH
COL_K = COL_Q + ATTN_WIDTH
COL_V = COL_K + ATTN_WIDTH
COL_AP = COL_V + ATTN_WIDTH
COL_AA = COL_AP + D_MODEL
IN_WIDTH = COL_AA + D_MODEL

LANES = 128
PROMPT_TILE = 512
SUPER = BLK * DILATIONS[-1]
VMEM_LIMIT = 56 * 1024 * 1024


def _cparams(*sem):
    return pltpu.CompilerParams(dimension_semantics=sem, vmem_limit_bytes=VMEM_LIMIT)


def _const_spec(shape):
    nd = len(shape)
    return pl.BlockSpec(shape, lambda *_: (0,) * nd, pipeline_mode=pl.Buffered(1))


def _layer_spec(arr, layer):
    nd = arr.ndim - 1
    return pl.BlockSpec((None,) + arr.shape[1:], lambda *_: (layer,) + (0,) * nd, pipeline_mode=pl.Buffered(1))


def _dot(a, b):
    return jnp.dot(a, b, preferred_element_type=F32)


def _rms_modulate(x, gain, shift, scale):
    ms = jnp.mean(x * x, axis=-1, keepdims=True)
    return (x * lax.rsqrt(ms + EPS) * gain) * (1.0 + scale) + shift


def _mod_kernel(cp_ref, cs_ref, w_ref, b_ref, op_ref, os_ref):
    w = w_ref[...].astype(BF16)

    def mod(c):
        s = c * (1.0 / (1.0 + jnp.exp(-c)))
        return _dot(s.astype(BF16), w) + b_ref[...]

    mp = mod(cp_ref[...])
    for bi in range(op_ref.shape[0]):
        op_ref[bi] = mp[bi:bi + 1, :]
    os_ref[...] = mod(cs_ref[...])


def _modulation(c_prompt, c_sample, w_ada, b_ada):
    b, n = c_prompt.shape[0], c_sample.shape[0]
    depth = w_ada.shape[0]
    tn = 1536
    return pl.pallas_call(
        _mod_kernel,
        grid=(depth, 6 * D_MODEL // tn),
        in_specs=[
            pl.BlockSpec((b, D_MODEL), lambda l, j: (0, 0)),
            pl.BlockSpec((n, D_MODEL), lambda l, j: (0, 0)),
            pl.BlockSpec((None, D_MODEL, tn), lambda l, j: (l, 0, j)),
            pl.BlockSpec((None, 1, tn), lambda l, j: (l, 0, j)),
        ],
        out_specs=[pl.BlockSpec((None, b, 1, tn), lambda l, j: (l, 0, 0, j)),
                   pl.BlockSpec((None, n, tn), lambda l, j: (l, 0, j))],
        out_shape=[jax.ShapeDtypeStruct((depth, b, 1, 6 * D_MODEL), F32),
                   jax.ShapeDtypeStruct((depth, n, 6 * D_MODEL), F32)],
        compiler_params=_cparams("parallel", "parallel"),
        name="adaln_mod",
    )(c_prompt, c_sample, w_ada, b_ada.reshape(depth, 1, 6 * D_MODEL))


def _head_norm_rope(z, gain_ref, seg_ref, cos_ref, sa_ref, sb_ref, out_scale, store):
    cos = cos_ref[...]
    sa = sa_ref[...]
    sb = sb_ref[...]
    for c in range(ATTN_WIDTH // GROUP_WIDTH):
        zc = z[:, c * GROUP_WIDTH:(c + 1) * GROUP_WIDTH]
        ms = _dot((zc * zc).astype(BF16), seg_ref[...])
        n = zc * lax.rsqrt(ms + EPS) * gain_ref[:, c * GROUP_WIDTH:(c + 1) * GROUP_WIDTH]
        for half in range(GROUP_WIDTH // LANES):
            xx = n[:, half * LANES:(half + 1) * LANES]
            r = xx * cos + pltpu.roll(xx, LANES - HEAD_DIM // 2, 1) * sa + pltpu.roll(xx, HEAD_DIM // 2, 1) * sb
            store(c * (GROUP_WIDTH // LANES) + half, r * out_scale if out_scale != 1.0 else r)


def _project(x_ref, mod_ref, n1_ref, w_ref, gq_ref, gk_ref, seg_ref, cos_ref, sa_ref, sb_ref,
             store_q, store_k, store_v, q_scale):
    x = x_ref[...]
    h = _rms_modulate(x, n1_ref[...], mod_ref[:, 0:D_MODEL], mod_ref[:, D_MODEL:2 * D_MODEL])
    hb = h.astype(BF16)
    u = _dot(hb, w_ref[:, COL_U:COL_Q])
    a_pool = _dot(hb, w_ref[:, COL_AP:COL_AA])
    a_attn = _dot(hb, w_ref[:, COL_AA:IN_WIDTH])
    _head_norm_rope(_dot(hb, w_ref[:, COL_Q:COL_K]), gq_ref, seg_ref, cos_ref, sa_ref, sb_ref, q_scale, store_q)
    _head_norm_rope(_dot(hb, w_ref[:, COL_K:COL_V]), gk_ref, seg_ref, cos_ref, sa_ref, sb_ref, 1.0, store_k)
    v = _dot(hb, w_ref[:, COL_V:COL_AP])
    for c in range(ATTN_WIDTH // LANES):
        store_v(c, v[:, c * LANES:(c + 1) * LANES])
    return u, a_pool, a_attn


def _chunk_store(ref):
    def store(c, val):
        ref[c] = val
    return store


def _lane_store(ref, rows):
    def store(c, val):
        ref[0:rows, c * LANES:(c + 1) * LANES] = val
    return store


def _prompt_proj_kernel(x_ref, mod_ref, n1_ref, w_ref, gq_ref, gk_ref, seg_ref, cos_ref, sa_ref, sb_ref,
                        u_ref, ap_ref, aa_ref,
                        q0, q1, q2, k0, k1, k2, v0, v1, v2,
                        kw0, kw1, kw2, vw0, vw1, vw2,
                        qs_ref, ks_ref, vs_ref):
    u, a_pool, a_attn = _project(x_ref, mod_ref, n1_ref, w_ref, gq_ref, gk_ref, seg_ref, cos_ref, sa_ref, sb_ref,
                                 _chunk_store(qs_ref), _chunk_store(ks_ref), _chunk_store(vs_ref),
                                 Q_SCALE * LOG2E)
    u_ref[...] = u
    ap_ref[...] = a_pool.astype(BF16)
    aa_ref[...] = a_attn.astype(BF16)
    tm = x_ref.shape[0]
    for src, dsts in ((qs_ref, (q0, q1, q2)), (ks_ref, (k0, k1, k2)), (vs_ref, (v0, v1, v2))):
        for g, dst in enumerate(dsts):
            d = DILATIONS[g]
            for r in range(d):
                rows = pl.ds(r, tm // d, stride=d) if d > 1 else slice(None)
                for half in range(GROUP_WIDTH // LANES):
                    c = g * (GROUP_WIDTH // LANES) + half
                    dst[r, :, half * LANES:(half + 1) * LANES] = src[c, rows, :].astype(BF16)
    for src, dsts in ((ks_ref, (kw0, kw1, kw2)), (vs_ref, (vw0, vw1, vw2))):
        for g, dst in enumerate(dsts):
            keep = dst.shape[0]
            for half in range(GROUP_WIDTH // LANES):
                c = g * (GROUP_WIDTH // LANES) + half
                dst[:, half * LANES:(half + 1) * LANES] = src[c, tm - keep:, :]


def _prompt_project(layer, x, mod_p, n1, w_in, gq, gk, seg, cos, sa, sb):
    b, s, _ = x.shape
    tm = min(PROMPT_TILE, s)
    nt = s // tm
    row = lambda width: pl.BlockSpec((None, tm, width), lambda bi, i: (bi, i, 0))
    in_specs = [
        row(D_MODEL),
        pl.BlockSpec((None, None, 1, 6 * D_MODEL), lambda bi, i: (layer, bi, 0, 0)),
        _layer_spec(n1, layer),
        _layer_spec(w_in, layer),
        _layer_spec(gq, layer),
        _layer_spec(gk, layer),
        _const_spec((GROUP_WIDTH, GROUP_WIDTH)),
        pl.BlockSpec((tm, LANES), lambda bi, i: (i, 0)),
        pl.BlockSpec((tm, LANES), lambda bi, i: (i, 0)),
        pl.BlockSpec((tm, LANES), lambda bi, i: (i, 0)),
    ]
    out_shape = [jax.ShapeDtypeStruct((b, s, POOL_WIDTH), F32),
                 jax.ShapeDtypeStruct((b, s, D_MODEL), BF16),
                 jax.ShapeDtypeStruct((b, s, D_MODEL), BF16)]
    out_specs = [row(POOL_WIDTH), row(D_MODEL), row(D_MODEL)]
    for _ in range(3):
        for d in DILATIONS:
            out_shape.append(jax.ShapeDtypeStruct((b, d, s // d, GROUP_WIDTH), BF16))
            out_specs.append(pl.BlockSpec((None, d, tm // d, GROUP_WIDTH), lambda bi, i: (bi, 0, i, 0)))
    for _ in range(2):
        for w in WINDOWS:
            keep = min(w, s)
            out_shape.append(jax.ShapeDtypeStruct((b, keep, GROUP_WIDTH), F32))
            if keep >= tm:
                first = (s - keep) // tm
                out_specs.append(pl.BlockSpec(
                    (None, tm, GROUP_WIDTH), lambda bi, i, first=first: (bi, jnp.maximum(i - first, 0), 0)))
            else:
                out_specs.append(pl.BlockSpec((None, keep, GROUP_WIDTH), lambda bi, i: (bi, 0, 0)))
    return pl.pallas_call(
        _prompt_proj_kernel,
        grid=(b, nt),
        in_specs=in_specs,
        out_specs=out_specs,
        out_shape=out_shape,
        scratch_shapes=[pltpu.VMEM((ATTN_WIDTH // LANES, tm, LANES), F32)] * 3,
        compiler_params=_cparams("parallel", "arbitrary"),
        name="prompt_project",
    )(x, mod_p, n1, w_in, gq, gk, seg, cos, sa, sb)


def _sample_proj_kernel(x_ref, mod_ref, n1_ref, w_ref, gq_ref, gk_ref, seg_ref, cos_ref, sa_ref, sb_ref,
                        u_ref, ap_ref, aa_ref, k_ref, v_ref, qt_ref, kt_ref, vt_ref, q_sc, k_sc, v_sc):
    n = x_ref.shape[0]
    for sc in (q_sc, k_sc, v_sc):
        sc[...] = jnp.zeros_like(sc)
    u, a_pool, a_attn = _project(x_ref, mod_ref, n1_ref, w_ref, gq_ref, gk_ref, seg_ref, cos_ref, sa_ref, sb_ref,
                                 _lane_store(q_sc, n), _lane_store(k_sc, n), _lane_store(v_sc, n), Q_SCALE)
    u_ref[...] = u
    ap_ref[...] = a_pool
    aa_ref[...] = a_attn
    k_ref[...] = k_sc[0:n, :]
    v_ref[...] = v_sc[0:n, :]
    for sc, dst in ((q_sc, qt_ref), (k_sc, kt_ref), (v_sc, vt_ref)):
        for c in range(ATTN_WIDTH // LANES):
            dst[c * LANES:(c + 1) * LANES, :] = sc[:, c * LANES:(c + 1) * LANES].T


def _sample_project(layer, x, mod_s, n1, w_in, gq, gk, seg, cos, sa, sb):
    n = x.shape[0]
    assert n <= LANES
    full = lambda shape: pl.BlockSpec(shape, lambda i: (0,) * len(shape))
    in_specs = [full(x.shape), _layer_spec(mod_s, layer), _layer_spec(n1, layer), _layer_spec(w_in, layer),
                _layer_spec(gq, layer), _layer_spec(gk, layer), full(seg.shape), full(cos.shape), full(sa.shape),
                full(sb.shape)]
    shapes = [(n, POOL_WIDTH), (n, D_MODEL), (n, D_MODEL), (n, ATTN_WIDTH), (n, ATTN_WIDTH)] + [(ATTN_WIDTH, LANES)] * 3
    return pl.pallas_call(
        _sample_proj_kernel,
        grid=(1,),
        in_specs=in_specs,
        out_specs=[full(sh) for sh in shapes],
        out_shape=[jax.ShapeDtypeStruct(sh, F32) for sh in shapes],
        scratch_shapes=[pltpu.VMEM((LANES, ATTN_WIDTH), F32)] * 3,
        compiler_params=_cparams("arbitrary"),
        name="sample_project",
    )(x, mod_s, n1, w_in, gq, gk, seg, cos, sa, sb)


def _head_lane_masks():
    lane = lax.broadcasted_iota(jnp.int32, (1, GROUP_WIDTH), 1)
    return [(lane // HEAD_DIM) == h for h in range(HEADS_PER_GROUP)]


def _prompt_attn_kernel(*refs):
    ins, o_ref = refs[:15], refs[15]
    kk_refs, vv_refs = refs[16:19], refs[19:22]
    cls_o, cls_l, nat_o, nat_l = refs[22:26]
    bias_sc = refs[26]
    t = pl.program_id(1)
    hm = _head_lane_masks()
    stacked = HEADS_PER_GROUP * BLK
    row = lax.broadcasted_iota(jnp.int32, (stacked, 2 * BLK), 0)
    col = lax.broadcasted_iota(jnp.int32, (stacked, 2 * BLK), 1)
    dist = BLK + (row % BLK) - col
    band = (dist >= 0) & (dist <= N_KEYS)
    bias_sc[0] = jnp.where(band, 0.0, -jnp.inf)
    bias_sc[1] = jnp.where(band & (col >= BLK), 0.0, -jnp.inf)

    for g in range(N_GROUPS):
        d = DILATIONS[g]
        q_ref, kc_ref, kp_ref, vc_ref, vp_ref = ins[5 * g:5 * g + 5]
        kk, vv = kk_refs[g], vv_refs[g]
        rows_per_class = kc_ref.shape[1]
        nb = rows_per_class // BLK
        n_units = d * nb
        kk[:, 0:BLK, :] = kp_ref[...]
        kk[:, BLK:, :] = kc_ref[...]
        vv[:, 0:BLK, :] = vp_ref[...]
        vv[:, BLK:, :] = vc_ref[...]
        halves = GROUP_WIDTH // LANES

        def unit(u, carry, q_ref=q_ref, kk=kk, vv=vv, nb=nb):
            r = u // nb
            j = u % nb
            base = pl.multiple_of(j * BLK, BLK)
            q = q_ref[r, pl.ds(base, BLK), :]
            keys = kk[r, pl.ds(base, 2 * BLK), :]
            vals = vv[r, pl.ds(base, 2 * BLK), :]
            zero = jnp.zeros_like(q)
            qst = jnp.concatenate([jnp.where(m, q, zero) for m in hm], axis=0)
            s = lax.dot_general(qst, keys, (((1,), (1,)), ((), ())), preferred_element_type=F32)
            s = s + bias_sc[jnp.where(jnp.logical_or(t > 0, j > 0), 0, 1)]
            m = jnp.max(s, axis=-1, keepdims=True)
            p = jnp.exp2(s - m)
            den = jnp.sum(p, axis=-1, keepdims=True)
            o = _dot(p.astype(BF16), vals) / den
            lse = m * LN2 + jnp.log(den)
            o_u = jnp.zeros((BLK, GROUP_WIDTH), F32)
            l_u = jnp.zeros((BLK, GROUP_WIDTH), F32)
            for h in range(HEADS_PER_GROUP):
                o_u = jnp.where(hm[h], o[h * BLK:(h + 1) * BLK], o_u)
                l_u = jnp.where(hm[h], lse[h * BLK:(h + 1) * BLK], l_u)
            out_rows = pl.ds(pl.multiple_of(u * BLK, BLK), BLK)
            cls_o[out_rows, :] = o_u
            cls_l[out_rows, :] = l_u
            return carry

        lax.fori_loop(0, n_units, unit, 0, unroll=16)
        for r in range(d):
            rows = pl.ds(r, rows_per_class, stride=d) if d > 1 else slice(None)
            src_rows = slice(r * rows_per_class, (r + 1) * rows_per_class)
            for half in range(halves):
                lanes = slice(half * LANES, (half + 1) * LANES)
                nat_o[g * halves + half, rows, :] = cls_o[src_rows, lanes]
                nat_l[g * halves + half, rows, :] = cls_l[src_rows, lanes]

    chunk = 256
    halves = GROUP_WIDTH // LANES

    def merge(ci, carry):
        rows = pl.ds(pl.multiple_of(ci * chunk, chunk), chunk)
        for half in range(halves):
            ls = [nat_l[g * halves + half, rows, :] for g in range(N_GROUPS)]
            os_ = [nat_o[g * halves + half, rows, :] for g in range(N_GROUPS)]
            m = jnp.maximum(jnp.maximum(ls[0], ls[1]), ls[2])
            es = [jnp.exp(l - m) for l in ls]
            tot = es[0] + es[1] + es[2]
            acc = es[0] * os_[0] + es[1] * os_[1] + es[2] * os_[2]
            o_ref[rows, half * LANES:(half + 1) * LANES] = (acc / tot).astype(o_ref.dtype)
        return carry

    lax.fori_loop(0, o_ref.shape[0] // chunk, merge, 0)


def _prompt_attention(qd, kd, vd):
    b = qd[0].shape[0]
    s = qd[0].shape[2]
    sb = min(SUPER, s)
    nsb = s // sb
    ins, in_specs, scratch_k, scratch_v = [], [], [], []
    for g, d in enumerate(DILATIONS):
        rpc = sb // d
        nb = rpc // BLK
        cur = pl.BlockSpec((None, d, rpc, GROUP_WIDTH), lambda bi, t: (bi, 0, t, 0))
        prev = pl.BlockSpec((None, d, BLK, GROUP_WIDTH),
                            lambda bi, t, nb=nb: (bi, 0, jnp.maximum(t * nb - 1, 0), 0))
        ins += [qd[g], kd[g], kd[g], vd[g], vd[g]]
        in_specs += [cur, cur, prev, cur, prev]
        scratch_k.append(pltpu.VMEM((d, BLK + rpc, GROUP_WIDTH), BF16))
        scratch_v.append(pltpu.VMEM((d, BLK + rpc, GROUP_WIDTH), BF16))
    scratch = scratch_k + scratch_v + [
        pltpu.VMEM((sb, GROUP_WIDTH), F32), pltpu.VMEM((sb, GROUP_WIDTH), F32),
        pltpu.VMEM((N_GROUPS * GROUP_WIDTH // LANES, sb, LANES), F32),
        pltpu.VMEM((N_GROUPS * GROUP_WIDTH // LANES, sb, LANES), F32),
        pltpu.VMEM((2, HEADS_PER_GROUP * BLK, 2 * BLK), F32)]
    return pl.pallas_call(
        _prompt_attn_kernel,
        grid=(b, nsb),
        in_specs=in_specs,
        out_specs=pl.BlockSpec((None, sb, GROUP_WIDTH), lambda bi, t: (bi, t, 0)),
        out_shape=jax.ShapeDtypeStruct((b, s, GROUP_WIDTH), BF16),
        scratch_shapes=scratch,
        compiler_params=_cparams("parallel", "parallel"),
        name="prompt_attention",
    )(*ins)


def _sigmoid(a):
    return 1.0 / (1.0 + jnp.exp(-a))


def _merge_ffn(x, pool_p, attn_y, a_pool, a_attn, mod_ref, n2_ref, wpg_ref, ps_ref, wpb_ref, wab_ref,
               wo_ref, wup_ref, wdn_ref):
    ys = [_dot(p.astype(BF16), wpg_ref[g]) for g, p in enumerate(pool_p)]
    pool_y = jnp.concatenate(ys, axis=-1) * ps_ref[...]
    pb = _dot(pool_y.astype(BF16), wpb_ref[...])
    ab = _dot(attn_y.astype(BF16), wab_ref[...])
    merged = _sigmoid(a_pool.astype(F32)) * pb + _sigmoid(a_attn.astype(F32)) * ab
    g1 = mod_ref[:, 2 * D_MODEL:3 * D_MODEL]
    x1 = x + g1 * _dot(merged.astype(BF16), wo_ref[...])
    h2 = _rms_modulate(x1, n2_ref[...], mod_ref[:, 3 * D_MODEL:4 * D_MODEL], mod_ref[:, 4 * D_MODEL:5 * D_MODEL])
    h2b = h2.astype(BF16)
    ff_chunk = 1024
    y = jnp.zeros_like(x)
    for c in range(D_FF // ff_chunk):
        hid = jnp.maximum(_dot(h2b, wup_ref[:, c * ff_chunk:(c + 1) * ff_chunk]), 0.0)
        y = y + _dot((hid * hid).astype(BF16), wdn_ref[c * ff_chunk:(c + 1) * ff_chunk, :])
    g2 = mod_ref[:, 5 * D_MODEL:6 * D_MODEL]
    return x1 + g2 * y


def _prompt_ffn_kernel(x_ref, y_ref, u_ref, up_ref, ap_ref, aa_ref, mod_ref, n2_ref, wpg_ref, ps_ref,
                       wpb_ref, wab_ref, wo_ref, wup_ref, wdn_ref, o_ref, ext_ref):
    i = pl.program_id(1)
    tm = x_ref.shape[0]
    hist = up_ref.shape[0]
    ext_ref[0:hist, :] = jnp.where(i > 0, up_ref[...], 0.0)
    ext_ref[hist:, :] = u_ref[...]
    pos = (i * tm + lax.broadcasted_iota(jnp.int32, (tm, 1), 0)).astype(F32)
    pool_p = []
    for g, w in enumerate(POOL_WINDOWS):
        e = ext_ref[:, g * POOL_GROUP:(g + 1) * POOL_GROUP]
        acc = e
        span = 1
        while span < w:
            acc = acc + pltpu.roll(acc, span, 0)
            span *= 2
        cnt = jnp.minimum(pos + 1.0, float(w))
        pool_p.append(acc[hist:] / cnt - e[hist:])
    o_ref[...] = _merge_ffn(x_ref[...], pool_p, y_ref[...], ap_ref[...], aa_ref[...], mod_ref, n2_ref, wpg_ref,
                            ps_ref, wpb_ref, wab_ref, wo_ref, wup_ref, wdn_ref)


def _prompt_ffn(layer, x, attn_y, u, a_pool, a_attn, mod_p, n2, wpg, ps, wpb, wab, wo, wup, wdn):
    b, s, _ = x.shape
    tm = min(PROMPT_TILE, s)
    hist = 16
    row = lambda width: pl.BlockSpec((None, tm, width), lambda bi, i: (bi, i, 0))
    in_specs = [
        row(D_MODEL), row(GROUP_WIDTH), row(POOL_WIDTH),
        pl.BlockSpec((None, hist, POOL_WIDTH), lambda bi, i: (bi, jnp.maximum(i * (tm // hist) - 1, 0), 0)),
        row(D_MODEL), row(D_MODEL),
        pl.BlockSpec((None, None, 1, 6 * D_MODEL), lambda bi, i: (layer, bi, 0, 0)),
    ] + [_layer_spec(w, layer) for w in (n2, wpg, ps, wpb, wab, wo, wup, wdn)]
    return pl.pallas_call(
        _prompt_ffn_kernel,
        grid=(b, s // tm),
        in_specs=in_specs,
        out_specs=row(D_MODEL),
        out_shape=jax.ShapeDtypeStruct((b, s, D_MODEL), F32),
        scratch_shapes=[pltpu.VMEM((hist + tm, POOL_WIDTH), F32)],
        compiler_params=_cparams("parallel", "parallel"),
        name="prompt_merge_ffn",
    )(x, attn_y, u, u, a_pool, a_attn, mod_p, n2, wpg, ps, wpb, wab, wo, wup, wdn)


def _sample_ffn_kernel(x_ref, yt_ref, u_ref, st_ref, ap_ref, aa_ref, mod_ref, n2_ref, wpg_ref, ps_ref,
                       wpb_ref, wab_ref, wo_ref, wup_ref, wdn_ref, o_ref):
    n = x_ref.shape[0]
    u = u_ref[...]
    attn_y = yt_ref[...].T[0:n, :]
    tail = jnp.zeros_like(u)
    sums = {}
    for back in range(1, POOL_HIST + 1):
        row = POOL_HIST - back
        tail = tail + st_ref[:, row * POOL_WIDTH:(row + 1) * POOL_WIDTH]
        if back + 1 in POOL_WINDOWS:
            sums[back + 1] = tail
    pool_p = []
    for g, w in enumerate(POOL_WINDOWS):
        cols = slice(g * POOL_GROUP, (g + 1) * POOL_GROUP)
        cnt = min(PAST_LEN + 1.0, float(w))
        pool_p.append((sums[w][:, cols] + u[:, cols]) / cnt - u[:, cols])
    o_ref[...] = _merge_ffn(x_ref[...], pool_p, attn_y, ap_ref[...], aa_ref[...], mod_ref, n2_ref, wpg_ref,
                            ps_ref, wpb_ref, wab_ref, wo_ref, wup_ref, wdn_ref)


def _sample_ffn(layer, x, attn_yt, u, state2d, a_pool, a_attn, mod_s, n2, wpg, ps, wpb, wab, wo, wup, wdn):
    n = x.shape[0]
    full = lambda a: pl.BlockSpec(a.shape, lambda i: (0,) * a.ndim)
    ins = (x, attn_yt, u, state2d, a_pool, a_attn, mod_s, n2, wpg, ps, wpb, wab, wo, wup, wdn)
    in_specs = [full(a) for a in ins[:3]] + [_layer_spec(state2d, layer)] + [full(a) for a in ins[4:6]]
    in_specs += [_layer_spec(a, layer) for a in ins[6:]]
    return pl.pallas_call(
        _sample_ffn_kernel,
        grid=(1,),
        in_specs=in_specs,
        out_specs=pl.BlockSpec((n, D_MODEL), lambda i: (0, 0)),
        out_shape=jax.ShapeDtypeStruct((n, D_MODEL), F32),
        compiler_params=_cparams("arbitrary"),
        name="sample_merge_ffn",
    )(*ins)


def _sample_attn_kernel(q_ref, k_ref, v_ref, ck0, cv0, ck1, cv1, ck2, cv2, o_ref):
    step = pl.program_id(0)
    bt = ck0.shape[0]
    caches = ((ck0, cv0), (ck1, cv1), (ck2, cv2))
    lane = lax.broadcasted_iota(jnp.int32, (HEAD_DIM, LANES), 1)

    @pl.when(step == 0)
    def _():
        o_ref[...] = jnp.zeros_like(o_ref)

    def one_head(i, carry):
        b = i // HEADS_PER_GROUP
        h = i % HEADS_PER_GROUP
        mine = lane == step * bt + b
        column = lambda ref, g: jnp.sum(jnp.where(mine, ref[g, h], 0.0), axis=1, keepdims=True)
        outs, lses = [], []
        for g, (ck, cv) in enumerate(caches):
            d = DILATIONS[g]
            q = column(q_ref, g)
            s = jnp.sum(ck[b, h] * q, axis=0, keepdims=True)
            if d > 1:
                row = lax.broadcasted_iota(jnp.int32, s.shape, 1)
                s = jnp.where(row % d == 0, s, -jnp.inf)
            s_new = jnp.sum(column(k_ref, g) * q, axis=0, keepdims=True)
            m = jnp.maximum(jnp.max(s, axis=1, keepdims=True), s_new)
            p = jnp.exp(s - m)
            p_new = jnp.exp(s_new - m)
            den = jnp.sum(p, axis=1, keepdims=True) + p_new
            acc = jnp.sum(cv[b, h] * p, axis=1, keepdims=True) + p_new * column(v_ref, g)
            outs.append(acc / den)
            lses.append(m + jnp.log(den))
        mx = jnp.maximum(jnp.maximum(lses[0], lses[1]), lses[2])
        es = [jnp.exp(l - mx) for l in lses]
        merged = (es[0] * outs[0] + es[1] * outs[1] + es[2] * outs[2]) / (es[0] + es[1] + es[2])
        o_ref[h] = jnp.where(mine, merged, o_ref[h])
        return carry

    lax.fori_loop(0, bt * HEADS_PER_GROUP, one_head, 0)


def _sample_attention(layer, n, qt, kt, vt, cache_views):
    bt = 2
    by_head = (N_GROUPS, HEADS_PER_GROUP, HEAD_DIM, LANES)
    new = pl.BlockSpec(by_head, lambda i: (0, 0, 0, 0))
    in_specs = [new, new, new]
    for c in cache_views:
        in_specs.append(pl.BlockSpec((None, bt) + c.shape[2:], lambda i: (layer, i, 0, 0, 0)))
    out = pl.pallas_call(
        _sample_attn_kernel,
        grid=(n // bt,),
        in_specs=in_specs,
        out_specs=pl.BlockSpec(by_head[1:], lambda i: (0, 0, 0)),
        out_shape=jax.ShapeDtypeStruct(by_head[1:], F32),
        compiler_params=_cparams("arbitrary"),
        name="sample_attention",
    )(qt.reshape(by_head), kt.reshape(by_head), vt.reshape(by_head), *cache_views)
    return out.reshape(GROUP_WIDTH, LANES)


def _rope_tables(pos):
    inv_freq = ROPE_THETA ** (-jnp.arange(0, HEAD_DIM, 2, dtype=F32) / HEAD_DIM)
    ang = pos[:, None] * inv_freq[None, :]
    cos, sin = jnp.cos(ang), jnp.sin(ang)
    zero = jnp.zeros_like(sin)
    reps = LANES // HEAD_DIM
    cos_t = jnp.tile(jnp.concatenate([cos, cos], axis=-1), (1, reps))
    sa_t = jnp.tile(jnp.concatenate([-sin, zero], axis=-1), (1, reps))
    sb_t = jnp.tile(jnp.concatenate([zero, sin], axis=-1), (1, reps))
    return cos_t, sa_t, sb_t


def _segment_matrix():
    lane = jnp.arange(GROUP_WIDTH)
    seg = (lane[:, None] // HEAD_DIM == lane[None, :] // HEAD_DIM).astype(F32) / HEAD_DIM
    return seg.astype(BF16)


def kernel(x_prompt, x_sample, cache_k_w128, cache_v_w128, cache_k_w512, cache_v_w512, cache_k_w2048,
           cache_v_w2048, state_pool, c_prompt, c_sample, norm1_g, norm2_g, w_ada, b_ada, w_in, q_norm_g,
           k_norm_g, w_pool_grp, pool_scale, w_pool_br, w_attn_br, w_out, w_up, w_down):
    depth = w_in.shape[0]
    b, s, _ = x_prompt.shape
    n = x_sample.shape[0]
    assert x_sample.shape[1] == 1 and s % min(SUPER, s) == 0 and s % min(PROMPT_TILE, s) == 0

    mod_p, mod_s = _modulation(c_prompt, c_sample, w_ada, b_ada)
    seg = _segment_matrix()
    tab_p = _rope_tables(jnp.arange(s, dtype=F32))
    tab_s = _rope_tables(PAST_LEN + jnp.arange(1, dtype=F32))

    caches = []
    for ck, cv, d in zip((cache_k_w128, cache_k_w512, cache_k_w2048),
                         (cache_v_w128, cache_v_w512, cache_v_w2048), DILATIONS):
        for c in (ck, cv):
            assert c.shape[2] == N_KEYS * d
            caches.append(jnp.transpose(c, (0, 1, 3, 4, 2)))
    state2d = state_pool.reshape(depth, n, POOL_HIST * POOL_WIDTH)

    bf = lambda a: a.astype(BF16)
    w_in_b, wpg_b, wpb_b, wab_b, wo_b, wup_b, wdn_b = map(
        bf, (w_in, w_pool_grp, w_pool_br, w_attn_br, w_out, w_up, w_down))

    n1 = norm1_g.reshape(depth, 1, D_MODEL)
    n2 = norm2_g.reshape(depth, 1, D_MODEL)
    gq = jnp.tile(q_norm_g, (1, ATTN_WIDTH // HEAD_DIM)).reshape(depth, 1, ATTN_WIDTH)
    gk = jnp.tile(k_norm_g, (1, ATTN_WIDTH // HEAD_DIM)).reshape(depth, 1, ATTN_WIDTH)
    ps = pool_scale.reshape(depth, 1, POOL_WIDTH)
    tail_w = (n2, wpg_b, ps, wpb_b, wab_b, wo_b, wup_b, wdn_b)

    xp = x_prompt
    xs = x_sample.reshape(n, D_MODEL)
    kp = [[] for _ in range(N_GROUPS)]
    vp = [[] for _ in range(N_GROUPS)]
    ks = [[] for _ in range(N_GROUPS)]
    vs = [[] for _ in range(N_GROUPS)]
    pool_p, pool_s = [], []
    for l in range(depth):
        outs = _prompt_project(l, xp, mod_p, n1, w_in_b, gq, gk, seg, *tab_p)
        u, a_pool, a_attn = outs[0:3]
        qd, kd, vd = outs[3:6], outs[6:9], outs[9:12]
        kws, vws = outs[12:15], outs[15:18]
        attn_y = _prompt_attention(qd, kd, vd)
        xp = _prompt_ffn(l, xp, attn_y, u, a_pool, a_attn, mod_p, *tail_w)
        for g in range(N_GROUPS):
            kp[g].append(kws[g].reshape(b, -1, HEADS_PER_GROUP, HEAD_DIM))
            vp[g].append(vws[g].reshape(b, -1, HEADS_PER_GROUP, HEAD_DIM))
        pool_p.append(u[:, s - POOL_HIST:, :])

        u_s, ap_s, aa_s, k_s, v_s, qt, kt, vt = _sample_project(l, xs, mod_s, n1, w_in_b, gq, gk, seg, *tab_s)
        yt_s = _sample_attention(l, n, qt, kt, vt, caches)
        xs = _sample_ffn(l, xs, yt_s, u_s, state2d, ap_s, aa_s, mod_s, *tail_w)
        kh, vh = (a.reshape(n, N_GROUPS, HEADS_PER_GROUP, HEAD_DIM) for a in (k_s, v_s))
        for g in range(N_GROUPS):
            ks[g].append(kh[:, g:g + 1])
            vs[g].append(vh[:, g:g + 1])
        pool_s.append(jnp.concatenate([state_pool[l][:, 1:], u_s[:, None, :]], axis=1))

    st = lambda rows: jnp.stack(rows, axis=0)
    return (xp, xs.reshape(n, 1, D_MODEL),
            st(kp[0]), st(vp[0]), st(kp[1]), st(vp[1]), st(kp[2]), st(vp[2]), st(pool_p),
            st(ks[0]), st(vs[0]), st(ks[1]), st(vs[1]), st(ks[2]), st(vs[2]), st(pool_s))
```

```python
import functools
import math

import jax
import jax.numpy as jnp
from jax import lax
from jax.experimental import pallas as pl
from jax.experimental.pallas import tpu as pltpu

F32 = jnp.float32
BF16 = jnp.bfloat16

D_MODEL = 1024
DEPTH = 4
PAST_LEN = 8192
POOL_WIDTH = 512
POOL_WINDOWS = (2, 4, 8, 16)
POOL_GROUP = 128
POOL_HIST = 15
HEAD_DIM = 64
HEADS_PER_GROUP = 4
GROUP_WIDTH = HEADS_PER_GROUP * HEAD_DIM
DILATIONS = (1, 4, 16)
WINDOWS = (128, 512, 2048)
N_GROUPS = 3
ATTN_WIDTH = N_GROUPS * GROUP_WIDTH
N_KEYS = 128
BLK = 128
D_FF = 4096
ROPE_THETA = 10000.0
EPS = 1e-6
Q_SCALE = 1.0 / math.sqrt(HEAD_DIM)
LOG2E = math.log2(math.e)
LN2 = math.log(2.0)

COL_U = 0
COL_Q = POOL_WIDTH
COL_K = COL_Q + ATTN_WIDTH
COL_V = COL_K + ATTN_WIDTH
COL_AP = COL_V + ATTN_WIDTH
COL_AA = COL_AP + D_MODEL
IN_WIDTH = COL_AA + D_MODEL

LANES = 128
PROMPT_TILE = 512
SUPER = BLK * DILATIONS[-1]
VMEM_LIMIT = 56 * 1024 * 1024


def _cparams(*sem):
    return pltpu.CompilerParams(dimension_semantics=sem, vmem_limit_bytes=VMEM_LIMIT)


def _const_spec(shape):
    nd = len(shape)
    return pl.BlockSpec(shape, lambda *_: (0,) * nd, pipeline_mode=pl.Buffered(1))


def _layer_spec(arr, layer):
    nd = arr.ndim - 1
    return pl.BlockSpec((None,) + arr.shape[1:], lambda *_: (layer,) + (0,) * nd, pipeline_mode=pl.Buffered(1))


def _dot(a, b):
    return jnp.dot(a, b, preferred_element_type=F32)


def _rms_modulate(x, gain, shift, scale):
    ms = jnp.mean(x * x, axis=-1, keepdims=True)
    return (x * lax.rsqrt(ms + EPS) * gain) * (1.0 + scale) + shift


def _mod_kernel(cp_ref, cs_ref, w_ref, b_ref, op_ref, os_ref):
    w = w_ref[...].astype(BF16)

    def mod(c):
        s = c * (1.0 / (1.0 + jnp.exp(-c)))
        return _dot(s.astype(BF16), w) + b_ref[...]

    mp = mod(cp_ref[...])
    for bi in range(op_ref.shape[0]):
        op_ref[bi] = mp[bi:bi + 1, :]
    os_ref[...] = mod(cs_ref[...])


def _modulation(c_prompt, c_sample, w_ada, b_ada):
    b, n = c_prompt.shape[0], c_sample.shape[0]
    depth = w_ada.shape[0]
    tn = 1536
    return pl.pallas_call(
        _mod_kernel,
        grid=(depth, 6 * D_MODEL // tn),
        in_specs=[
            pl.BlockSpec((b, D_MODEL), lambda l, j: (0, 0)),
            pl.BlockSpec((n, D_MODEL), lambda l, j: (0, 0)),
            pl.BlockSpec((None, D_MODEL, tn), lambda l, j: (l, 0, j)),
            pl.BlockSpec((None, 1, tn), lambda l, j: (l, 0, j)),
        ],
        out_specs=[pl.BlockSpec((None, b, 1, tn), lambda l, j: (l, 0, 0, j)),
                   pl.BlockSpec((None, n, tn), lambda l, j: (l, 0, j))],
        out_shape=[jax.ShapeDtypeStruct((depth, b, 1, 6 * D_MODEL), F32),
                   jax.ShapeDtypeStruct((depth, n, 6 * D_MODEL), F32)],
        compiler_params=_cparams("parallel", "parallel"),
        name="adaln_mod",
    )(c_prompt, c_sample, w_ada, b_ada.reshape(depth, 1, 6 * D_MODEL))


def _head_norm_rope(z, gain_ref, seg_ref, cos_ref, sa_ref, sb_ref, out_scale, store):
    cos = cos_ref[...]
    sa = sa_ref[...]
    sb = sb_ref[...]
    for c in range(ATTN_WIDTH // GROUP_WIDTH):
        zc = z[:, c * GROUP_WIDTH:(c + 1) * GROUP_WIDTH]
        ms = _dot((zc * zc).astype(BF16), seg_ref[...])
        n = zc * lax.rsqrt(ms + EPS) * gain_ref[:, c * GROUP_WIDTH:(c + 1) * GROUP_WIDTH]
        for half in range(GROUP_WIDTH // LANES):
            xx = n[:, half * LANES:(half + 1) * LANES]
            r = xx * cos + pltpu.roll(xx, LANES - HEAD_DIM // 2, 1) * sa + pltpu.roll(xx, HEAD_DIM // 2, 1) * sb
            store(c * (GROUP_WIDTH // LANES) + half, r * out_scale if out_scale != 1.0 else r)


def _project(x_ref, mod_ref, n1_ref, w_ref, gq_ref, gk_ref, seg_ref, cos_ref, sa_ref, sb_ref,
             store_q, store_k, store_v, q_scale, after_qkv=None):
    x = x_ref[...]
    h = _rms_modulate(x, n1_ref[...], mod_ref[:, 0:D_MODEL], mod_ref[:, D_MODEL:2 * D_MODEL])
    hb = h.astype(BF16)
    _head_norm_rope(_dot(hb, w_ref[:, COL_Q:COL_K]), gq_ref, seg_ref, cos_ref, sa_ref, sb_ref, q_scale, store_q)
    _head_norm_rope(_dot(hb, w_ref[:, COL_K:COL_V]), gk_ref, seg_ref, cos_ref, sa_ref, sb_ref, 1.0, store_k)
    v = _dot(hb, w_ref[:, COL_V:COL_AP])
    for c in range(ATTN_WIDTH // LANES):
        store_v(c, v[:, c * LANES:(c + 1) * LANES])
    if after_qkv is not None:
        after_qkv()
    u = _dot(hb, w_ref[:, COL_U:COL_Q])
    a_pool = _dot(hb, w_ref[:, COL_AP:COL_AA])
    a_attn = _dot(hb, w_ref[:, COL_AA:IN_WIDTH])
    return u, a_pool, a_attn


def _chunk_store(ref):
    def store(c, val):
        ref[c] = val
    return store


def _lane_store(ref, rows):
    def store(c, val):
        ref[0:rows, c * LANES:(c + 1) * LANES] = val
    return store


def _prompt_proj_kernel(x_ref, mod_ref, n1_ref, w_ref, gq_ref, gk_ref, seg_ref, cos_ref, sa_ref, sb_ref,
                        u_ref, ap_ref, aa_ref,
                        q0, q1, q2, k0, k1, k2, v0, v1, v2,
                        kw0, kw1, kw2, vw0, vw1, vw2,
                        qs_ref, ks_ref, vs_ref):
    tm = x_ref.shape[0]

    def write_qkv():
        for src, dsts in ((qs_ref, (q0, q1, q2)), (ks_ref, (k0, k1, k2)), (vs_ref, (v0, v1, v2))):
            for g, dst in enumerate(dsts):
                d = DILATIONS[g]
                for r in range(d):
                    rows = pl.ds(r, tm // d, stride=d) if d > 1 else slice(None)
                    for half in range(GROUP_WIDTH // LANES):
                        c = g * (GROUP_WIDTH // LANES) + half
                        dst[r, :, half * LANES:(half + 1) * LANES] = src[c, rows, :].astype(BF16)
        for src, dsts in ((ks_ref, (kw0, kw1, kw2)), (vs_ref, (vw0, vw1, vw2))):
            for g, dst in enumerate(dsts):
                keep = dst.shape[0]
                for half in range(GROUP_WIDTH // LANES):
                    c = g * (GROUP_WIDTH // LANES) + half
                    dst[:, half * LANES:(half + 1) * LANES] = src[c, tm - keep:, :]

    u, a_pool, a_attn = _project(x_ref, mod_ref, n1_ref, w_ref, gq_ref, gk_ref, seg_ref, cos_ref, sa_ref, sb_ref,
                                 _chunk_store(qs_ref), _chunk_store(ks_ref), _chunk_store(vs_ref),
                                 Q_SCALE * LOG2E,
                                 after_qkv=write_qkv)
    u_ref[...] = u
    ap_ref[...] = a_pool.astype(BF16)
    aa_ref[...] = a_attn.astype(BF16)


def _prompt_project(layer, x, mod_p, n1, w_in, gq, gk, seg, cos, sa, sb):
    b, s, _ = x.shape
    tm = min(PROMPT_TILE, s)
    nt = s // tm
    row = lambda width: pl.BlockSpec((None, tm, width), lambda bi, i: (bi, i, 0))
    in_specs = [
        row(D_MODEL),
        pl.BlockSpec((None, None, 1, 6 * D_MODEL), lambda bi, i: (layer, bi, 0, 0)),
        _layer_spec(n1, layer),
        _layer_spec(w_in, layer),
        _layer_spec(gq, layer),
        _layer_spec(gk, layer),
        _const_spec((GROUP_WIDTH, GROUP_WIDTH)),
        pl.BlockSpec((tm, LANES), lambda bi, i: (i, 0)),
        pl.BlockSpec((tm, LANES), lambda bi, i: (i, 0)),
        pl.BlockSpec((tm, LANES), lambda bi, i: (i, 0)),
    ]
    out_shape = [jax.ShapeDtypeStruct((b, s, POOL_WIDTH), F32),
                 jax.ShapeDtypeStruct((b, s, D_MODEL), BF16),
                 jax.ShapeDtypeStruct((b, s, D_MODEL), BF16)]
    out_specs = [row(POOL_WIDTH), row(D_MODEL), row(D_MODEL)]
    for _ in range(3):
        for d in DILATIONS:
            out_shape.append(jax.ShapeDtypeStruct((b, d, s // d, GROUP_WIDTH), BF16))
            out_specs.append(pl.BlockSpec((None, d, tm // d, GROUP_WIDTH), lambda bi, i: (bi, 0, i, 0)))
    for _ in range(2):
        for w in WINDOWS:
            keep = min(w, s)
            out_shape.append(jax.ShapeDtypeStruct((b, keep, GROUP_WIDTH), F32))
            if keep >= tm:
                first = (s - keep) // tm
                out_specs.append(pl.BlockSpec(
                    (None, tm, GROUP_WIDTH), lambda bi, i, first=first: (bi, jnp.maximum(i - first, 0), 0)))
            else:
                out_specs.append(pl.BlockSpec((None, keep, GROUP_WIDTH), lambda bi, i: (bi, 0, 0)))
    return pl.pallas_call(
        _prompt_proj_kernel,
        grid=(b, nt),
        in_specs=in_specs,
        out_specs=out_specs,
        out_shape=out_shape,
        scratch_shapes=[pltpu.VMEM((ATTN_WIDTH // LANES, tm, LANES), F32)] * 3,
        compiler_params=_cparams("parallel", "arbitrary"),
        name="prompt_project",
    )(x, mod_p, n1, w_in, gq, gk, seg, cos, sa, sb)


def _sample_proj_kernel(x_ref, mod_ref, n1_ref, w_ref, gq_ref, gk_ref, seg_ref, cos_ref, sa_ref, sb_ref,
                        u_ref, ap_ref, aa_ref, k_ref, v_ref, qt_ref, kt_ref, vt_ref, q_sc, k_sc, v_sc):
    n = x_ref.shape[0]
    for sc in (q_sc, k_sc, v_sc):
        sc[...] = jnp.zeros_like(sc)
    u, a_pool, a_attn = _project(x_ref, mod_ref, n1_ref, w_ref, gq_ref, gk_ref, seg_ref, cos_ref, sa_ref, sb_ref,
                                 _lane_store(q_sc, n), _lane_store(k_sc, n), _lane_store(v_sc, n), Q_SCALE)
    u_ref[...] = u
    ap_ref[...] = a_pool
    aa_ref[...] = a_attn
    k_ref[...] = k_sc[0:n, :]
    v_ref[...] = v_sc[0:n, :]
    for sc, dst in ((q_sc, qt_ref), (k_sc, kt_ref), (v_sc, vt_ref)):
        for c in range(ATTN_WIDTH // LANES):
            dst[c * LANES:(c + 1) * LANES, :] = sc[:, c * LANES:(c + 1) * LANES].T


def _sample_project(layer, x, mod_s, n1, w_in, gq, gk, seg, cos, sa, sb):
    n = x.shape[0]
    assert n <= LANES
    full = lambda shape: pl.BlockSpec(shape, lambda i: (0,) * len(shape))
    in_specs = [full(x.shape), _layer_spec(mod_s, layer), _layer_spec(n1, layer), _layer_spec(w_in, layer),
                _layer_spec(gq, layer), _layer_spec(gk, layer), full(seg.shape), full(cos.shape), full(sa.shape),
                full(sb.shape)]
    shapes = [(n, POOL_WIDTH), (n, D_MODEL), (n, D_MODEL), (n, ATTN_WIDTH), (n, ATTN_WIDTH)] + [(ATTN_WIDTH, LANES)] * 3
    return pl.pallas_call(
        _sample_proj_kernel,
        grid=(1,),
        in_specs=in_specs,
        out_specs=[full(sh) for sh in shapes],
        out_shape=[jax.ShapeDtypeStruct(sh, F32) for sh in shapes],
        scratch_shapes=[pltpu.VMEM((LANES, ATTN_WIDTH), F32)] * 3,
        compiler_params=_cparams("arbitrary"),
        name="sample_project",
    )(x, mod_s, n1, w_in, gq, gk, seg, cos, sa, sb)


def _head_lane_masks():
    lane = lax.broadcasted_iota(jnp.int32, (1, GROUP_WIDTH), 1)
    return [(lane // HEAD_DIM) == h for h in range(HEADS_PER_GROUP)]


def _prompt_attn_kernel(*refs):
    ins, o_ref = refs[:15], refs[15]
    kk_refs, vv_refs = refs[16:19], refs[19:22]
    cls_o, cls_l, nat_o, nat_l = refs[22:26]
    bias_sc = refs[26]
    t = pl.program_id(1)
    hm = _head_lane_masks()
    stacked = HEADS_PER_GROUP * BLK
    row = lax.broadcasted_iota(jnp.int32, (stacked, 2 * BLK), 0)
    col = lax.broadcasted_iota(jnp.int32, (stacked, 2 * BLK), 1)
    dist = BLK + (row % BLK) - col
    band = (dist >= 0) & (dist <= N_KEYS)
    bias_sc[0] = jnp.where(band, 0.0, -jnp.inf)
    bias_sc[1] = jnp.where(band & (col >= BLK), 0.0, -jnp.inf)

    for g in range(N_GROUPS):
        d = DILATIONS[g]
        q_ref, kc_ref, kp_ref, vc_ref, vp_ref = ins[5 * g:5 * g + 5]
        kk, vv = kk_refs[g], vv_refs[g]
        rows_per_class = kc_ref.shape[1]
        nb = rows_per_class // BLK
        n_units = d * nb
        kk[:, 0:BLK, :] = kp_ref[...]
        kk[:, BLK:, :] = kc_ref[...]
        vv[:, 0:BLK, :] = vp_ref[...]
        vv[:, BLK:, :] = vc_ref[...]
        halves = GROUP_WIDTH // LANES

        def unit(u, carry, q_ref=q_ref, kk=kk, vv=vv, nb=nb):
            r = u // nb
            j = u % nb
            base = pl.multiple_of(j * BLK, BLK)
            q = q_ref[r, pl.ds(base, BLK), :]
            keys = kk[r, pl.ds(base, 2 * BLK), :]
            vals = vv[r, pl.ds(base, 2 * BLK), :]
            zero = jnp.zeros_like(q)
            qst = jnp.concatenate([jnp.where(m, q, zero) for m in hm], axis=0)
            s = lax.dot_general(qst, keys, (((1,), (1,)), ((), ())), preferred_element_type=F32)
            s = s + bias_sc[jnp.where(jnp.logical_or(t > 0, j > 0), 0, 1)]
            m = jnp.max(s, axis=-1, keepdims=True)
            p = jnp.exp2(s - m)
            den = jnp.sum(p, axis=-1, keepdims=True)
            o = _dot(p.astype(BF16), vals) / den
            lse = m * LN2 + jnp.log(den)
            o_u = jnp.zeros((BLK, GROUP_WIDTH), F32)
            l_u = jnp.zeros((BLK, GROUP_WIDTH), F32)
            for h in range(HEADS_PER_GROUP):
                o_u = jnp.where(hm[h], o[h * BLK:(h + 1) * BLK], o_u)
                l_u = jnp.where(hm[h], lse[h * BLK:(h + 1) * BLK], l_u)
            out_rows = pl.ds(pl.multiple_of(u * BLK, BLK), BLK)
            cls_o[out_rows, :] = o_u
            cls_l[out_rows, :] = l_u
            return carry

        lax.fori_loop(0, n_units, unit, 0, unroll=16)
        for r in range(d):
            rows = pl.ds(r, rows_per_class, stride=d) if d > 1 else slice(None)
            src_rows = slice(r * rows_per_class, (r + 1) * rows_per_class)
            for half in range(halves):
                lanes = slice(half * LANES, (half + 1) * LANES)
                nat_o[g * halves + half, rows, :] = cls_o[src_rows, lanes]
                nat_l[g * halves + half, rows, :] = cls_l[src_rows, lanes]

    chunk = 256
    halves = GROUP_WIDTH // LANES

    def merge(ci, carry):
        rows = pl.ds(pl.multiple_of(ci * chunk, chunk), chunk)
        for half in range(halves):
            ls = [nat_l[g * halves + half, rows, :] for g in range(N_GROUPS)]
            os_ = [nat_o[g * halves + half, rows, :] for g in range(N_GROUPS)]
            m = jnp.maximum(jnp.maximum(ls[0], ls[1]), ls[2])
            es = [jnp.exp(l - m) for l in ls]
            tot = es[0] + es[1] + es[2]
            acc = es[0] * os_[0] + es[1] * os_[1] + es[2] * os_[2]
            o_ref[rows, half * LANES:(half + 1) * LANES] = (acc / tot).astype(o_ref.dtype)
        return carry

    lax.fori_loop(0, o_ref.shape[0] // chunk, merge, 0)


def _prompt_attention(qd, kd, vd):
    b = qd[0].shape[0]
    s = qd[0].shape[2]
    sb = min(SUPER, s)
    nsb = s // sb
    ins, in_specs, scratch_k, scratch_v = [], [], [], []
    for g, d in enumerate(DILATIONS):
        rpc = sb // d
        nb = rpc // BLK
        cur = pl.BlockSpec((None, d, rpc, GROUP_WIDTH), lambda bi, t: (bi, 0, t, 0))
        prev = pl.BlockSpec((None, d, BLK, GROUP_WIDTH),
                            lambda bi, t, nb=nb: (bi, 0, jnp.maximum(t * nb - 1, 0), 0))
        ins += [qd[g], kd[g], kd[g], vd[g], vd[g]]
        in_specs += [cur, cur, prev, cur, prev]
        scratch_k.append(pltpu.VMEM((d, BLK + rpc, GROUP_WIDTH), BF16))
        scratch_v.append(pltpu.VMEM((d, BLK + rpc, GROUP_WIDTH), BF16))
    scratch = scratch_k + scratch_v + [
        pltpu.VMEM((sb, GROUP_WIDTH), F32), pltpu.VMEM((sb, GROUP_WIDTH), F32),
        pltpu.VMEM((N_GROUPS * GROUP_WIDTH // LANES, sb, LANES), F32),
        pltpu.VMEM((N_GROUPS * GROUP_WIDTH // LANES, sb, LANES), F32),
        pltpu.VMEM((2, HEADS_PER_GROUP * BLK, 2 * BLK), F32)]
    return pl.pallas_call(
        _prompt_attn_kernel,
        grid=(b, nsb),
        in_specs=in_specs,
        out_specs=pl.BlockSpec((None, sb, GROUP_WIDTH), lambda bi, t: (bi, t, 0)),
        out_shape=jax.ShapeDtypeStruct((b, s, GROUP_WIDTH), BF16),
        scratch_shapes=scratch,
        compiler_params=_cparams("parallel", "parallel"),
        name="prompt_attention",
    )(*ins)


def _sigmoid(a):
    return 1.0 / (1.0 + jnp.exp(-a))


def _merge_ffn(x, pool_p, attn_y, a_pool, a_attn, mod_ref, n2_ref, wpg_ref, ps_ref, wpb_ref, wab_ref,
               wo_ref, wup_ref, wdn_ref):
    ys = [_dot(p.astype(BF16), wpg_ref[g]) for g, p in enumerate(pool_p)]
    pool_y = jnp.concatenate(ys, axis=-1) * ps_ref[...]
    pb = _dot(pool_y.astype(BF16), wpb_ref[...])
    ab = _dot(attn_y.astype(BF16), wab_ref[...])
    merged = _sigmoid(a_pool.astype(F32)) * pb + _sigmoid(a_attn.astype(F32)) * ab
    g1 = mod_ref[:, 2 * D_MODEL:3 * D_MODEL]
    x1 = x + g1 * _dot(merged.astype(BF16), wo_ref[...])
    h2 = _rms_modulate(x1, n2_ref[...], mod_ref[:, 3 * D_MODEL:4 * D_MODEL], mod_ref[:, 4 * D_MODEL:5 * D_MODEL])
    h2b = h2.astype(BF16)
    ff_chunk = 1024
    y = jnp.zeros_like(x)
    for c in range(D_FF // ff_chunk):
        hid = jnp.maximum(_dot(h2b, wup_ref[:, c * ff_chunk:(c + 1) * ff_chunk]), 0.0)
        y = y + _dot((hid * hid).astype(BF16), wdn_ref[c * ff_chunk:(c + 1) * ff_chunk, :])
    g2 = mod_ref[:, 5 * D_MODEL:6 * D_MODEL]
    return x1 + g2 * y


def _prompt_ffn_kernel(x_ref, y_ref, u_ref, up_ref, ap_ref, aa_ref, mod_ref, n2_ref, wpg_ref, ps_ref,
                       wpb_ref, wab_ref, wo_ref, wup_ref, wdn_ref,
                       qt_ref, kt_ref, vt_ref, ck0, cv0, ck1, cv1, ck2, cv2,
                       o_ref, yt_ref, ext_ref):
    i = pl.program_id(1)
    step = pl.program_id(0) * pl.num_programs(1) + i
    _sample_attn_step(step, qt_ref, kt_ref, vt_ref, ((ck0, cv0), (ck1, cv1), (ck2, cv2)), yt_ref)
    tm = x_ref.shape[0]
    hist = up_ref.shape[0]
    ext_ref[0:hist, :] = jnp.where(i > 0, up_ref[...], 0.0)
    ext_ref[hist:, :] = u_ref[...]
    pos = (i * tm + lax.broadcasted_iota(jnp.int32, (tm, 1), 0)).astype(F32)
    pool_p = []
    for g, w in enumerate(POOL_WINDOWS):
        e = ext_ref[:, g * POOL_GROUP:(g + 1) * POOL_GROUP]
        acc = e
        span = 1
        while span < w:
            acc = acc + pltpu.roll(acc, span, 0)
            span *= 2
        cnt = jnp.minimum(pos + 1.0, float(w))
        pool_p.append(acc[hist:] / cnt - e[hist:])
    o_ref[...] = _merge_ffn(x_ref[...], pool_p, y_ref[...], ap_ref[...], aa_ref[...], mod_ref, n2_ref, wpg_ref,
                            ps_ref, wpb_ref, wab_ref, wo_ref, wup_ref, wdn_ref)


def _prompt_ffn(layer, x, attn_y, u, a_pool, a_attn, mod_p, n2, wpg, ps, wpb, wab, wo, wup, wdn,
                n, qt, kt, vt, cache_views):
    b, s, _ = x.shape
    tm = min(PROMPT_TILE, s)
    nt = s // tm
    assert n % (b * nt) == 0
    bt = n // (b * nt)
    hist = 16
    row = lambda width: pl.BlockSpec((None, tm, width), lambda bi, i: (bi, i, 0))
    by_head = (N_GROUPS, HEADS_PER_GROUP, HEAD_DIM, LANES)
    new = pl.BlockSpec(by_head, lambda bi, i: (0, 0, 0, 0))
    in_specs = [
        row(D_MODEL), row(GROUP_WIDTH), row(POOL_WIDTH),
        pl.BlockSpec((None, hist, POOL_WIDTH), lambda bi, i: (bi, jnp.maximum(i * (tm // hist) - 1, 0), 0)),
        row(D_MODEL), row(D_MODEL),
        pl.BlockSpec((None, None, 1, 6 * D_MODEL), lambda bi, i: (layer, bi, 0, 0)),
    ] + [_layer_spec(w, layer) for w in (n2, wpg, ps, wpb, wab, wo, wup, wdn)] + [new, new, new]
    for c in cache_views:
        in_specs.append(pl.BlockSpec((None, bt) + c.shape[2:], lambda bi, i: (layer, bi * nt + i, 0, 0, 0)))
    x_out, yt = pl.pallas_call(
        _prompt_ffn_kernel,
        grid=(b, nt),
        in_specs=in_specs,
        out_specs=[row(D_MODEL), pl.BlockSpec(by_head[1:], lambda bi, i: (0, 0, 0))],
        out_shape=[jax.ShapeDtypeStruct((b, s, D_MODEL), F32), jax.ShapeDtypeStruct(by_head[1:], F32)],
        scratch_shapes=[pltpu.VMEM((hist + tm, POOL_WIDTH), F32)],
        compiler_params=_cparams("arbitrary", "arbitrary"),
        name="prompt_merge_ffn",
    )(x, attn_y, u, u, a_pool, a_attn, mod_p, n2, wpg, ps, wpb, wab, wo, wup, wdn,
      qt.reshape(by_head), kt.reshape(by_head), vt.reshape(by_head), *cache_views)
    return x_out, yt.reshape(GROUP_WIDTH, LANES)


def _sample_ffn_kernel(x_ref, yt_ref, u_ref, st_ref, ap_ref, aa_ref, mod_ref, n2_ref, wpg_ref, ps_ref,
                       wpb_ref, wab_ref, wo_ref, wup_ref, wdn_ref, o_ref):
    n = x_ref.shape[0]
    u = u_ref[...]
    attn_y = yt_ref[...].T[0:n, :]
    tail = jnp.zeros_like(u)
    sums = {}
    for back in range(1, POOL_HIST + 1):
        row = POOL_HIST - back
        tail = tail + st_ref[:, row * POOL_WIDTH:(row + 1) * POOL_WIDTH]
        if back + 1 in POOL_WINDOWS:
            sums[back + 1] = tail
    pool_p = []
    for g, w in enumerate(POOL_WINDOWS):
        cols = slice(g * POOL_GROUP, (g + 1) * POOL_GROUP)
        cnt = min(PAST_LEN + 1.0, float(w))
        pool_p.append((sums[w][:, cols] + u[:, cols]) / cnt - u[:, cols])
    o_ref[...] = _merge_ffn(x_ref[...], pool_p, attn_y, ap_ref[...], aa_ref[...], mod_ref, n2_ref, wpg_ref,
                            ps_ref, wpb_ref, wab_ref, wo_ref, wup_ref, wdn_ref)


def _sample_ffn(layer, x, attn_yt, u, state2d, a_pool, a_attn, mod_s, n2, wpg, ps, wpb, wab, wo, wup, wdn):
    n = x.shape[0]
    full = lambda a: pl.BlockSpec(a.shape, lambda i: (0,) * a.ndim)
    ins = (x, attn_yt, u, state2d, a_pool, a_attn, mod_s, n2, wpg, ps, wpb, wab, wo, wup, wdn)
    in_specs = [full(a) for a in ins[:3]] + [_layer_spec(state2d, layer)] + [full(a) for a in ins[4:6]]
    in_specs += [_layer_spec(a, layer) for a in ins[6:]]
    return pl.pallas_call(
        _sample_ffn_kernel,
        grid=(1,),
        in_specs=in_specs,
        out_specs=pl.BlockSpec((n, D_MODEL), lambda i: (0, 0)),
        out_shape=jax.ShapeDtypeStruct((n, D_MODEL), F32),
        compiler_params=_cparams("arbitrary"),
        name="sample_merge_ffn",
    )(*ins)


def _sample_attn_step(step, q_ref, k_ref, v_ref, caches, o_ref):
    bt = caches[0][0].shape[0]
    lane = lax.broadcasted_iota(jnp.int32, (HEAD_DIM, LANES), 1)

    @pl.when(step == 0)
    def _():
        o_ref[...] = jnp.zeros_like(o_ref)

    for b in range(bt):
        mine = lane == step * bt + b
        for h in range(HEADS_PER_GROUP):
            column = lambda ref, g: jnp.sum(jnp.where(mine, ref[g, h], 0.0), axis=1, keepdims=True)
            outs, lses = [], []
            for g, (ck, cv) in enumerate(caches):
                d = DILATIONS[g]
                q = column(q_ref, g)
                s = jnp.sum(ck[b, h] * q, axis=0, keepdims=True)
                if d > 1:
                    row = lax.broadcasted_iota(jnp.int32, s.shape, 1)
                    s = jnp.where(row % d == 0, s, -jnp.inf)
                s_new = jnp.sum(column(k_ref, g) * q, axis=0, keepdims=True)
                m = jnp.maximum(jnp.max(s, axis=1, keepdims=True), s_new)
                p = jnp.exp(s - m)
                p_new = jnp.exp(s_new - m)
                den = jnp.sum(p, axis=1, keepdims=True) + p_new
                acc = jnp.sum(cv[b, h] * p, axis=1, keepdims=True) + p_new * column(v_ref, g)
                outs.append(acc / den)
                lses.append(m + jnp.log(den))
            mx = jnp.maximum(jnp.maximum(lses[0], lses[1]), lses[2])
            es = [jnp.exp(l - mx) for l in lses]
            merged = (es[0] * outs[0] + es[1] * outs[1] + es[2] * outs[2]) / (es[0] + es[1] + es[2])
            o_ref[h] = jnp.where(mine, merged, o_ref[h])


def _rope_tables(pos):
    inv_freq = ROPE_THETA ** (-jnp.arange(0, HEAD_DIM, 2, dtype=F32) / HEAD_DIM)
    ang = pos[:, None] * inv_freq[None, :]
    cos, sin = jnp.cos(ang), jnp.sin(ang)
    zero = jnp.zeros_like(sin)
    reps = LANES // HEAD_DIM
    cos_t = jnp.tile(jnp.concatenate([cos, cos], axis=-1), (1, reps))
    sa_t = jnp.tile(jnp.concatenate([-sin, zero], axis=-1), (1, reps))
    sb_t = jnp.tile(jnp.concatenate([zero, sin], axis=-1), (1, reps))
    return cos_t, sa_t, sb_t


def _segment_matrix():
    lane = jnp.arange(GROUP_WIDTH)
    seg = (lane[:, None] // HEAD_DIM == lane[None, :] // HEAD_DIM).astype(F32) / HEAD_DIM
    return seg.astype(BF16)


def kernel(x_prompt, x_sample, cache_k_w128, cache_v_w128, cache_k_w512, cache_v_w512, cache_k_w2048,
           cache_v_w2048, state_pool, c_prompt, c_sample, norm1_g, norm2_g, w_ada, b_ada, w_in, q_norm_g,
           k_norm_g, w_pool_grp, pool_scale, w_pool_br, w_attn_br, w_out, w_up, w_down):
    depth = w_in.shape[0]
    b, s, _ = x_prompt.shape
    n = x_sample.shape[0]
    assert x_sample.shape[1] == 1 and s % min(SUPER, s) == 0 and s % min(PROMPT_TILE, s) == 0

    mod_p, mod_s = _modulation(c_prompt, c_sample, w_ada, b_ada)
    seg = _segment_matrix()
    tab_p = _rope_tables(jnp.arange(s, dtype=F32))
    tab_s = _rope_tables(PAST_LEN + jnp.arange(1, dtype=F32))

    caches = []
    for ck, cv, d in zip((cache_k_w128, cache_k_w512, cache_k_w2048),
                         (cache_v_w128, cache_v_w512, cache_v_w2048), DILATIONS):
        for c in (ck, cv):
            assert c.shape[2] == N_KEYS * d
            caches.append(jnp.transpose(c, (0, 1, 3, 4, 2)))
    state2d = state_pool.reshape(depth, n, POOL_HIST * POOL_WIDTH)

    bf = lambda a: a.astype(BF16)
    w_in_b, wpg_b, wpb_b, wab_b, wo_b, wup_b, wdn_b = map(
        bf, (w_in, w_pool_grp, w_pool_br, w_attn_br, w_out, w_up, w_down))

    n1 = norm1_g.reshape(depth, 1, D_MODEL)
    n2 = norm2_g.reshape(depth, 1, D_MODEL)
    gq = jnp.tile(q_norm_g, (1, ATTN_WIDTH // HEAD_DIM)).reshape(depth, 1, ATTN_WIDTH)
    gk = jnp.tile(k_norm_g, (1, ATTN_WIDTH // HEAD_DIM)).reshape(depth, 1, ATTN_WIDTH)
    ps = pool_scale.reshape(depth, 1, POOL_WIDTH)
    tail_w = (n2, wpg_b, ps, wpb_b, wab_b, wo_b, wup_b, wdn_b)

    xp = x_prompt
    xs = x_sample.reshape(n, D_MODEL)
    kp = [[] for _ in range(N_GROUPS)]
    vp = [[] for _ in range(N_GROUPS)]
    ks = [[] for _ in range(N_GROUPS)]
    vs = [[] for _ in range(N_GROUPS)]
    pool_p, pool_s = [], []
    for l in range(depth):
        outs = _prompt_project(l, xp, mod_p, n1, w_in_b, gq, gk, seg, *tab_p)
        u, a_pool, a_attn = outs[0:3]
        qd, kd, vd = outs[3:6], outs[6:9], outs[9:12]
        kws, vws = outs[12:15], outs[15:18]
        attn_y = _prompt_attention(qd, kd, vd)
        u_s, ap_s, aa_s, k_s, v_s, qt, kt, vt = _sample_project(l, xs, mod_s, n1, w_in_b, gq, gk, seg, *tab_s)
        xp, yt_s = _prompt_ffn(l, xp, attn_y, u, a_pool, a_attn, mod_p, *tail_w, n, qt, kt, vt, caches)
        for g in range(N_GROUPS):
            kp[g].append(kws[g].reshape(b, -1, HEADS_PER_GROUP, HEAD_DIM))
            vp[g].append(vws[g].reshape(b, -1, HEADS_PER_GROUP, HEAD_DIM))
        pool_p.append(u[:, s - POOL_HIST:, :])

        xs = _sample_ffn(l, xs, yt_s, u_s, state2d, ap_s, aa_s, mod_s, *tail_w)
        kh, vh = (a.reshape(n, N_GROUPS, HEADS_PER_GROUP, HEAD_DIM) for a in (k_s, v_s))
        for g in range(N_GROUPS):
            ks[g].append(kh[:, g:g + 1])
            vs[g].append(vh[:, g:g + 1])
        pool_s.append(jnp.concatenate([state_pool[l][:, 1:], u_s[:, None, :]], axis=1))

    st = lambda rows: jnp.stack(rows, axis=0)
    return (xp, xs.reshape(n, 1, D_MODEL),
            st(kp[0]), st(vp[0]), st(kp[1]), st(vp[1]), st(kp[2]), st(vp[2]), st(pool_p),
            st(ks[0]), st(vs[0]), st(ks[1]), st(vs[1]), st(ks[2]), st(vs[2]), st(pool_s))
```

```python
import functools
import math

import jax
import jax.numpy as jnp
from jax import lax
from jax.experimental import pallas as pl
from jax.experimental.pallas import tpu as pltpu

F32 = jnp.float32
BF16 = jnp.bfloat16

D_MODEL = 1024
DEPTH = 4
PAST_LEN = 8192
POOL_WIDTH = 512
POOL_WINDOWS = (2, 4, 8, 16)
POOL_GROUP = 128
POOL_HIST = 15
HEAD_DIM = 64
HEADS_PER_GROUP = 4
GROUP_WIDTH = HEADS_PER_GROUP * HEAD_DIM
DILATIONS = (1, 4, 16)
WINDOWS = (128, 512, 2048)
N_GROUPS = 3
ATTN_WIDTH = N_GROUPS * GROUP_WIDTH
N_KEYS = 128
BLK = 128
D_FF = 4096
ROPE_THETA = 10000.0
EPS = 1e-6
Q_SCALE = 1.0 / math.sqrt(HEAD_DIM)
LOG2E = math.log2(math.e)
LN2 = math.log(2.0)

COL_U = 0
COL_Q = POOL_WIDTH
COL_K = COL_Q + ATTN_WIDTH
COL_V = COL_K + ATTN_WIDTH
COL_AP = COL_V + ATTN_WIDTH
COL_AA = COL_AP + D_MODEL
IN_WIDTH = COL_AA + D_MODEL

LANES = 128
PROMPT_TILE = 512
SUPER = BLK * DILATIONS[-1]
CHAIN_HEADS = 4
VMEM_LIMIT = 56 * 1024 * 1024


def _cparams(*sem):
    return pltpu.CompilerParams(dimension_semantics=sem, vmem_limit_bytes=VMEM_LIMIT)


def _const_spec(shape):
    nd = len(shape)
    return pl.BlockSpec(shape, lambda *_: (0,) * nd, pipeline_mode=pl.Buffered(1))


def _layer_spec(arr, layer):
    nd = arr.ndim - 1
    first = layer if arr.shape[0] > 1 else 0
    return pl.BlockSpec((None,) + arr.shape[1:], lambda *_: (first,) + (0,) * nd, pipeline_mode=pl.Buffered(1))


def _dot(a, b):
    return jnp.dot(a, b, preferred_element_type=F32)


def _rms_modulate(x, gain, shift, scale):
    ms = jnp.mean(x * x, axis=-1, keepdims=True)
    return (x * lax.rsqrt(ms + EPS) * gain) * (1.0 + scale) + shift


def _mod_kernel(cp_ref, cs_ref, w_ref, b_ref, op_ref, os_ref):
    w = w_ref[...].astype(BF16)

    def mod(c):
        s = c * (1.0 / (1.0 + jnp.exp(-c)))
        return _dot(s.astype(BF16), w) + b_ref[...]

    mp = mod(cp_ref[...])
    for bi in range(op_ref.shape[0]):
        op_ref[bi] = mp[bi:bi + 1, :]
    os_ref[...] = mod(cs_ref[...])


def _modulation(c_prompt, c_sample, w_ada, b_ada):
    b, n = c_prompt.shape[0], c_sample.shape[0]
    depth = w_ada.shape[0]
    tn = 1536
    return pl.pallas_call(
        _mod_kernel,
        grid=(depth, 6 * D_MODEL // tn),
        in_specs=[
            pl.BlockSpec((b, D_MODEL), lambda l, j: (0, 0)),
            pl.BlockSpec((n, D_MODEL), lambda l, j: (0, 0)),
            pl.BlockSpec((None, D_MODEL, tn), lambda l, j: (l, 0, j)),
            pl.BlockSpec((None, 1, tn), lambda l, j: (l, 0, j)),
        ],
        out_specs=[pl.BlockSpec((None, b, 1, tn), lambda l, j: (l, 0, 0, j)),
                   pl.BlockSpec((None, n, tn), lambda l, j: (l, 0, j))],
        out_shape=[jax.ShapeDtypeStruct((depth, b, 1, 6 * D_MODEL), F32),
                   jax.ShapeDtypeStruct((depth, n, 6 * D_MODEL), F32)],
        compiler_params=_cparams("parallel", "parallel"),
        name="adaln_mod",
    )(c_prompt, c_sample, w_ada, b_ada.reshape(depth, 1, 6 * D_MODEL))


def _head_norm_rope(z, gain_ref, seg_ref, cos_ref, sa_ref, sb_ref, out_scale, store):
    cos = cos_ref[...]
    sa = sa_ref[...]
    sb = sb_ref[...]
    for c in range(ATTN_WIDTH // GROUP_WIDTH):
        zc = z[:, c * GROUP_WIDTH:(c + 1) * GROUP_WIDTH]
        ms = _dot((zc * zc).astype(BF16), seg_ref[...])
        n = zc * lax.rsqrt(ms + EPS) * gain_ref[:, c * GROUP_WIDTH:(c + 1) * GROUP_WIDTH]
        for half in range(GROUP_WIDTH // LANES):
            xx = n[:, half * LANES:(half + 1) * LANES]
            r = xx * cos + pltpu.roll(xx, LANES - HEAD_DIM // 2, 1) * sa + pltpu.roll(xx, HEAD_DIM // 2, 1) * sb
            store(c * (GROUP_WIDTH // LANES) + half, r * out_scale if out_scale != 1.0 else r)


def _project(x_ref, mod_ref, n1_ref, w_ref, gq_ref, gk_ref, seg_ref, cos_ref, sa_ref, sb_ref,
             store_q, store_k, store_v, q_scale, after_qkv=None):
    x = x_ref[...]
    h = _rms_modulate(x, n1_ref[...], mod_ref[:, 0:D_MODEL], mod_ref[:, D_MODEL:2 * D_MODEL])
    hb = h.astype(BF16)
    _head_norm_rope(_dot(hb, w_ref[:, COL_Q:COL_K]), gq_ref, seg_ref, cos_ref, sa_ref, sb_ref, q_scale, store_q)
    _head_norm_rope(_dot(hb, w_ref[:, COL_K:COL_V]), gk_ref, seg_ref, cos_ref, sa_ref, sb_ref, 1.0, store_k)
    v = _dot(hb, w_ref[:, COL_V:COL_AP])
    for c in range(ATTN_WIDTH // LANES):
        store_v(c, v[:, c * LANES:(c + 1) * LANES])
    if after_qkv is not None:
        after_qkv()
    u = _dot(hb, w_ref[:, COL_U:COL_Q])
    a_pool = _dot(hb, w_ref[:, COL_AP:COL_AA])
    a_attn = _dot(hb, w_ref[:, COL_AA:IN_WIDTH])
    return u, a_pool, a_attn


def _chunk_store(ref):
    def store(c, val):
        ref[c] = val
    return store


def _lane_store(ref, rows):
    def store(c, val):
        ref[0:rows, c * LANES:(c + 1) * LANES] = val
    return store


def _prompt_proj_kernel(*refs, n_cast):
    (x_ref, mod_ref, n1_ref, w_ref, gq_ref, gk_ref, seg_ref, cos_ref, sa_ref, sb_ref), refs = refs[:10], refs[10:]
    cast_in, refs = refs[:n_cast], refs[n_cast:]
    (u_ref, ap_ref, aa_ref, q0, q1, q2, k0, k1, k2, v0, v1, v2,
     kw0, kw1, kw2, vw0, vw1, vw2), refs = refs[:18], refs[18:]
    cast_out, (qs_ref, ks_ref, vs_ref) = refs[:n_cast], refs[n_cast:]
    tm = x_ref.shape[0]
    for src, dst in zip(cast_in, cast_out):
        dst[...] = src[...].astype(BF16)

    def write_qkv():
        for src, dsts in ((qs_ref, (q0, q1, q2)), (ks_ref, (k0, k1, k2)), (vs_ref, (v0, v1, v2))):
            for g, dst in enumerate(dsts):
                d = DILATIONS[g]
                for r in range(d):
                    rows = pl.ds(r, tm // d, stride=d) if d > 1 else slice(None)
                    for half in range(GROUP_WIDTH // LANES):
                        c = g * (GROUP_WIDTH // LANES) + half
                        dst[r, :, half * LANES:(half + 1) * LANES] = src[c, rows, :].astype(BF16)
        for src, dsts in ((ks_ref, (kw0, kw1, kw2)), (vs_ref, (vw0, vw1, vw2))):
            for g, dst in enumerate(dsts):
                keep = dst.shape[0]
                for half in range(GROUP_WIDTH // LANES):
                    c = g * (GROUP_WIDTH // LANES) + half
                    dst[:, half * LANES:(half + 1) * LANES] = src[c, tm - keep:, :]

    u, a_pool, a_attn = _project(x_ref, mod_ref, n1_ref, w_ref, gq_ref, gk_ref, seg_ref, cos_ref, sa_ref, sb_ref,
                                 _chunk_store(qs_ref), _chunk_store(ks_ref), _chunk_store(vs_ref),
                                 Q_SCALE * LOG2E,
                                 after_qkv=write_qkv)
    u_ref[...] = u
    ap_ref[...] = a_pool.astype(BF16)
    aa_ref[...] = a_attn.astype(BF16)


def _prompt_project(layer, x, mod_p, n1, w_in, gq, gk, seg, cos, sa, sb, to_cast):
    b, s, _ = x.shape
    tm = min(PROMPT_TILE, s)
    nt = s // tm
    steps = b * nt
    row = lambda width: pl.BlockSpec((None, tm, width), lambda bi, i: (bi, i, 0))
    in_specs = [
        row(D_MODEL),
        pl.BlockSpec((None, None, 1, 6 * D_MODEL), lambda bi, i: (layer, bi, 0, 0)),
        _layer_spec(n1, layer),
        _layer_spec(w_in, layer),
        _layer_spec(gq, layer),
        _layer_spec(gk, layer),
        _const_spec((GROUP_WIDTH, GROUP_WIDTH)),
        pl.BlockSpec((tm, LANES), lambda bi, i: (i, 0)),
        pl.BlockSpec((tm, LANES), lambda bi, i: (i, 0)),
        pl.BlockSpec((tm, LANES), lambda bi, i: (i, 0)),
    ]
    out_shape = [jax.ShapeDtypeStruct((b, s, POOL_WIDTH), F32),
                 jax.ShapeDtypeStruct((b, s, D_MODEL), BF16),
                 jax.ShapeDtypeStruct((b, s, D_MODEL), BF16)]
    out_specs = [row(POOL_WIDTH), row(D_MODEL), row(D_MODEL)]
    for _ in range(3):
        for d in DILATIONS:
            out_shape.append(jax.ShapeDtypeStruct((b, d, s // d, GROUP_WIDTH), BF16))
            out_specs.append(pl.BlockSpec((None, d, tm // d, GROUP_WIDTH), lambda bi, i: (bi, 0, i, 0)))
    for _ in range(2):
        for w in WINDOWS:
            keep = min(w, s)
            out_shape.append(jax.ShapeDtypeStruct((b, keep, GROUP_WIDTH), F32))
            if keep >= tm:
                first = (s - keep) // tm
                out_specs.append(pl.BlockSpec(
                    (None, tm, GROUP_WIDTH), lambda bi, i, first=first: (bi, jnp.maximum(i - first, 0), 0)))
            else:
                out_specs.append(pl.BlockSpec((None, keep, GROUP_WIDTH), lambda bi, i: (bi, 0, 0)))
    for w, wl in to_cast:
        _, k, nn = w.shape
        assert k % (steps * 16) == 0
        in_specs.append(pl.BlockSpec((None, k // steps, nn), lambda bi, i, wl=wl: (wl, bi * nt + i, 0)))
        out_specs.append(pl.BlockSpec((None, k // steps, nn), lambda bi, i: (0, bi * nt + i, 0)))
        out_shape.append(jax.ShapeDtypeStruct((1, k, nn), BF16))
    return pl.pallas_call(
        functools.partial(_prompt_proj_kernel, n_cast=len(to_cast)),
        grid=(b, nt),
        in_specs=in_specs,
        out_specs=out_specs,
        out_shape=out_shape,
        scratch_shapes=[pltpu.VMEM((ATTN_WIDTH // LANES, tm, LANES), F32)] * 3,
        compiler_params=_cparams("parallel", "arbitrary"),
        name="prompt_project",
    )(x, mod_p, n1, w_in, gq, gk, seg, cos, sa, sb, *[w for w, _ in to_cast])


def _sample_proj_kernel(x_ref, mod_ref, n1_ref, w_ref, gq_ref, gk_ref, seg_ref, cos_ref, sa_ref, sb_ref,
                        u_ref, ap_ref, aa_ref, k_ref, v_ref, qt_ref, kt_ref, vt_ref, q_sc, k_sc, v_sc):
    n = x_ref.shape[0]
    for sc in (q_sc, k_sc, v_sc):
        sc[...] = jnp.zeros_like(sc)
    u, a_pool, a_attn = _project(x_ref, mod_ref, n1_ref, w_ref, gq_ref, gk_ref, seg_ref, cos_ref, sa_ref, sb_ref,
                                 _lane_store(q_sc, n), _lane_store(k_sc, n), _lane_store(v_sc, n), Q_SCALE)
    u_ref[...] = u
    ap_ref[...] = a_pool
    aa_ref[...] = a_attn
    k_ref[...] = k_sc[0:n, :]
    v_ref[...] = v_sc[0:n, :]
    for sc, dst in ((q_sc, qt_ref), (k_sc, kt_ref), (v_sc, vt_ref)):
        for c in range(ATTN_WIDTH // LANES):
            dst[c * LANES:(c + 1) * LANES, :] = sc[:, c * LANES:(c + 1) * LANES].T


def _sample_project(layer, x, mod_s, n1, w_in, gq, gk, seg, cos, sa, sb):
    n = x.shape[0]
    assert n <= LANES
    full = lambda shape: pl.BlockSpec(shape, lambda i: (0,) * len(shape))
    in_specs = [full(x.shape), _layer_spec(mod_s, layer), _layer_spec(n1, layer), _layer_spec(w_in, layer),
                _layer_spec(gq, layer), _layer_spec(gk, layer), full(seg.shape), full(cos.shape), full(sa.shape),
                full(sb.shape)]
    shapes = [(n, POOL_WIDTH), (n, D_MODEL), (n, D_MODEL), (n, ATTN_WIDTH), (n, ATTN_WIDTH)] + [(ATTN_WIDTH, LANES)] * 3
    return pl.pallas_call(
        _sample_proj_kernel,
        grid=(1,),
        in_specs=in_specs,
        out_specs=[full(sh) for sh in shapes],
        out_shape=[jax.ShapeDtypeStruct(sh, F32) for sh in shapes],
        scratch_shapes=[pltpu.VMEM((LANES, ATTN_WIDTH), F32)] * 3,
        compiler_params=_cparams("arbitrary"),
        name="sample_project",
    )(x, mod_s, n1, w_in, gq, gk, seg, cos, sa, sb)


def _head_lane_masks():
    lane = lax.broadcasted_iota(jnp.int32, (1, GROUP_WIDTH), 1)
    return [(lane // HEAD_DIM) == h for h in range(HEADS_PER_GROUP)]


def _prompt_attn_kernel(*refs):
    ins, o_ref = refs[:15], refs[15]
    nat_o, nat_l, bias_sc = refs[16:19]
    t = pl.program_id(1)
    hm = _head_lane_masks()
    heads_per_chain = CHAIN_HEADS
    stacked = heads_per_chain * BLK
    row = lax.broadcasted_iota(jnp.int32, (stacked, 2 * BLK), 0)
    col = lax.broadcasted_iota(jnp.int32, (stacked, 2 * BLK), 1)
    dist = BLK + (row % BLK) - col
    band = (dist >= 0) & (dist <= N_KEYS)
    bias_sc[0] = jnp.where(band, 0.0, -jnp.inf)
    bias_sc[1] = jnp.where(band & (col >= BLK), 0.0, -jnp.inf)

    for g in range(N_GROUPS):
        d = DILATIONS[g]
        q_ref, kc_ref, kp_ref, vc_ref, vp_ref = ins[5 * g:5 * g + 5]
        nb = kc_ref.shape[1] // BLK
        halves = GROUP_WIDTH // LANES

        for r in range(d):
            for j in range(nb):
                q = q_ref[r, j * BLK:(j + 1) * BLK, :]
                if j == 0:
                    keys = jnp.concatenate([kp_ref[r], kc_ref[r, 0:BLK, :]], axis=0)
                    vals = jnp.concatenate([vp_ref[r], vc_ref[r, 0:BLK, :]], axis=0)
                    bias = bias_sc[jnp.where(t > 0, 0, 1)]
                else:
                    keys = kc_ref[r, (j - 1) * BLK:(j + 1) * BLK, :]
                    vals = vc_ref[r, (j - 1) * BLK:(j + 1) * BLK, :]
                    bias = bias_sc[0]
                zero = jnp.zeros_like(q)
                o_u = jnp.zeros((BLK, GROUP_WIDTH), F32)
                l_u = jnp.zeros((BLK, GROUP_WIDTH), F32)
                for c in range(HEADS_PER_GROUP // heads_per_chain):
                    heads = range(c * heads_per_chain, (c + 1) * heads_per_chain)
                    qst = jnp.concatenate([jnp.where(hm[h], q, zero) for h in heads], axis=0)
                    s = lax.dot_general(qst, keys, (((1,), (1,)), ((), ())), preferred_element_type=F32) + bias
                    m = jnp.max(s, axis=-1, keepdims=True)
                    p = jnp.exp2(s - m)
                    den = jnp.sum(p, axis=-1, keepdims=True)
                    o = _dot(p.astype(BF16), vals)
                    inv = 1.0 / den
                    lse = m * LN2 + jnp.log(den)
                    for k, h in enumerate(heads):
                        o_u = jnp.where(hm[h], o[k * BLK:(k + 1) * BLK] * inv[k * BLK:(k + 1) * BLK], o_u)
                        l_u = jnp.where(hm[h], lse[k * BLK:(k + 1) * BLK], l_u)
                first = j * BLK * d + r
                rows = pl.ds(first, BLK, stride=d) if d > 1 else slice(first, first + BLK)
                for half in range(halves):
                    lanes = slice(half * LANES, (half + 1) * LANES)
                    nat_o[g * halves + half, rows, :] = o_u[:, lanes]
                    nat_l[g * halves + half, rows, :] = l_u[:, lanes]

    chunk = 256
    halves = GROUP_WIDTH // LANES

    def merge(ci, carry):
        rows = pl.ds(pl.multiple_of(ci * chunk, chunk), chunk)
        for half in range(halves):
            ls = [nat_l[g * halves + half, rows, :] for g in range(N_GROUPS)]
            os_ = [nat_o[g * halves + half, rows, :] for g in range(N_GROUPS)]
            m = jnp.maximum(jnp.maximum(ls[0], ls[1]), ls[2])
            es = [jnp.exp(l - m) for l in ls]
            tot = es[0] + es[1] + es[2]
            acc = es[0] * os_[0] + es[1] * os_[1] + es[2] * os_[2]
            o_ref[rows, half * LANES:(half + 1) * LANES] = (acc / tot).astype(o_ref.dtype)
        return carry

    lax.fori_loop(0, o_ref.shape[0] // chunk, merge, 0)


def _prompt_attention(qd, kd, vd):
    b = qd[0].shape[0]
    s = qd[0].shape[2]
    sb = min(SUPER, s)
    nsb = s // sb
    ins, in_specs = [], []
    for g, d in enumerate(DILATIONS):
        rpc = sb // d
        nb = rpc // BLK
        cur = pl.BlockSpec((None, d, rpc, GROUP_WIDTH), lambda bi, t: (bi, 0, t, 0))
        prev = pl.BlockSpec((None, d, BLK, GROUP_WIDTH),
                            lambda bi, t, nb=nb: (bi, 0, jnp.maximum(t * nb - 1, 0), 0))
        ins += [qd[g], kd[g], kd[g], vd[g], vd[g]]
        in_specs += [cur, cur, prev, cur, prev]
    scratch = [
        pltpu.VMEM((N_GROUPS * GROUP_WIDTH // LANES, sb, LANES), F32),
        pltpu.VMEM((N_GROUPS * GROUP_WIDTH // LANES, sb, LANES), F32),
        pltpu.VMEM((2, CHAIN_HEADS * BLK, 2 * BLK), F32)]
    return pl.pallas_call(
        _prompt_attn_kernel,
        grid=(b, nsb),
        in_specs=in_specs,
        out_specs=pl.BlockSpec((None, sb, GROUP_WIDTH), lambda bi, t: (bi, t, 0)),
        out_shape=jax.ShapeDtypeStruct((b, s, GROUP_WIDTH), BF16),
        scratch_shapes=scratch,
        compiler_params=_cparams("parallel", "parallel"),
        name="prompt_attention",
    )(*ins)


def _sigmoid(a):
    return 1.0 / (1.0 + jnp.exp(-a))


def _merge_ffn(x, pool_p, attn_y, a_pool, a_attn, mod_ref, n2_ref, wpg_ref, ps_ref, wpb_ref, wab_ref,
               wo_ref, wup_ref, wdn_ref):
    ys = [_dot(p.astype(BF16), wpg_ref[g]) for g, p in enumerate(pool_p)]
    pool_y = jnp.concatenate(ys, axis=-1) * ps_ref[...]
    pb = _dot(pool_y.astype(BF16), wpb_ref[...])
    ab = _dot(attn_y.astype(BF16), wab_ref[...])
    merged = _sigmoid(a_pool.astype(F32)) * pb + _sigmoid(a_attn.astype(F32)) * ab
    g1 = mod_ref[:, 2 * D_MODEL:3 * D_MODEL]
    x1 = x + g1 * _dot(merged.astype(BF16), wo_ref[...])
    h2 = _rms_modulate(x1, n2_ref[...], mod_ref[:, 3 * D_MODEL:4 * D_MODEL], mod_ref[:, 4 * D_MODEL:5 * D_MODEL])
    h2b = h2.astype(BF16)
    ff_chunk = 1024
    y = jnp.zeros_like(x)
    for c in range(D_FF // ff_chunk):
        hid = jnp.maximum(_dot(h2b, wup_ref[:, c * ff_chunk:(c + 1) * ff_chunk]), 0.0)
        y = y + _dot((hid * hid).astype(BF16), wdn_ref[c * ff_chunk:(c + 1) * ff_chunk, :])
    g2 = mod_ref[:, 5 * D_MODEL:6 * D_MODEL]
    return x1 + g2 * y


def _prompt_ffn_kernel(x_ref, y_ref, u_ref, up_ref, ap_ref, aa_ref, mod_ref, n2_ref, wpg_ref, ps_ref,
                       wpb_ref, wab_ref, wo_ref, wup_ref, wdn_ref,
                       qt_ref, kt_ref, vt_ref, ck0, cv0, ck1, cv1, ck2, cv2,
                       o_ref, yt_ref, ext_ref):
    i = pl.program_id(1)
    step = pl.program_id(0) * pl.num_programs(1) + i
    _sample_attn_step(step, qt_ref, kt_ref, vt_ref, ((ck0, cv0), (ck1, cv1), (ck2, cv2)), yt_ref)
    tm = x_ref.shape[0]
    hist = up_ref.shape[0]
    ext_ref[0:hist, :] = jnp.where(i > 0, up_ref[...], 0.0)
    ext_ref[hist:, :] = u_ref[...]
    pos = (i * tm + lax.broadcasted_iota(jnp.int32, (tm, 1), 0)).astype(F32)
    pool_p = []
    for g, w in enumerate(POOL_WINDOWS):
        e = ext_ref[:, g * POOL_GROUP:(g + 1) * POOL_GROUP]
        acc = e
        span = 1
        while span < w:
            acc = acc + pltpu.roll(acc, span, 0)
            span *= 2
        cnt = jnp.minimum(pos + 1.0, float(w))
        pool_p.append(acc[hist:] / cnt - e[hist:])
    o_ref[...] = _merge_ffn(x_ref[...], pool_p, y_ref[...], ap_ref[...], aa_ref[...], mod_ref, n2_ref, wpg_ref,
                            ps_ref, wpb_ref, wab_ref, wo_ref, wup_ref, wdn_ref)


def _prompt_ffn(layer, x, attn_y, u, a_pool, a_attn, mod_p, n2, wpg, ps, wpb, wab, wo, wup, wdn,
                n, qt, kt, vt, cache_views):
    b, s, _ = x.shape
    tm = min(PROMPT_TILE, s)
    nt = s // tm
    assert n % (b * nt) == 0
    bt = n // (b * nt)
    hist = 16
    row = lambda width: pl.BlockSpec((None, tm, width), lambda bi, i: (bi, i, 0))
    by_head = (N_GROUPS, HEADS_PER_GROUP, HEAD_DIM, LANES)
    new = pl.BlockSpec(by_head, lambda bi, i: (0, 0, 0, 0))
    in_specs = [
        row(D_MODEL), row(GROUP_WIDTH), row(POOL_WIDTH),
        pl.BlockSpec((None, hist, POOL_WIDTH), lambda bi, i: (bi, jnp.maximum(i * (tm // hist) - 1, 0), 0)),
        row(D_MODEL), row(D_MODEL),
        pl.BlockSpec((None, None, 1, 6 * D_MODEL), lambda bi, i: (layer, bi, 0, 0)),
    ] + [_layer_spec(w, layer) for w in (n2, wpg, ps, wpb, wab, wo, wup, wdn)] + [new, new, new]
    for c in cache_views:
        in_specs.append(pl.BlockSpec((None, bt) + c.shape[2:], lambda bi, i: (layer, bi * nt + i, 0, 0, 0)))
    x_out, yt = pl.pallas_call(
        _prompt_ffn_kernel,
        grid=(b, nt),
        in_specs=in_specs,
        out_specs=[row(D_MODEL), pl.BlockSpec(by_head[1:], lambda bi, i: (0, 0, 0))],
        out_shape=[jax.ShapeDtypeStruct((b, s, D_MODEL), F32), jax.ShapeDtypeStruct(by_head[1:], F32)],
        scratch_shapes=[pltpu.VMEM((hist + tm, POOL_WIDTH), F32)],
        compiler_params=_cparams("arbitrary", "arbitrary"),
        name="prompt_merge_ffn",
    )(x, attn_y, u, u, a_pool, a_attn, mod_p, n2, wpg, ps, wpb, wab, wo, wup, wdn,
      qt.reshape(by_head), kt.reshape(by_head), vt.reshape(by_head), *cache_views)
    return x_out, yt.reshape(GROUP_WIDTH, LANES)


def _sample_ffn_kernel(x_ref, yt_ref, u_ref, st_ref, ap_ref, aa_ref, mod_ref, n2_ref, wpg_ref, ps_ref,
                       wpb_ref, wab_ref, wo_ref, wup_ref, wdn_ref, o_ref):
    n = x_ref.shape[0]
    u = u_ref[...]
    attn_y = yt_ref[...].T[0:n, :]
    tail = jnp.zeros_like(u)
    sums = {}
    for back in range(1, POOL_HIST + 1):
        row = POOL_HIST - back
        tail = tail + st_ref[:, row * POOL_WIDTH:(row + 1) * POOL_WIDTH]
        if back + 1 in POOL_WINDOWS:
            sums[back + 1] = tail
    pool_p = []
    for g, w in enumerate(POOL_WINDOWS):
        cols = slice(g * POOL_GROUP, (g + 1) * POOL_GROUP)
        cnt = min(PAST_LEN + 1.0, float(w))
        pool_p.append((sums[w][:, cols] + u[:, cols]) / cnt - u[:, cols])
    o_ref[...] = _merge_ffn(x_ref[...], pool_p, attn_y, ap_ref[...], aa_ref[...], mod_ref, n2_ref, wpg_ref,
                            ps_ref, wpb_ref, wab_ref, wo_ref, wup_ref, wdn_ref)


def _sample_ffn(layer, x, attn_yt, u, state2d, a_pool, a_attn, mod_s, n2, wpg, ps, wpb, wab, wo, wup, wdn):
    n = x.shape[0]
    full = lambda a: pl.BlockSpec(a.shape, lambda i: (0,) * a.ndim)
    ins = (x, attn_yt, u, state2d, a_pool, a_attn, mod_s, n2, wpg, ps, wpb, wab, wo, wup, wdn)
    in_specs = [full(a) for a in ins[:3]] + [_layer_spec(state2d, layer)] + [full(a) for a in ins[4:6]]
    in_specs += [_layer_spec(a, layer) for a in ins[6:]]
    return pl.pallas_call(
        _sample_ffn_kernel,
        grid=(1,),
        in_specs=in_specs,
        out_specs=pl.BlockSpec((n, D_MODEL), lambda i: (0, 0)),
        out_shape=jax.ShapeDtypeStruct((n, D_MODEL), F32),
        compiler_params=_cparams("arbitrary"),
        name="sample_merge_ffn",
    )(*ins)


def _sample_attn_step(step, q_ref, k_ref, v_ref, caches, o_ref):
    bt = caches[0][0].shape[0]
    lane = lax.broadcasted_iota(jnp.int32, (HEAD_DIM, LANES), 1)

    @pl.when(step == 0)
    def _():
        o_ref[...] = jnp.zeros_like(o_ref)

    for b in range(bt):
        mine = lane == step * bt + b
        for h in range(HEADS_PER_GROUP):
            column = lambda ref, g: jnp.sum(jnp.where(mine, ref[g, h], 0.0), axis=1, keepdims=True)
            outs, lses = [], []
            for g, (ck, cv) in enumerate(caches):
                d = DILATIONS[g]
                q = column(q_ref, g)
                s = jnp.sum(ck[b, h] * q, axis=0, keepdims=True)
                if d > 1:
                    row = lax.broadcasted_iota(jnp.int32, s.shape, 1)
                    s = jnp.where(row % d == 0, s, -jnp.inf)
                s_new = jnp.sum(column(k_ref, g) * q, axis=0, keepdims=True)
                m = jnp.maximum(jnp.max(s, axis=1, keepdims=True), s_new)
                p = jnp.exp(s - m)
                p_new = jnp.exp(s_new - m)
                den = jnp.sum(p, axis=1, keepdims=True) + p_new
                acc = jnp.sum(cv[b, h] * p, axis=1, keepdims=True) + p_new * column(v_ref, g)
                outs.append(acc / den)
                lses.append(m + jnp.log(den))
            mx = jnp.maximum(jnp.maximum(lses[0], lses[1]), lses[2])
            es = [jnp.exp(l - mx) for l in lses]
            merged = (es[0] * outs[0] + es[1] * outs[1] + es[2] * outs[2]) / (es[0] + es[1] + es[2])
            o_ref[h] = jnp.where(mine, merged, o_ref[h])


def _rope_tables(pos):
    inv_freq = ROPE_THETA ** (-jnp.arange(0, HEAD_DIM, 2, dtype=F32) / HEAD_DIM)
    ang = pos[:, None] * inv_freq[None, :]
    cos, sin = jnp.cos(ang), jnp.sin(ang)
    zero = jnp.zeros_like(sin)
    reps = LANES // HEAD_DIM
    cos_t = jnp.tile(jnp.concatenate([cos, cos], axis=-1), (1, reps))
    sa_t = jnp.tile(jnp.concatenate([-sin, zero], axis=-1), (1, reps))
    sb_t = jnp.tile(jnp.concatenate([zero, sin], axis=-1), (1, reps))
    return cos_t, sa_t, sb_t


def _segment_matrix():
    lane = jnp.arange(GROUP_WIDTH)
    seg = (lane[:, None] // HEAD_DIM == lane[None, :] // HEAD_DIM).astype(F32) / HEAD_DIM
    return seg.astype(BF16)


def kernel(x_prompt, x_sample, cache_k_w128, cache_v_w128, cache_k_w512, cache_v_w512, cache_k_w2048,
           cache_v_w2048, state_pool, c_prompt, c_sample, norm1_g, norm2_g, w_ada, b_ada, w_in, q_norm_g,
           k_norm_g, w_pool_grp, pool_scale, w_pool_br, w_attn_br, w_out, w_up, w_down):
    depth = w_in.shape[0]
    b, s, _ = x_prompt.shape
    n = x_sample.shape[0]
    assert x_sample.shape[1] == 1 and s % min(SUPER, s) == 0 and s % min(PROMPT_TILE, s) == 0

    mod_p, mod_s = _modulation(c_prompt, c_sample, w_ada, b_ada)
    seg = _segment_matrix()
    tab_p = _rope_tables(jnp.arange(s, dtype=F32))
    tab_s = _rope_tables(PAST_LEN + jnp.arange(1, dtype=F32))

    caches = []
    for ck, cv, d in zip((cache_k_w128, cache_k_w512, cache_k_w2048),
                         (cache_v_w128, cache_v_w512, cache_v_w2048), DILATIONS):
        for c in (ck, cv):
            assert c.shape[2] == N_KEYS * d
            caches.append(jnp.transpose(c, (0, 1, 3, 4, 2)))
    state2d = state_pool.reshape(depth, n, POOL_HIST * POOL_WIDTH)

    wpg_b = w_pool_grp.astype(BF16)
    wab_b = w_attn_br.astype(BF16)
    w_in_l = w_in[0:1].astype(BF16)

    n1 = norm1_g.reshape(depth, 1, D_MODEL)
    n2 = norm2_g.reshape(depth, 1, D_MODEL)
    gq = jnp.tile(q_norm_g, (1, ATTN_WIDTH // HEAD_DIM)).reshape(depth, 1, ATTN_WIDTH)
    gk = jnp.tile(k_norm_g, (1, ATTN_WIDTH // HEAD_DIM)).reshape(depth, 1, ATTN_WIDTH)
    ps = pool_scale.reshape(depth, 1, POOL_WIDTH)

    xp = x_prompt
    xs = x_sample.reshape(n, D_MODEL)
    kp = [[] for _ in range(N_GROUPS)]
    vp = [[] for _ in range(N_GROUPS)]
    ks = [[] for _ in range(N_GROUPS)]
    vs = [[] for _ in range(N_GROUPS)]
    pool_p, pool_s = [], []
    for l in range(depth):
        to_cast = [(w_pool_br, l), (w_out, l), (w_up, l), (w_down, l)] + ([(w_in, l + 1)] if l + 1 < depth else [])
        outs = _prompt_project(l, xp, mod_p, n1, w_in_l, gq, gk, seg, *tab_p, to_cast)
        u, a_pool, a_attn = outs[0:3]
        qd, kd, vd = outs[3:6], outs[6:9], outs[9:12]
        kws, vws = outs[12:15], outs[15:18]
        wpb_l, wo_l, wup_l, wdn_l = outs[18:22]
        tail_w = (n2, wpg_b, ps, wpb_l, wab_b, wo_l, wup_l, wdn_l)
        attn_y = _prompt_attention(qd, kd, vd)
        u_s, ap_s, aa_s, k_s, v_s, qt, kt, vt = _sample_project(l, xs, mod_s, n1, w_in_l, gq, gk, seg, *tab_s)
        if l + 1 < depth:
            w_in_l = outs[22]
        xp, yt_s = _prompt_ffn(l, xp, attn_y, u, a_pool, a_attn, mod_p, *tail_w, n, qt, kt, vt, caches)
        for g in range(N_GROUPS):
            kp[g].append(kws[g].reshape(b, -1, HEADS_PER_GROUP, HEAD_DIM))
            vp[g].append(vws[g].reshape(b, -1, HEADS_PER_GROUP, HEAD_DIM))
        pool_p.append(u[:, s - POOL_HIST:, :])

        xs = _sample_ffn(l, xs, yt_s, u_s, state2d, ap_s, aa_s, mod_s, *tail_w)
        kh, vh = (a.reshape(n, N_GROUPS, HEADS_PER_GROUP, HEAD_DIM) for a in (k_s, v_s))
        for g in range(N_GROUPS):
            ks[g].append(kh[:, g:g + 1])
            vs[g].append(vh[:, g:g + 1])
        pool_s.append(jnp.concatenate([state_pool[l][:, 1:], u_s[:, None, :]], axis=1))

    st = lambda rows: jnp.stack(rows, axis=0)
    return (xp, xs.reshape(n, 1, D_MODEL),
            st(kp[0]), st(vp[0]), st(kp[1]), st(vp[1]), st(kp[2]), st(vp[2]), st(pool_p),
            st(ks[0]), st(vs[0]), st(ks[1]), st(vs[1]), st(ks[2]), st(vs[2]), st(pool_s))
```

```python
import functools
import math

import jax
import jax.numpy as jnp
from jax import lax
from jax.experimental import pallas as pl
from jax.experimental.pallas import tpu as pltpu

F32 = jnp.float32
BF16 = jnp.bfloat16

D_MODEL = 1024
DEPTH = 4
PAST_LEN = 8192
POOL_WIDTH = 512
POOL_WINDOWS = (2, 4, 8, 16)
POOL_GROUP = 128
POOL_HIST = 15
HEAD_DIM = 64
HEADS_PER_GROUP = 4
GROUP_WIDTH = HEADS_PER_GROUP * HEAD_DIM
DILATIONS = (1, 4, 16)
WINDOWS = (128, 512, 2048)
N_GROUPS = 3
ATTN_WIDTH = N_GROUPS * GROUP_WIDTH
N_KEYS = 128
BLK = 128
D_FF = 4096
ROPE_THETA = 10000.0
EPS = 1e-6
Q_SCALE = 1.0 / math.sqrt(HEAD_DIM)
LOG2E = math.log2(math.e)
LN2 = math.log(2.0)

COL_U = 0
COL_Q = POOL_WIDTH
COL_K = COL_Q + ATTN_WIDTH
COL_V = COL_K + ATTN_WIDTH
COL_AP = COL_V + ATTN_WIDTH
COL_AA = COL_AP + D_MODEL
IN_WIDTH = COL_AA + D_MODEL

LANES = 128
PROMPT_TILE = 512
SUPER = BLK * DILATIONS[-1]
VMEM_LIMIT = 56 * 1024 * 1024


def _cparams(*sem):
    return pltpu.CompilerParams(dimension_semantics=sem, vmem_limit_bytes=VMEM_LIMIT)


def _const_spec(shape):
    nd = len(shape)
    return pl.BlockSpec(shape, lambda *_: (0,) * nd, pipeline_mode=pl.Buffered(1))


def _layer_spec(arr, layer):
    nd = arr.ndim - 1
    first = layer if arr.shape[0] > 1 else 0
    return pl.BlockSpec((None,) + arr.shape[1:], lambda *_: (first,) + (0,) * nd, pipeline_mode=pl.Buffered(1))


def _dot(a, b):
    return jnp.dot(a, b, preferred_element_type=F32)


def _rms_modulate(x, gain, shift, scale):
    ms = jnp.mean(x * x, axis=-1, keepdims=True)
    return (x * lax.rsqrt(ms + EPS) * gain) * (1.0 + scale) + shift


def _mod_kernel(cp_ref, cs_ref, w_ref, b_ref, op_ref, os_ref):
    w = w_ref[...].astype(BF16)

    def mod(c):
        s = c * (1.0 / (1.0 + jnp.exp(-c)))
        return _dot(s.astype(BF16), w) + b_ref[...]

    mp = mod(cp_ref[...])
    for bi in range(op_ref.shape[0]):
        op_ref[bi] = mp[bi:bi + 1, :]
    os_ref[...] = mod(cs_ref[...])


def _modulation(c_prompt, c_sample, w_ada, b_ada):
    b, n = c_prompt.shape[0], c_sample.shape[0]
    depth = w_ada.shape[0]
    tn = 1536
    return pl.pallas_call(
        _mod_kernel,
        grid=(depth, 6 * D_MODEL // tn),
        in_specs=[
            pl.BlockSpec((b, D_MODEL), lambda l, j: (0, 0)),
            pl.BlockSpec((n, D_MODEL), lambda l, j: (0, 0)),
            pl.BlockSpec((None, D_MODEL, tn), lambda l, j: (l, 0, j)),
            pl.BlockSpec((None, 1, tn), lambda l, j: (l, 0, j)),
        ],
        out_specs=[pl.BlockSpec((None, b, 1, tn), lambda l, j: (l, 0, 0, j)),
                   pl.BlockSpec((None, n, tn), lambda l, j: (l, 0, j))],
        out_shape=[jax.ShapeDtypeStruct((depth, b, 1, 6 * D_MODEL), F32),
                   jax.ShapeDtypeStruct((depth, n, 6 * D_MODEL), F32)],
        compiler_params=_cparams("parallel", "parallel"),
        name="adaln_mod",
    )(c_prompt, c_sample, w_ada, b_ada.reshape(depth, 1, 6 * D_MODEL))


def _head_norm_rope(z, gain_ref, seg_ref, cos_ref, sa_ref, sb_ref, out_scale, store):
    cos = cos_ref[...]
    sa = sa_ref[...]
    sb = sb_ref[...]
    for c in range(ATTN_WIDTH // GROUP_WIDTH):
        zc = z[:, c * GROUP_WIDTH:(c + 1) * GROUP_WIDTH]
        ms = _dot((zc * zc).astype(BF16), seg_ref[...])
        n = zc * lax.rsqrt(ms + EPS) * gain_ref[:, c * GROUP_WIDTH:(c + 1) * GROUP_WIDTH]
        for half in range(GROUP_WIDTH // LANES):
            xx = n[:, half * LANES:(half + 1) * LANES]
            r = xx * cos + pltpu.roll(xx, LANES - HEAD_DIM // 2, 1) * sa + pltpu.roll(xx, HEAD_DIM // 2, 1) * sb
            store(c * (GROUP_WIDTH // LANES) + half, r * out_scale if out_scale != 1.0 else r)


def _project(x_ref, mod_ref, n1_ref, w_ref, gq_ref, gk_ref, seg_ref, cos_ref, sa_ref, sb_ref,
             store_q, store_k, store_v, q_scale, after_qkv=None):
    x = x_ref[...]
    h = _rms_modulate(x, n1_ref[...], mod_ref[:, 0:D_MODEL], mod_ref[:, D_MODEL:2 * D_MODEL])
    hb = h.astype(BF16)
    _head_norm_rope(_dot(hb, w_ref[:, COL_Q:COL_K]), gq_ref, seg_ref, cos_ref, sa_ref, sb_ref, q_scale, store_q)
    _head_norm_rope(_dot(hb, w_ref[:, COL_K:COL_V]), gk_ref, seg_ref, cos_ref, sa_ref, sb_ref, 1.0, store_k)
    v = _dot(hb, w_ref[:, COL_V:COL_AP])
    for c in range(ATTN_WIDTH // LANES):
        store_v(c, v[:, c * LANES:(c + 1) * LANES])
    if after_qkv is not None:
        after_qkv()
    u = _dot(hb, w_ref[:, COL_U:COL_Q])
    a_pool = _dot(hb, w_ref[:, COL_AP:COL_AA])
    a_attn = _dot(hb, w_ref[:, COL_AA:IN_WIDTH])
    return u, a_pool, a_attn


def _chunk_store(ref):
    def store(c, val):
        ref[c] = val
    return store


def _lane_store(ref, rows):
    def store(c, val):
        ref[0:rows, c * LANES:(c + 1) * LANES] = val
    return store


def _prompt_proj_kernel(*refs, n_cast):
    (x_ref, mod_ref, n1_ref, w_ref, gq_ref, gk_ref, seg_ref, cos_ref, sa_ref, sb_ref), refs = refs[:10], refs[10:]
    cast_in, refs = refs[:n_cast], refs[n_cast:]
    (u_ref, ap_ref, aa_ref, q0, q1, q2, k0, k1, k2, v0, v1, v2,
     kw0, kw1, kw2, vw0, vw1, vw2), refs = refs[:18], refs[18:]
    cast_out, (qs_ref, ks_ref, vs_ref) = refs[:n_cast], refs[n_cast:]
    tm = x_ref.shape[0]
    for src, dst in zip(cast_in, cast_out):
        dst[...] = src[...].astype(BF16)

    def write_qkv():
        for src, dsts in ((qs_ref, (q0, q1, q2)), (ks_ref, (k0, k1, k2)), (vs_ref, (v0, v1, v2))):
            for g, dst in enumerate(dsts):
                d = DILATIONS[g]
                for r in range(d):
                    rows = pl.ds(r, tm // d, stride=d) if d > 1 else slice(None)
                    for half in range(GROUP_WIDTH // LANES):
                        c = g * (GROUP_WIDTH // LANES) + half
                        dst[r, :, half * LANES:(half + 1) * LANES] = src[c, rows, :].astype(BF16)
        for src, dsts in ((ks_ref, (kw0, kw1, kw2)), (vs_ref, (vw0, vw1, vw2))):
            for g, dst in enumerate(dsts):
                keep = dst.shape[0]
                for half in range(GROUP_WIDTH // LANES):
                    c = g * (GROUP_WIDTH // LANES) + half
                    dst[:, half * LANES:(half + 1) * LANES] = src[c, tm - keep:, :]

    u, a_pool, a_attn = _project(x_ref, mod_ref, n1_ref, w_ref, gq_ref, gk_ref, seg_ref, cos_ref, sa_ref, sb_ref,
                                 _chunk_store(qs_ref), _chunk_store(ks_ref), _chunk_store(vs_ref),
                                 Q_SCALE * LOG2E,
                                 after_qkv=write_qkv)
    u_ref[...] = u
    ap_ref[...] = a_pool.astype(BF16)
    aa_ref[...] = a_attn.astype(BF16)


def _prompt_project(layer, x, mod_p, n1, w_in, gq, gk, seg, cos, sa, sb, to_cast):
    b, s, _ = x.shape
    tm = min(PROMPT_TILE, s)
    nt = s // tm
    steps = b * nt
    row = lambda width: pl.BlockSpec((None, tm, width), lambda bi, i: (bi, i, 0))
    in_specs = [
        row(D_MODEL),
        pl.BlockSpec((None, None, 1, 6 * D_MODEL), lambda bi, i: (layer, bi, 0, 0)),
        _layer_spec(n1, layer),
        _layer_spec(w_in, layer),
        _layer_spec(gq, layer),
        _layer_spec(gk, layer),
        _const_spec((GROUP_WIDTH, GROUP_WIDTH)),
        pl.BlockSpec((tm, LANES), lambda bi, i: (i, 0)),
        pl.BlockSpec((tm, LANES), lambda bi, i: (i, 0)),
        pl.BlockSpec((tm, LANES), lambda bi, i: (i, 0)),
    ]
    out_shape = [jax.ShapeDtypeStruct((b, s, POOL_WIDTH), F32),
                 jax.ShapeDtypeStruct((b, s, D_MODEL), BF16),
                 jax.ShapeDtypeStruct((b, s, D_MODEL), BF16)]
    out_specs = [row(POOL_WIDTH), row(D_MODEL), row(D_MODEL)]
    for _ in range(3):
        for d in DILATIONS:
            out_shape.append(jax.ShapeDtypeStruct((b, d, s // d, GROUP_WIDTH), BF16))
            out_specs.append(pl.BlockSpec((None, d, tm // d, GROUP_WIDTH), lambda bi, i: (bi, 0, i, 0)))
    for _ in range(2):
        for w in WINDOWS:
            keep = min(w, s)
            out_shape.append(jax.ShapeDtypeStruct((b, keep, GROUP_WIDTH), F32))
            if keep >= tm:
                first = (s - keep) // tm
                out_specs.append(pl.BlockSpec(
                    (None, tm, GROUP_WIDTH), lambda bi, i, first=first: (bi, jnp.maximum(i - first, 0), 0)))
            else:
                out_specs.append(pl.BlockSpec((None, keep, GROUP_WIDTH), lambda bi, i: (bi, 0, 0)))
    for w, wl in to_cast:
        _, k, nn = w.shape
        assert k % (steps * 16) == 0
        in_specs.append(pl.BlockSpec((None, k // steps, nn), lambda bi, i, wl=wl: (wl, bi * nt + i, 0)))
        out_specs.append(pl.BlockSpec((None, k // steps, nn), lambda bi, i: (0, bi * nt + i, 0)))
        out_shape.append(jax.ShapeDtypeStruct((1, k, nn), BF16))
    return pl.pallas_call(
        functools.partial(_prompt_proj_kernel, n_cast=len(to_cast)),
        grid=(b, nt),
        in_specs=in_specs,
        out_specs=out_specs,
        out_shape=out_shape,
        scratch_shapes=[pltpu.VMEM((ATTN_WIDTH // LANES, tm, LANES), F32)] * 3,
        compiler_params=_cparams("parallel", "arbitrary"),
        name="prompt_project",
    )(x, mod_p, n1, w_in, gq, gk, seg, cos, sa, sb, *[w for w, _ in to_cast])


def _sample_proj_kernel(x_ref, mod_ref, n1_ref, w_ref, gq_ref, gk_ref, seg_ref, cos_ref, sa_ref, sb_ref,
                        u_ref, ap_ref, aa_ref, k_ref, v_ref, qt_ref, kt_ref, vt_ref, q_sc, k_sc, v_sc):
    n = x_ref.shape[0]
    for sc in (q_sc, k_sc, v_sc):
        sc[...] = jnp.zeros_like(sc)
    u, a_pool, a_attn = _project(x_ref, mod_ref, n1_ref, w_ref, gq_ref, gk_ref, seg_ref, cos_ref, sa_ref, sb_ref,
                                 _lane_store(q_sc, n), _lane_store(k_sc, n), _lane_store(v_sc, n), Q_SCALE)
    u_ref[...] = u
    ap_ref[...] = a_pool
    aa_ref[...] = a_attn
    k_ref[...] = k_sc[0:n, :]
    v_ref[...] = v_sc[0:n, :]
    for sc, dst in ((q_sc, qt_ref), (k_sc, kt_ref), (v_sc, vt_ref)):
        for c in range(ATTN_WIDTH // LANES):
            dst[c * LANES:(c + 1) * LANES, :] = sc[:, c * LANES:(c + 1) * LANES].T


def _sample_project(layer, x, mod_s, n1, w_in, gq, gk, seg, cos, sa, sb):
    n = x.shape[0]
    assert n <= LANES
    full = lambda shape: pl.BlockSpec(shape, lambda i: (0,) * len(shape))
    in_specs = [full(x.shape), _layer_spec(mod_s, layer), _layer_spec(n1, layer), _layer_spec(w_in, layer),
                _layer_spec(gq, layer), _layer_spec(gk, layer), full(seg.shape), full(cos.shape), full(sa.shape),
                full(sb.shape)]
    shapes = [(n, POOL_WIDTH), (n, D_MODEL), (n, D_MODEL), (n, ATTN_WIDTH), (n, ATTN_WIDTH)] + [(ATTN_WIDTH, LANES)] * 3
    return pl.pallas_call(
        _sample_proj_kernel,
        grid=(1,),
        in_specs=in_specs,
        out_specs=[full(sh) for sh in shapes],
        out_shape=[jax.ShapeDtypeStruct(sh, F32) for sh in shapes],
        scratch_shapes=[pltpu.VMEM((LANES, ATTN_WIDTH), F32)] * 3,
        compiler_params=_cparams("arbitrary"),
        name="sample_project",
    )(x, mod_s, n1, w_in, gq, gk, seg, cos, sa, sb)


def _head_lane_masks():
    lane = lax.broadcasted_iota(jnp.int32, (1, GROUP_WIDTH), 1)
    return [(lane // HEAD_DIM) == h for h in range(HEADS_PER_GROUP)]


def _prompt_attn_kernel(*refs):
    ins, o_ref = refs[:15], refs[15]
    nat_o, nat_m, nat_d, bias_sc = refs[16:20]
    t = pl.program_id(1)
    hm = _head_lane_masks()
    low_head = lax.broadcasted_iota(jnp.int32, (1, LANES), 1) < HEAD_DIM
    stacked = HEADS_PER_GROUP * BLK
    row = lax.broadcasted_iota(jnp.int32, (stacked, 2 * BLK), 0)
    col = lax.broadcasted_iota(jnp.int32, (stacked, 2 * BLK), 1)
    dist = BLK + (row % BLK) - col
    band = (dist >= 0) & (dist <= N_KEYS)
    bias_sc[0] = jnp.where(band, 0.0, -jnp.inf)
    bias_sc[1] = jnp.where(band & (col >= BLK), 0.0, -jnp.inf)

    for g in range(N_GROUPS):
        d = DILATIONS[g]
        q_ref, kc_ref, kp_ref, vc_ref, vp_ref = ins[5 * g:5 * g + 5]
        nb = kc_ref.shape[1] // BLK
        halves = GROUP_WIDTH // LANES

        for r in range(d):
            for j in range(nb):
                q = q_ref[r, j * BLK:(j + 1) * BLK, :]
                if j == 0:
                    keys = jnp.concatenate([kp_ref[r], kc_ref[r, 0:BLK, :]], axis=0)
                    vals = jnp.concatenate([vp_ref[r], vc_ref[r, 0:BLK, :]], axis=0)
                    bias = bias_sc[jnp.where(t > 0, 0, 1)]
                else:
                    keys = kc_ref[r, (j - 1) * BLK:(j + 1) * BLK, :]
                    vals = vc_ref[r, (j - 1) * BLK:(j + 1) * BLK, :]
                    bias = bias_sc[0]
                zero = jnp.zeros_like(q)
                qst = jnp.concatenate([jnp.where(m, q, zero) for m in hm], axis=0)
                s = lax.dot_general(qst, keys, (((1,), (1,)), ((), ())), preferred_element_type=F32) + bias
                m = jnp.max(s, axis=-1, keepdims=True)
                p = jnp.exp2(s - m)
                den = jnp.sum(p, axis=-1, keepdims=True)
                pb = p.astype(BF16)
                first = j * BLK * d + r
                rows = pl.ds(first, BLK, stride=d) if d > 1 else slice(first, first + BLK)
                for half in range(halves):
                    ra = slice(2 * half * BLK, (2 * half + 1) * BLK)
                    rb = slice((2 * half + 1) * BLK, (2 * half + 2) * BLK)
                    o = _dot(pb[2 * half * BLK:(2 * half + 2) * BLK], vals[:, half * LANES:(half + 1) * LANES])
                    nat_o[g * halves + half, rows, :] = jnp.where(low_head, o[0:BLK], o[BLK:2 * BLK])
                    nat_m[g * halves + half, rows, :] = jnp.where(low_head, m[ra], m[rb])
                    nat_d[g * halves + half, rows, :] = jnp.where(low_head, den[ra], den[rb])

    chunk = 256
    halves = GROUP_WIDTH // LANES

    def merge(ci, carry):
        rows = pl.ds(pl.multiple_of(ci * chunk, chunk), chunk)
        for half in range(halves):
            ms = [nat_m[g * halves + half, rows, :] for g in range(N_GROUPS)]
            top = jnp.maximum(jnp.maximum(ms[0], ms[1]), ms[2])
            es = [jnp.exp2(mg - top) for mg in ms]
            acc = sum(es[g] * nat_o[g * halves + half, rows, :] for g in range(N_GROUPS))
            tot = sum(es[g] * nat_d[g * halves + half, rows, :] for g in range(N_GROUPS))
            o_ref[rows, half * LANES:(half + 1) * LANES] = (acc / tot).astype(o_ref.dtype)
        return carry

    lax.fori_loop(0, o_ref.shape[0] // chunk, merge, 0)


def _prompt_attention(qd, kd, vd):
    b = qd[0].shape[0]
    s = qd[0].shape[2]
    sb = min(SUPER, s)
    nsb = s // sb
    ins, in_specs = [], []
    for g, d in enumerate(DILATIONS):
        rpc = sb // d
        nb = rpc // BLK
        cur = pl.BlockSpec((None, d, rpc, GROUP_WIDTH), lambda bi, t: (bi, 0, t, 0))
        prev = pl.BlockSpec((None, d, BLK, GROUP_WIDTH),
                            lambda bi, t, nb=nb: (bi, 0, jnp.maximum(t * nb - 1, 0), 0))
        ins += [qd[g], kd[g], kd[g], vd[g], vd[g]]
        in_specs += [cur, cur, prev, cur, prev]
    per_position = pltpu.VMEM((N_GROUPS * GROUP_WIDTH // LANES, sb, LANES), F32)
    scratch = [per_position, per_position, per_position, pltpu.VMEM((2, HEADS_PER_GROUP * BLK, 2 * BLK), F32)]
    return pl.pallas_call(
        _prompt_attn_kernel,
        grid=(b, nsb),
        in_specs=in_specs,
        out_specs=pl.BlockSpec((None, sb, GROUP_WIDTH), lambda bi, t: (bi, t, 0)),
        out_shape=jax.ShapeDtypeStruct((b, s, GROUP_WIDTH), BF16),
        scratch_shapes=scratch,
        compiler_params=_cparams("parallel", "parallel"),
        name="prompt_attention",
    )(*ins)


def _sigmoid(a):
    return 1.0 / (1.0 + jnp.exp(-a))


def _merge_ffn(x, pool_p, attn_y, a_pool, a_attn, mod_ref, n2_ref, wpg_ref, ps_ref, wpb_ref, wab_ref,
               wo_ref, wup_ref, wdn_ref):
    ys = [_dot(p.astype(BF16), wpg_ref[g]) for g, p in enumerate(pool_p)]
    pool_y = jnp.concatenate(ys, axis=-1) * ps_ref[...]
    pb = _dot(pool_y.astype(BF16), wpb_ref[...])
    ab = _dot(attn_y.astype(BF16), wab_ref[...])
    merged = _sigmoid(a_pool.astype(F32)) * pb + _sigmoid(a_attn.astype(F32)) * ab
    g1 = mod_ref[:, 2 * D_MODEL:3 * D_MODEL]
    x1 = x + g1 * _dot(merged.astype(BF16), wo_ref[...])
    h2 = _rms_modulate(x1, n2_ref[...], mod_ref[:, 3 * D_MODEL:4 * D_MODEL], mod_ref[:, 4 * D_MODEL:5 * D_MODEL])
    h2b = h2.astype(BF16)
    ff_chunk = 1024
    y = jnp.zeros_like(x)
    for c in range(D_FF // ff_chunk):
        hid = jnp.maximum(_dot(h2b, wup_ref[:, c * ff_chunk:(c + 1) * ff_chunk]), 0.0)
        y = y + _dot((hid * hid).astype(BF16), wdn_ref[c * ff_chunk:(c + 1) * ff_chunk, :])
    g2 = mod_ref[:, 5 * D_MODEL:6 * D_MODEL]
    return x1 + g2 * y


def _prompt_ffn_kernel(x_ref, y_ref, u_ref, up_ref, ap_ref, aa_ref, mod_ref, n2_ref, wpg_ref, ps_ref,
                       wpb_ref, wab_ref, wo_ref, wup_ref, wdn_ref,
                       qt_ref, kt_ref, vt_ref, ck0, cv0, ck1, cv1, ck2, cv2,
                       o_ref, yt_ref, ext_ref):
    i = pl.program_id(1)
    step = pl.program_id(0) * pl.num_programs(1) + i
    _sample_attn_step(step, qt_ref, kt_ref, vt_ref, ((ck0, cv0), (ck1, cv1), (ck2, cv2)), yt_ref)
    tm = x_ref.shape[0]
    hist = up_ref.shape[0]
    ext_ref[0:hist, :] = jnp.where(i > 0, up_ref[...], 0.0)
    ext_ref[hist:, :] = u_ref[...]
    pos = (i * tm + lax.broadcasted_iota(jnp.int32, (tm, 1), 0)).astype(F32)
    pool_p = []
    for g, w in enumerate(POOL_WINDOWS):
        e = ext_ref[:, g * POOL_GROUP:(g + 1) * POOL_GROUP]
        acc = e
        span = 1
        while span < w:
            acc = acc + pltpu.roll(acc, span, 0)
            span *= 2
        cnt = jnp.minimum(pos + 1.0, float(w))
        pool_p.append(acc[hist:] / cnt - e[hist:])
    o_ref[...] = _merge_ffn(x_ref[...], pool_p, y_ref[...], ap_ref[...], aa_ref[...], mod_ref, n2_ref, wpg_ref,
                            ps_ref, wpb_ref, wab_ref, wo_ref, wup_ref, wdn_ref)


def _prompt_ffn(layer, x, attn_y, u, a_pool, a_attn, mod_p, n2, wpg, ps, wpb, wab, wo, wup, wdn,
                n, qt, kt, vt, cache_views):
    b, s, _ = x.shape
    tm = min(PROMPT_TILE, s)
    nt = s // tm
    assert n % (b * nt) == 0
    bt = n // (b * nt)
    hist = 16
    row = lambda width: pl.BlockSpec((None, tm, width), lambda bi, i: (bi, i, 0))
    by_head = (N_GROUPS, HEADS_PER_GROUP, HEAD_DIM, LANES)
    new = pl.BlockSpec(by_head, lambda bi, i: (0, 0, 0, 0))
    in_specs = [
        row(D_MODEL), row(GROUP_WIDTH), row(POOL_WIDTH),
        pl.BlockSpec((None, hist, POOL_WIDTH), lambda bi, i: (bi, jnp.maximum(i * (tm // hist) - 1, 0), 0)),
        row(D_MODEL), row(D_MODEL),
        pl.BlockSpec((None, None, 1, 6 * D_MODEL), lambda bi, i: (layer, bi, 0, 0)),
    ] + [_layer_spec(w, layer) for w in (n2, wpg, ps, wpb, wab, wo, wup, wdn)] + [new, new, new]
    for c in cache_views:
        in_specs.append(pl.BlockSpec((None, bt) + c.shape[2:], lambda bi, i: (layer, bi * nt + i, 0, 0, 0)))
    x_out, yt = pl.pallas_call(
        _prompt_ffn_kernel,
        grid=(b, nt),
        in_specs=in_specs,
        out_specs=[row(D_MODEL), pl.BlockSpec(by_head[1:], lambda bi, i: (0, 0, 0))],
        out_shape=[jax.ShapeDtypeStruct((b, s, D_MODEL), F32), jax.ShapeDtypeStruct(by_head[1:], F32)],
        scratch_shapes=[pltpu.VMEM((hist + tm, POOL_WIDTH), F32)],
        compiler_params=_cparams("arbitrary", "arbitrary"),
        name="prompt_merge_ffn",
    )(x, attn_y, u, u, a_pool, a_attn, mod_p, n2, wpg, ps, wpb, wab, wo, wup, wdn,
      qt.reshape(by_head), kt.reshape(by_head), vt.reshape(by_head), *cache_views)
    return x_out, yt.reshape(GROUP_WIDTH, LANES)


def _sample_ffn_kernel(x_ref, yt_ref, u_ref, st_ref, ap_ref, aa_ref, mod_ref, n2_ref, wpg_ref, ps_ref,
                       wpb_ref, wab_ref, wo_ref, wup_ref, wdn_ref, o_ref):
    n = x_ref.shape[0]
    u = u_ref[...]
    attn_y = yt_ref[...].T[0:n, :]
    tail = jnp.zeros_like(u)
    sums = {}
    for back in range(1, POOL_HIST + 1):
        row = POOL_HIST - back
        tail = tail + st_ref[:, row * POOL_WIDTH:(row + 1) * POOL_WIDTH]
        if back + 1 in POOL_WINDOWS:
            sums[back + 1] = tail
    pool_p = []
    for g, w in enumerate(POOL_WINDOWS):
        cols = slice(g * POOL_GROUP, (g + 1) * POOL_GROUP)
        cnt = min(PAST_LEN + 1.0, float(w))
        pool_p.append((sums[w][:, cols] + u[:, cols]) / cnt - u[:, cols])
    o_ref[...] = _merge_ffn(x_ref[...], pool_p, attn_y, ap_ref[...], aa_ref[...], mod_ref, n2_ref, wpg_ref,
                            ps_ref, wpb_ref, wab_ref, wo_ref, wup_ref, wdn_ref)


def _sample_ffn(layer, x, attn_yt, u, state2d, a_pool, a_attn, mod_s, n2, wpg, ps, wpb, wab, wo, wup, wdn):
    n = x.shape[0]
    full = lambda a: pl.BlockSpec(a.shape, lambda i: (0,) * a.ndim)
    ins = (x, attn_yt, u, state2d, a_pool, a_attn, mod_s, n2, wpg, ps, wpb, wab, wo, wup, wdn)
    in_specs = [full(a) for a in ins[:3]] + [_layer_spec(state2d, layer)] + [full(a) for a in ins[4:6]]
    in_specs += [_layer_spec(a, layer) for a in ins[6:]]
    return pl.pallas_call(
        _sample_ffn_kernel,
        grid=(1,),
        in_specs=in_specs,
        out_specs=pl.BlockSpec((n, D_MODEL), lambda i: (0, 0)),
        out_shape=jax.ShapeDtypeStruct((n, D_MODEL), F32),
        compiler_params=_cparams("arbitrary"),
        name="sample_merge_ffn",
    )(*ins)


def _sample_attn_step(step, q_ref, k_ref, v_ref, caches, o_ref):
    bt = caches[0][0].shape[0]
    lane = lax.broadcasted_iota(jnp.int32, (HEAD_DIM, LANES), 1)

    @pl.when(step == 0)
    def _():
        o_ref[...] = jnp.zeros_like(o_ref)

    for b in range(bt):
        mine = lane == step * bt + b
        for h in range(HEADS_PER_GROUP):
            column = lambda ref, g: jnp.sum(jnp.where(mine, ref[g, h], 0.0), axis=1, keepdims=True)
            outs, lses = [], []
            for g, (ck, cv) in enumerate(caches):
                d = DILATIONS[g]
                q = column(q_ref, g)
                s = jnp.sum(ck[b, h] * q, axis=0, keepdims=True)
                if d > 1:
                    row = lax.broadcasted_iota(jnp.int32, s.shape, 1)
                    s = jnp.where(row % d == 0, s, -jnp.inf)
                s_new = jnp.sum(column(k_ref, g) * q, axis=0, keepdims=True)
                m = jnp.maximum(jnp.max(s, axis=1, keepdims=True), s_new)
                p = jnp.exp(s - m)
                p_new = jnp.exp(s_new - m)
                den = jnp.sum(p, axis=1, keepdims=True) + p_new
                acc = jnp.sum(cv[b, h] * p, axis=1, keepdims=True) + p_new * column(v_ref, g)
                outs.append(acc / den)
                lses.append(m + jnp.log(den))
            mx = jnp.maximum(jnp.maximum(lses[0], lses[1]), lses[2])
            es = [jnp.exp(l - mx) for l in lses]
            merged = (es[0] * outs[0] + es[1] * outs[1] + es[2] * outs[2]) / (es[0] + es[1] + es[2])
            o_ref[h] = jnp.where(mine, merged, o_ref[h])


def _rope_tables(pos):
    inv_freq = ROPE_THETA ** (-jnp.arange(0, HEAD_DIM, 2, dtype=F32) / HEAD_DIM)
    ang = pos[:, None] * inv_freq[None, :]
    cos, sin = jnp.cos(ang), jnp.sin(ang)
    zero = jnp.zeros_like(sin)
    reps = LANES // HEAD_DIM
    cos_t = jnp.tile(jnp.concatenate([cos, cos], axis=-1), (1, reps))
    sa_t = jnp.tile(jnp.concatenate([-sin, zero], axis=-1), (1, reps))
    sb_t = jnp.tile(jnp.concatenate([zero, sin], axis=-1), (1, reps))
    return cos_t, sa_t, sb_t


def _segment_matrix():
    lane = jnp.arange(GROUP_WIDTH)
    seg = (lane[:, None] // HEAD_DIM == lane[None, :] // HEAD_DIM).astype(F32) / HEAD_DIM
    return seg.astype(BF16)


def kernel(x_prompt, x_sample, cache_k_w128, cache_v_w128, cache_k_w512, cache_v_w512, cache_k_w2048,
           cache_v_w2048, state_pool, c_prompt, c_sample, norm1_g, norm2_g, w_ada, b_ada, w_in, q_norm_g,
           k_norm_g, w_pool_grp, pool_scale, w_pool_br, w_attn_br, w_out, w_up, w_down):
    depth = w_in.shape[0]
    b, s, _ = x_prompt.shape
    n = x_sample.shape[0]
    assert x_sample.shape[1] == 1 and s % min(SUPER, s) == 0 and s % min(PROMPT_TILE, s) == 0

    mod_p, mod_s = _modulation(c_prompt, c_sample, w_ada, b_ada)
    seg = _segment_matrix()
    tab_p = _rope_tables(jnp.arange(s, dtype=F32))
    tab_s = _rope_tables(PAST_LEN + jnp.arange(1, dtype=F32))

    caches = []
    for ck, cv, d in zip((cache_k_w128, cache_k_w512, cache_k_w2048),
                         (cache_v_w128, cache_v_w512, cache_v_w2048), DILATIONS):
        for c in (ck, cv):
            assert c.shape[2] == N_KEYS * d
            caches.append(jnp.transpose(c, (0, 1, 3, 4, 2)))
    state2d = state_pool.reshape(depth, n, POOL_HIST * POOL_WIDTH)

    wpg_b = w_pool_grp.astype(BF16)
    wab_b = w_attn_br.astype(BF16)
    w_in_l = w_in[0:1].astype(BF16)

    n1 = norm1_g.reshape(depth, 1, D_MODEL)
    n2 = norm2_g.reshape(depth, 1, D_MODEL)
    gq = jnp.tile(q_norm_g, (1, ATTN_WIDTH // HEAD_DIM)).reshape(depth, 1, ATTN_WIDTH)
    gk = jnp.tile(k_norm_g, (1, ATTN_WIDTH // HEAD_DIM)).reshape(depth, 1, ATTN_WIDTH)
    ps = pool_scale.reshape(depth, 1, POOL_WIDTH)

    xp = x_prompt
    xs = x_sample.reshape(n, D_MODEL)
    kp = [[] for _ in range(N_GROUPS)]
    vp = [[] for _ in range(N_GROUPS)]
    ks = [[] for _ in range(N_GROUPS)]
    vs = [[] for _ in range(N_GROUPS)]
    pool_p, pool_s = [], []
    for l in range(depth):
        to_cast = [(w_pool_br, l), (w_out, l), (w_up, l), (w_down, l)] + ([(w_in, l + 1)] if l + 1 < depth else [])
        outs = _prompt_project(l, xp, mod_p, n1, w_in_l, gq, gk, seg, *tab_p, to_cast)
        u, a_pool, a_attn = outs[0:3]
        qd, kd, vd = outs[3:6], outs[6:9], outs[9:12]
        kws, vws = outs[12:15], outs[15:18]
        wpb_l, wo_l, wup_l, wdn_l = outs[18:22]
        tail_w = (n2, wpg_b, ps, wpb_l, wab_b, wo_l, wup_l, wdn_l)
        attn_y = _prompt_attention(qd, kd, vd)
        u_s, ap_s, aa_s, k_s, v_s, qt, kt, vt = _sample_project(l, xs, mod_s, n1, w_in_l, gq, gk, seg, *tab_s)
        if l + 1 < depth:
            w_in_l = outs[22]
        xp, yt_s = _prompt_ffn(l, xp, attn_y, u, a_pool, a_attn, mod_p, *tail_w, n, qt, kt, vt, caches)
        for g in range(N_GROUPS):
            kp[g].append(kws[g].reshape(b, -1, HEADS_PER_GROUP, HEAD_DIM))
            vp[g].append(vws[g].reshape(b, -1, HEADS_PER_GROUP, HEAD_DIM))
        pool_p.append(u[:, s - POOL_HIST:, :])

        xs = _sample_ffn(l, xs, yt_s, u_s, state2d, ap_s, aa_s, mod_s, *tail_w)
        kh, vh = (a.reshape(n, N_GROUPS, HEADS_PER_GROUP, HEAD_DIM) for a in (k_s, v_s))
        for g in range(N_GROUPS):
            ks[g].append(kh[:, g:g + 1])
            vs[g].append(vh[:, g:g + 1])
        pool_s.append(jnp.concatenate([state_pool[l][:, 1:], u_s[:, None, :]], axis=1))

    st = lambda rows: jnp.stack(rows, axis=0)
    return (xp, xs.reshape(n, 1, D_MODEL),
            st(kp[0]), st(vp[0]), st(kp[1]), st(vp[1]), st(kp[2]), st(vp[2]), st(pool_p),
            st(ks[0]), st(vs[0]), st(ks[1]), st(vs[1]), st(ks[2]), st(vs[2]), st(pool_s))
```

```python
import functools
import math

import jax
import jax.numpy as jnp
from jax import lax
from jax.experimental import pallas as pl
from jax.experimental.pallas import tpu as pltpu

F32 = jnp.float32
BF16 = jnp.bfloat16

D_MODEL = 1024
DEPTH = 4
PAST_LEN = 8192
POOL_WIDTH = 512
POOL_WINDOWS = (2, 4, 8, 16)
POOL_GROUP = 128
POOL_HIST = 15
HEAD_DIM = 64
HEADS_PER_GROUP = 4
GROUP_WIDTH = HEADS_PER_GROUP * HEAD_DIM
DILATIONS = (1, 4, 16)
WINDOWS = (128, 512, 2048)
N_GROUPS = 3
ATTN_WIDTH = N_GROUPS * GROUP_WIDTH
N_KEYS = 128
BLK = 128
D_FF = 4096
ROPE_THETA = 10000.0
EPS = 1e-6
Q_SCALE = 1.0 / math.sqrt(HEAD_DIM)
LOG2E = math.log2(math.e)
LN2 = math.log(2.0)

COL_U = 0
COL_Q = POOL_WIDTH
COL_K = COL_Q + ATTN_WIDTH
COL_V = COL_K + ATTN_WIDTH
COL_AP = COL_V + ATTN_WIDTH
COL_AA = COL_AP + D_MODEL
IN_WIDTH = COL_AA + D_MODEL

LANES = 128
PROMPT_TILE = 512
SUPER = BLK * DILATIONS[-1]
VMEM_LIMIT = 56 * 1024 * 1024


def _cparams(*sem):
    return pltpu.CompilerParams(dimension_semantics=sem, vmem_limit_bytes=VMEM_LIMIT)


def _const_spec(shape):
    nd = len(shape)
    return pl.BlockSpec(shape, lambda *_: (0,) * nd, pipeline_mode=pl.Buffered(1))


def _layer_spec(arr, layer):
    nd = arr.ndim - 1
    first = layer if arr.shape[0] > 1 else 0
    return pl.BlockSpec((None,) + arr.shape[1:], lambda *_: (first,) + (0,) * nd, pipeline_mode=pl.Buffered(1))


def _dot(a, b):
    return jnp.dot(a, b, preferred_element_type=F32)


def _rms_modulate(x, gain, shift, scale):
    ms = jnp.mean(x * x, axis=-1, keepdims=True)
    return (x * lax.rsqrt(ms + EPS) * gain) * (1.0 + scale) + shift


def _mod_kernel(cp_ref, cs_ref, w_ref, b_ref, op_ref, os_ref):
    w = w_ref[...].astype(BF16)

    def mod(c):
        s = c * (1.0 / (1.0 + jnp.exp(-c)))
        return _dot(s.astype(BF16), w) + b_ref[...]

    mp = mod(cp_ref[...])
    for bi in range(op_ref.shape[0]):
        op_ref[bi] = mp[bi:bi + 1, :]
    os_ref[...] = mod(cs_ref[...])


def _modulation(c_prompt, c_sample, w_ada, b_ada):
    b, n = c_prompt.shape[0], c_sample.shape[0]
    depth = w_ada.shape[0]
    tn = 1536
    return pl.pallas_call(
        _mod_kernel,
        grid=(depth, 6 * D_MODEL // tn),
        in_specs=[
            pl.BlockSpec((b, D_MODEL), lambda l, j: (0, 0)),
            pl.BlockSpec((n, D_MODEL), lambda l, j: (0, 0)),
            pl.BlockSpec((None, D_MODEL, tn), lambda l, j: (l, 0, j)),
            pl.BlockSpec((None, 1, tn), lambda l, j: (l, 0, j)),
        ],
        out_specs=[pl.BlockSpec((None, b, 1, tn), lambda l, j: (l, 0, 0, j)),
                   pl.BlockSpec((None, n, tn), lambda l, j: (l, 0, j))],
        out_shape=[jax.ShapeDtypeStruct((depth, b, 1, 6 * D_MODEL), F32),
                   jax.ShapeDtypeStruct((depth, n, 6 * D_MODEL), F32)],
        compiler_params=_cparams("parallel", "parallel"),
        name="adaln_mod",
    )(c_prompt, c_sample, w_ada, b_ada.reshape(depth, 1, 6 * D_MODEL))


def _head_norm_rope(z, gain_ref, seg_ref, cos_ref, sa_ref, sb_ref, out_scale, store):
    cos = cos_ref[...]
    sa = sa_ref[...]
    sb = sb_ref[...]
    for c in range(ATTN_WIDTH // GROUP_WIDTH):
        zc = z[:, c * GROUP_WIDTH:(c + 1) * GROUP_WIDTH]
        ms = _dot((zc * zc).astype(BF16), seg_ref[...])
        n = zc * lax.rsqrt(ms + EPS) * gain_ref[:, c * GROUP_WIDTH:(c + 1) * GROUP_WIDTH]
        for half in range(GROUP_WIDTH // LANES):
            xx = n[:, half * LANES:(half + 1) * LANES]
            r = xx * cos + pltpu.roll(xx, LANES - HEAD_DIM // 2, 1) * sa + pltpu.roll(xx, HEAD_DIM // 2, 1) * sb
            store(c * (GROUP_WIDTH // LANES) + half, r * out_scale if out_scale != 1.0 else r)


def _project(x_ref, mod_ref, n1_ref, w_ref, gq_ref, gk_ref, seg_ref, cos_ref, sa_ref, sb_ref,
             store_q, store_k, store_v, q_scale, after_qkv=None):
    x = x_ref[...]
    h = _rms_modulate(x, n1_ref[...], mod_ref[:, 0:D_MODEL], mod_ref[:, D_MODEL:2 * D_MODEL])
    hb = h.astype(BF16)
    _head_norm_rope(_dot(hb, w_ref[:, COL_Q:COL_K]), gq_ref, seg_ref, cos_ref, sa_ref, sb_ref, q_scale, store_q)
    _head_norm_rope(_dot(hb, w_ref[:, COL_K:COL_V]), gk_ref, seg_ref, cos_ref, sa_ref, sb_ref, 1.0, store_k)
    v = _dot(hb, w_ref[:, COL_V:COL_AP])
    for c in range(ATTN_WIDTH // LANES):
        store_v(c, v[:, c * LANES:(c + 1) * LANES])
    if after_qkv is not None:
        after_qkv()
    u = _dot(hb, w_ref[:, COL_U:COL_Q])
    a_pool = _dot(hb, w_ref[:, COL_AP:COL_AA])
    a_attn = _dot(hb, w_ref[:, COL_AA:IN_WIDTH])
    return u, a_pool, a_attn


def _chunk_store(ref):
    def store(c, val):
        ref[c] = val
    return store


def _lane_store(ref, rows):
    def store(c, val):
        ref[0:rows, c * LANES:(c + 1) * LANES] = val
    return store


def _prompt_proj_kernel(*refs, n_cast, n_alias, win_first):
    (x_ref, mod_ref, n1_ref, w_ref, gq_ref, gk_ref, seg_ref, cos_ref, sa_ref, sb_ref), refs = refs[:10], refs[10:]
    cast_in, refs = refs[:n_cast], refs[n_cast + n_alias:]
    (u_ref, ap_ref, aa_ref, q0, q1, q2, k0, k1, k2, v0, v1, v2,
     kw0, kw1, kw2, vw0, vw1, vw2), refs = refs[:18], refs[18:]
    cast_out, (qs_ref, ks_ref, vs_ref) = refs[:n_cast], refs[n_cast:]
    tm = x_ref.shape[0]
    for src, dst in zip(cast_in, cast_out):
        dst[...] = src[...].astype(BF16)

    def write_qkv():
        for src, dsts in ((qs_ref, (q0, q1, q2)), (ks_ref, (k0, k1, k2)), (vs_ref, (v0, v1, v2))):
            for g, dst in enumerate(dsts):
                d = DILATIONS[g]
                for r in range(d):
                    rows = pl.ds(r, tm // d, stride=d) if d > 1 else slice(None)
                    for half in range(GROUP_WIDTH // LANES):
                        c = g * (GROUP_WIDTH // LANES) + half
                        dst[r, :, half * LANES:(half + 1) * LANES] = src[c, rows, :].astype(BF16)

    u, a_pool, a_attn = _project(x_ref, mod_ref, n1_ref, w_ref, gq_ref, gk_ref, seg_ref, cos_ref, sa_ref, sb_ref,
                                 _chunk_store(qs_ref), _chunk_store(ks_ref), _chunk_store(vs_ref),
                                 Q_SCALE * LOG2E,
                                 after_qkv=write_qkv)
    u_ref[...] = u
    ap_ref[...] = a_pool.astype(BF16)
    aa_ref[...] = a_attn.astype(BF16)

    i = pl.program_id(1)
    for src, dsts in ((ks_ref, (kw0, kw1, kw2)), (vs_ref, (vw0, vw1, vw2))):
        for g, dst in enumerate(dsts):
            @pl.when(i >= win_first[g])
            def _(src=src, g=g, dst=dst):
                rows = dst.shape[1]
                for half in range(GROUP_WIDTH // LANES):
                    c = g * (GROUP_WIDTH // LANES) + half
                    dst[half * LANES:(half + 1) * LANES, :] = src[c, tm - rows:, :].T


def _prompt_project(layer, depth, x, mod_p, n1, w_in, gq, gk, seg, cos, sa, sb, to_cast, windows):
    b, s, _ = x.shape
    tm = min(PROMPT_TILE, s)
    nt = s // tm
    steps = b * nt
    row = lambda width: pl.BlockSpec((None, tm, width), lambda bi, i: (bi, i, 0))
    in_specs = [
        row(D_MODEL),
        pl.BlockSpec((None, None, 1, 6 * D_MODEL), lambda bi, i: (layer, bi, 0, 0)),
        _layer_spec(n1, layer),
        _layer_spec(w_in, layer),
        _layer_spec(gq, layer),
        _layer_spec(gk, layer),
        _const_spec((GROUP_WIDTH, GROUP_WIDTH)),
        pl.BlockSpec((tm, LANES), lambda bi, i: (i, 0)),
        pl.BlockSpec((tm, LANES), lambda bi, i: (i, 0)),
        pl.BlockSpec((tm, LANES), lambda bi, i: (i, 0)),
    ]
    out_shape = [jax.ShapeDtypeStruct((b, s, POOL_WIDTH), F32),
                 jax.ShapeDtypeStruct((b, s, D_MODEL), BF16),
                 jax.ShapeDtypeStruct((b, s, D_MODEL), BF16)]
    out_specs = [row(POOL_WIDTH), row(D_MODEL), row(D_MODEL)]
    for _ in range(3):
        for d in DILATIONS:
            out_shape.append(jax.ShapeDtypeStruct((b, d, s // d, GROUP_WIDTH), BF16))
            out_specs.append(pl.BlockSpec((None, d, tm // d, GROUP_WIDTH), lambda bi, i: (bi, 0, i, 0)))
    win_first = []
    for _ in range(2):
        for w in WINDOWS:
            keep = min(w, s)
            out_shape.append(jax.ShapeDtypeStruct((depth, b, GROUP_WIDTH, keep), F32))
            first = (s - keep) // tm if keep >= tm else nt - 1
            win_first.append(first)
            out_specs.append(pl.BlockSpec(
                (None, None, GROUP_WIDTH, min(keep, tm)),
                lambda bi, i, first=first: (layer, bi, 0, jnp.maximum(i - first, 0))))
    for w, wl in to_cast:
        _, k, nn = w.shape
        assert k % (steps * 16) == 0
        in_specs.append(pl.BlockSpec((None, k // steps, nn), lambda bi, i, wl=wl: (wl, bi * nt + i, 0)))
        out_specs.append(pl.BlockSpec((None, k // steps, nn), lambda bi, i: (0, bi * nt + i, 0)))
        out_shape.append(jax.ShapeDtypeStruct((1, k, nn), BF16))
    windows = list(windows or ())
    n_in = len(in_specs)
    in_specs += [pl.BlockSpec(memory_space=pl.ANY)] * len(windows)
    return pl.pallas_call(
        functools.partial(_prompt_proj_kernel, n_cast=len(to_cast), n_alias=len(windows),
                          win_first=tuple(win_first[:N_GROUPS])),
        grid=(b, nt),
        in_specs=in_specs,
        out_specs=out_specs,
        out_shape=out_shape,
        input_output_aliases={n_in + k: 12 + k for k in range(len(windows))},
        scratch_shapes=[pltpu.VMEM((ATTN_WIDTH // LANES, tm, LANES), F32)] * 3,
        compiler_params=_cparams("parallel", "arbitrary"),
        name="prompt_project",
    )(x, mod_p, n1, w_in, gq, gk, seg, cos, sa, sb, *[w for w, _ in to_cast], *windows)


def _sample_proj_kernel(x_ref, mod_ref, n1_ref, w_ref, gq_ref, gk_ref, seg_ref, cos_ref, sa_ref, sb_ref,
                        u_ref, ap_ref, aa_ref, k_ref, v_ref, qt_ref, kt_ref, vt_ref, q_sc, k_sc, v_sc):
    n = x_ref.shape[0]
    for sc in (q_sc, k_sc, v_sc):
        sc[...] = jnp.zeros_like(sc)
    u, a_pool, a_attn = _project(x_ref, mod_ref, n1_ref, w_ref, gq_ref, gk_ref, seg_ref, cos_ref, sa_ref, sb_ref,
                                 _lane_store(q_sc, n), _lane_store(k_sc, n), _lane_store(v_sc, n), Q_SCALE)
    u_ref[...] = u
    ap_ref[...] = a_pool
    aa_ref[...] = a_attn
    k_ref[...] = k_sc[0:n, :]
    v_ref[...] = v_sc[0:n, :]
    for sc, dst in ((q_sc, qt_ref), (k_sc, kt_ref), (v_sc, vt_ref)):
        for c in range(ATTN_WIDTH // LANES):
            dst[c * LANES:(c + 1) * LANES, :] = sc[:, c * LANES:(c + 1) * LANES].T


def _sample_project(layer, x, mod_s, n1, w_in, gq, gk, seg, cos, sa, sb):
    n = x.shape[0]
    assert n <= LANES
    full = lambda shape: pl.BlockSpec(shape, lambda i: (0,) * len(shape))
    in_specs = [full(x.shape), _layer_spec(mod_s, layer), _layer_spec(n1, layer), _layer_spec(w_in, layer),
                _layer_spec(gq, layer), _layer_spec(gk, layer), full(seg.shape), full(cos.shape), full(sa.shape),
                full(sb.shape)]
    shapes = [(n, POOL_WIDTH), (n, D_MODEL), (n, D_MODEL), (n, ATTN_WIDTH), (n, ATTN_WIDTH)] + [(ATTN_WIDTH, LANES)] * 3
    return pl.pallas_call(
        _sample_proj_kernel,
        grid=(1,),
        in_specs=in_specs,
        out_specs=[full(sh) for sh in shapes],
        out_shape=[jax.ShapeDtypeStruct(sh, F32) for sh in shapes],
        scratch_shapes=[pltpu.VMEM((LANES, ATTN_WIDTH), F32)] * 3,
        compiler_params=_cparams("arbitrary"),
        name="sample_project",
    )(x, mod_s, n1, w_in, gq, gk, seg, cos, sa, sb)


def _head_lane_masks():
    lane = lax.broadcasted_iota(jnp.int32, (1, GROUP_WIDTH), 1)
    return [(lane // HEAD_DIM) == h for h in range(HEADS_PER_GROUP)]


def _prompt_attn_kernel(*refs):
    ins, o_ref = refs[:15], refs[15]
    nat_o, nat_m, nat_d, bias_sc = refs[16:20]
    t = pl.program_id(1)
    hm = _head_lane_masks()
    low_head = lax.broadcasted_iota(jnp.int32, (1, LANES), 1) < HEAD_DIM
    stacked = HEADS_PER_GROUP * BLK
    row = lax.broadcasted_iota(jnp.int32, (stacked, 2 * BLK), 0)
    col = lax.broadcasted_iota(jnp.int32, (stacked, 2 * BLK), 1)
    dist = BLK + (row % BLK) - col
    band = (dist >= 0) & (dist <= N_KEYS)
    bias_sc[0] = jnp.where(band, 0.0, -jnp.inf)
    bias_sc[1] = jnp.where(band & (col >= BLK), 0.0, -jnp.inf)

    for g in range(N_GROUPS):
        d = DILATIONS[g]
        q_ref, kc_ref, kp_ref, vc_ref, vp_ref = ins[5 * g:5 * g + 5]
        nb = kc_ref.shape[1] // BLK
        halves = GROUP_WIDTH // LANES

        for r in range(d):
            for j in range(nb):
                q = q_ref[r, j * BLK:(j + 1) * BLK, :]
                if j == 0:
                    keys = jnp.concatenate([kp_ref[r], kc_ref[r, 0:BLK, :]], axis=0)
                    vals = jnp.concatenate([vp_ref[r], vc_ref[r, 0:BLK, :]], axis=0)
                    bias = bias_sc[jnp.where(t > 0, 0, 1)]
                else:
                    keys = kc_ref[r, (j - 1) * BLK:(j + 1) * BLK, :]
                    vals = vc_ref[r, (j - 1) * BLK:(j + 1) * BLK, :]
                    bias = bias_sc[0]
                zero = jnp.zeros_like(q)
                qst = jnp.concatenate([jnp.where(m, q, zero) for m in hm], axis=0)
                s = lax.dot_general(qst, keys, (((1,), (1,)), ((), ())), preferred_element_type=F32) + bias
                m = jnp.max(s, axis=-1, keepdims=True)
                p = jnp.exp2(s - m)
                den = jnp.sum(p, axis=-1, keepdims=True)
                pb = p.astype(BF16)
                first = j * BLK * d + r
                rows = pl.ds(first, BLK, stride=d) if d > 1 else slice(first, first + BLK)
                for half in range(halves):
                    ra = slice(2 * half * BLK, (2 * half + 1) * BLK)
                    rb = slice((2 * half + 1) * BLK, (2 * half + 2) * BLK)
                    o = _dot(pb[2 * half * BLK:(2 * half + 2) * BLK], vals[:, half * LANES:(half + 1) * LANES])
                    nat_o[g * halves + half, rows, :] = jnp.where(low_head, o[0:BLK], o[BLK:2 * BLK])
                    nat_m[g * halves + half, rows, :] = jnp.where(low_head, m[ra], m[rb])
                    nat_d[g * halves + half, rows, :] = jnp.where(low_head, den[ra], den[rb])

    chunk = 256
    halves = GROUP_WIDTH // LANES

    def merge(ci, carry):
        rows = pl.ds(pl.multiple_of(ci * chunk, chunk), chunk)
        for half in range(halves):
            ms = [nat_m[g * halves + half, rows, :] for g in range(N_GROUPS)]
            top = jnp.maximum(jnp.maximum(ms[0], ms[1]), ms[2])
            es = [jnp.exp2(mg - top) for mg in ms]
            acc = sum(es[g] * nat_o[g * halves + half, rows, :] for g in range(N_GROUPS))
            tot = sum(es[g] * nat_d[g * halves + half, rows, :] for g in range(N_GROUPS))
            o_ref[rows, half * LANES:(half + 1) * LANES] = (acc / tot).astype(o_ref.dtype)
        return carry

    lax.fori_loop(0, o_ref.shape[0] // chunk, merge, 0)


def _prompt_attention(qd, kd, vd):
    b = qd[0].shape[0]
    s = qd[0].shape[2]
    sb = min(SUPER, s)
    nsb = s // sb
    ins, in_specs = [], []
    for g, d in enumerate(DILATIONS):
        rpc = sb // d
        nb = rpc // BLK
        cur = pl.BlockSpec((None, d, rpc, GROUP_WIDTH), lambda bi, t: (bi, 0, t, 0))
        prev = pl.BlockSpec((None, d, BLK, GROUP_WIDTH),
                            lambda bi, t, nb=nb: (bi, 0, jnp.maximum(t * nb - 1, 0), 0))
        ins += [qd[g], kd[g], kd[g], vd[g], vd[g]]
        in_specs += [cur, cur, prev, cur, prev]
    per_position = pltpu.VMEM((N_GROUPS * GROUP_WIDTH // LANES, sb, LANES), F32)
    scratch = [per_position, per_position, per_position, pltpu.VMEM((2, HEADS_PER_GROUP * BLK, 2 * BLK), F32)]
    return pl.pallas_call(
        _prompt_attn_kernel,
        grid=(b, nsb),
        in_specs=in_specs,
        out_specs=pl.BlockSpec((None, sb, GROUP_WIDTH), lambda bi, t: (bi, t, 0)),
        out_shape=jax.ShapeDtypeStruct((b, s, GROUP_WIDTH), BF16),
        scratch_shapes=scratch,
        compiler_params=_cparams("parallel", "parallel"),
        name="prompt_attention",
    )(*ins)


def _sigmoid(a):
    return 1.0 / (1.0 + jnp.exp(-a))


def _merge_ffn(x, pool_p, attn_y, a_pool, a_attn, mod_ref, n2_ref, wpg_ref, ps_ref, wpb_ref, wab_ref,
               wo_ref, wup_ref, wdn_ref):
    ys = [_dot(p.astype(BF16), wpg_ref[g]) for g, p in enumerate(pool_p)]
    pool_y = jnp.concatenate(ys, axis=-1) * ps_ref[...]
    pb = _dot(pool_y.astype(BF16), wpb_ref[...])
    ab = _dot(attn_y.astype(BF16), wab_ref[...])
    merged = _sigmoid(a_pool.astype(F32)) * pb + _sigmoid(a_attn.astype(F32)) * ab
    g1 = mod_ref[:, 2 * D_MODEL:3 * D_MODEL]
    x1 = x + g1 * _dot(merged.astype(BF16), wo_ref[...])
    h2 = _rms_modulate(x1, n2_ref[...], mod_ref[:, 3 * D_MODEL:4 * D_MODEL], mod_ref[:, 4 * D_MODEL:5 * D_MODEL])
    h2b = h2.astype(BF16)
    ff_chunk = 1024
    y = jnp.zeros_like(x)
    for c in range(D_FF // ff_chunk):
        hid = jnp.maximum(_dot(h2b, wup_ref[:, c * ff_chunk:(c + 1) * ff_chunk]), 0.0)
        y = y + _dot((hid * hid).astype(BF16), wdn_ref[c * ff_chunk:(c + 1) * ff_chunk, :])
    g2 = mod_ref[:, 5 * D_MODEL:6 * D_MODEL]
    return x1 + g2 * y


def _prompt_ffn_kernel(x_ref, y_ref, u_ref, up_ref, ap_ref, aa_ref, mod_ref, n2_ref, wpg_ref, ps_ref,
                       wpb_ref, wab_ref, wo_ref, wup_ref, wdn_ref,
                       qt_ref, kt_ref, vt_ref, ck0, cv0, ck1, cv1, ck2, cv2,
                       o_ref, yt_ref, ext_ref):
    i = pl.program_id(1)
    step = pl.program_id(0) * pl.num_programs(1) + i
    _sample_attn_step(step, qt_ref, kt_ref, vt_ref, ((ck0, cv0), (ck1, cv1), (ck2, cv2)), yt_ref)
    tm = x_ref.shape[0]
    hist = up_ref.shape[0]
    ext_ref[0:hist, :] = jnp.where(i > 0, up_ref[...], 0.0)
    ext_ref[hist:, :] = u_ref[...]
    pos = (i * tm + lax.broadcasted_iota(jnp.int32, (tm, 1), 0)).astype(F32)
    pool_p = []
    for g, w in enumerate(POOL_WINDOWS):
        e = ext_ref[:, g * POOL_GROUP:(g + 1) * POOL_GROUP]
        acc = e
        span = 1
        while span < w:
            acc = acc + pltpu.roll(acc, span, 0)
            span *= 2
        cnt = jnp.minimum(pos + 1.0, float(w))
        pool_p.append(acc[hist:] / cnt - e[hist:])
    o_ref[...] = _merge_ffn(x_ref[...], pool_p, y_ref[...], ap_ref[...], aa_ref[...], mod_ref, n2_ref, wpg_ref,
                            ps_ref, wpb_ref, wab_ref, wo_ref, wup_ref, wdn_ref)


def _prompt_ffn(layer, x, attn_y, u, a_pool, a_attn, mod_p, n2, wpg, ps, wpb, wab, wo, wup, wdn,
                n, qt, kt, vt, cache_views):
    b, s, _ = x.shape
    tm = min(PROMPT_TILE, s)
    nt = s // tm
    assert n % (b * nt) == 0
    bt = n // (b * nt)
    hist = 16
    row = lambda width: pl.BlockSpec((None, tm, width), lambda bi, i: (bi, i, 0))
    by_head = (N_GROUPS, HEADS_PER_GROUP, HEAD_DIM, LANES)
    new = pl.BlockSpec(by_head, lambda bi, i: (0, 0, 0, 0))
    in_specs = [
        row(D_MODEL), row(GROUP_WIDTH), row(POOL_WIDTH),
        pl.BlockSpec((None, hist, POOL_WIDTH), lambda bi, i: (bi, jnp.maximum(i * (tm // hist) - 1, 0), 0)),
        row(D_MODEL), row(D_MODEL),
        pl.BlockSpec((None, None, 1, 6 * D_MODEL), lambda bi, i: (layer, bi, 0, 0)),
    ] + [_layer_spec(w, layer) for w in (n2, wpg, ps, wpb, wab, wo, wup, wdn)] + [new, new, new]
    for c in cache_views:
        in_specs.append(pl.BlockSpec((None, bt) + c.shape[2:], lambda bi, i: (layer, bi * nt + i, 0, 0, 0)))
    x_out, yt = pl.pallas_call(
        _prompt_ffn_kernel,
        grid=(b, nt),
        in_specs=in_specs,
        out_specs=[row(D_MODEL), pl.BlockSpec(by_head[1:], lambda bi, i: (0, 0, 0))],
        out_shape=[jax.ShapeDtypeStruct((b, s, D_MODEL), F32), jax.ShapeDtypeStruct(by_head[1:], F32)],
        scratch_shapes=[pltpu.VMEM((hist + tm, POOL_WIDTH), F32)],
        compiler_params=_cparams("arbitrary", "arbitrary"),
        name="prompt_merge_ffn",
    )(x, attn_y, u, u, a_pool, a_attn, mod_p, n2, wpg, ps, wpb, wab, wo, wup, wdn,
      qt.reshape(by_head), kt.reshape(by_head), vt.reshape(by_head), *cache_views)
    return x_out, yt.reshape(GROUP_WIDTH, LANES)


def _sample_ffn_kernel(x_ref, yt_ref, u_ref, st_ref, ap_ref, aa_ref, mod_ref, n2_ref, wpg_ref, ps_ref,
                       wpb_ref, wab_ref, wo_ref, wup_ref, wdn_ref, o_ref):
    n = x_ref.shape[0]
    u = u_ref[...]
    attn_y = yt_ref[...].T[0:n, :]
    tail = jnp.zeros_like(u)
    sums = {}
    for back in range(1, POOL_HIST + 1):
        row = POOL_HIST - back
        tail = tail + st_ref[:, row * POOL_WIDTH:(row + 1) * POOL_WIDTH]
        if back + 1 in POOL_WINDOWS:
            sums[back + 1] = tail
    pool_p = []
    for g, w in enumerate(POOL_WINDOWS):
        cols = slice(g * POOL_GROUP, (g + 1) * POOL_GROUP)
        cnt = min(PAST_LEN + 1.0, float(w))
        pool_p.append((sums[w][:, cols] + u[:, cols]) / cnt - u[:, cols])
    o_ref[...] = _merge_ffn(x_ref[...], pool_p, attn_y, ap_ref[...], aa_ref[...], mod_ref, n2_ref, wpg_ref,
                            ps_ref, wpb_ref, wab_ref, wo_ref, wup_ref, wdn_ref)


def _sample_ffn(layer, x, attn_yt, u, state2d, a_pool, a_attn, mod_s, n2, wpg, ps, wpb, wab, wo, wup, wdn):
    n = x.shape[0]
    full = lambda a: pl.BlockSpec(a.shape, lambda i: (0,) * a.ndim)
    ins = (x, attn_yt, u, state2d, a_pool, a_attn, mod_s, n2, wpg, ps, wpb, wab, wo, wup, wdn)
    in_specs = [full(a) for a in ins[:3]] + [_layer_spec(state2d, layer)] + [full(a) for a in ins[4:6]]
    in_specs += [_layer_spec(a, layer) for a in ins[6:]]
    return pl.pallas_call(
        _sample_ffn_kernel,
        grid=(1,),
        in_specs=in_specs,
        out_specs=pl.BlockSpec((n, D_MODEL), lambda i: (0, 0)),
        out_shape=jax.ShapeDtypeStruct((n, D_MODEL), F32),
        compiler_params=_cparams("arbitrary"),
        name="sample_merge_ffn",
    )(*ins)


def _sample_attn_step(step, q_ref, k_ref, v_ref, caches, o_ref):
    bt = caches[0][0].shape[0]
    lane = lax.broadcasted_iota(jnp.int32, (HEAD_DIM, LANES), 1)

    @pl.when(step == 0)
    def _():
        o_ref[...] = jnp.zeros_like(o_ref)

    for b in range(bt):
        mine = lane == step * bt + b
        for h in range(HEADS_PER_GROUP):
            column = lambda ref, g: jnp.sum(jnp.where(mine, ref[g, h], 0.0), axis=1, keepdims=True)
            outs, lses = [], []
            for g, (ck, cv) in enumerate(caches):
                d = DILATIONS[g]
                q = column(q_ref, g)
                s = jnp.sum(ck[b, h] * q, axis=0, keepdims=True)
                if d > 1:
                    row = lax.broadcasted_iota(jnp.int32, s.shape, 1)
                    s = jnp.where(row % d == 0, s, -jnp.inf)
                s_new = jnp.sum(column(k_ref, g) * q, axis=0, keepdims=True)
                m = jnp.maximum(jnp.max(s, axis=1, keepdims=True), s_new)
                p = jnp.exp(s - m)
                p_new = jnp.exp(s_new - m)
                den = jnp.sum(p, axis=1, keepdims=True) + p_new
                acc = jnp.sum(cv[b, h] * p, axis=1, keepdims=True) + p_new * column(v_ref, g)
                outs.append(acc / den)
                lses.append(m + jnp.log(den))
            mx = jnp.maximum(jnp.maximum(lses[0], lses[1]), lses[2])
            es = [jnp.exp(l - mx) for l in lses]
            merged = (es[0] * outs[0] + es[1] * outs[1] + es[2] * outs[2]) / (es[0] + es[1] + es[2])
            o_ref[h] = jnp.where(mine, merged, o_ref[h])


def _rope_tables(pos):
    inv_freq = ROPE_THETA ** (-jnp.arange(0, HEAD_DIM, 2, dtype=F32) / HEAD_DIM)
    ang = pos[:, None] * inv_freq[None, :]
    cos, sin = jnp.cos(ang), jnp.sin(ang)
    zero = jnp.zeros_like(sin)
    reps = LANES // HEAD_DIM
    cos_t = jnp.tile(jnp.concatenate([cos, cos], axis=-1), (1, reps))
    sa_t = jnp.tile(jnp.concatenate([-sin, zero], axis=-1), (1, reps))
    sb_t = jnp.tile(jnp.concatenate([zero, sin], axis=-1), (1, reps))
    return cos_t, sa_t, sb_t


def _segment_matrix():
    lane = jnp.arange(GROUP_WIDTH)
    seg = (lane[:, None] // HEAD_DIM == lane[None, :] // HEAD_DIM).astype(F32) / HEAD_DIM
    return seg.astype(BF16)


def kernel(x_prompt, x_sample, cache_k_w128, cache_v_w128, cache_k_w512, cache_v_w512, cache_k_w2048,
           cache_v_w2048, state_pool, c_prompt, c_sample, norm1_g, norm2_g, w_ada, b_ada, w_in, q_norm_g,
           k_norm_g, w_pool_grp, pool_scale, w_pool_br, w_attn_br, w_out, w_up, w_down):
    depth = w_in.shape[0]
    b, s, _ = x_prompt.shape
    n = x_sample.shape[0]
    assert x_sample.shape[1] == 1 and s % min(SUPER, s) == 0 and s % min(PROMPT_TILE, s) == 0

    mod_p, mod_s = _modulation(c_prompt, c_sample, w_ada, b_ada)
    seg = _segment_matrix()
    tab_p = _rope_tables(jnp.arange(s, dtype=F32))
    tab_s = _rope_tables(PAST_LEN + jnp.arange(1, dtype=F32))

    caches = []
    for ck, cv, d in zip((cache_k_w128, cache_k_w512, cache_k_w2048),
                         (cache_v_w128, cache_v_w512, cache_v_w2048), DILATIONS):
        for c in (ck, cv):
            assert c.shape[2] == N_KEYS * d
            caches.append(jnp.transpose(c, (0, 1, 3, 4, 2)))
    state2d = state_pool.reshape(depth, n, POOL_HIST * POOL_WIDTH)

    wpg_b = w_pool_grp.astype(BF16)
    wab_b = w_attn_br.astype(BF16)
    w_in_l = w_in[0:1].astype(BF16)

    n1 = norm1_g.reshape(depth, 1, D_MODEL)
    n2 = norm2_g.reshape(depth, 1, D_MODEL)
    gq = jnp.tile(q_norm_g, (1, ATTN_WIDTH // HEAD_DIM)).reshape(depth, 1, ATTN_WIDTH)
    gk = jnp.tile(k_norm_g, (1, ATTN_WIDTH // HEAD_DIM)).reshape(depth, 1, ATTN_WIDTH)
    ps = pool_scale.reshape(depth, 1, POOL_WIDTH)

    xp = x_prompt
    xs = x_sample.reshape(n, D_MODEL)
    ks = [[] for _ in range(N_GROUPS)]
    vs = [[] for _ in range(N_GROUPS)]
    pool_p, pool_s = [], []
    windows = None
    for l in range(depth):
        to_cast = [(w_pool_br, l), (w_out, l), (w_up, l), (w_down, l)] + ([(w_in, l + 1)] if l + 1 < depth else [])
        outs = _prompt_project(l, depth, xp, mod_p, n1, w_in_l, gq, gk, seg, *tab_p, to_cast, windows)
        u, a_pool, a_attn = outs[0:3]
        qd, kd, vd = outs[3:6], outs[6:9], outs[9:12]
        windows = outs[12:18]
        wpb_l, wo_l, wup_l, wdn_l = outs[18:22]
        tail_w = (n2, wpg_b, ps, wpb_l, wab_b, wo_l, wup_l, wdn_l)
        attn_y = _prompt_attention(qd, kd, vd)
        u_s, ap_s, aa_s, k_s, v_s, qt, kt, vt = _sample_project(l, xs, mod_s, n1, w_in_l, gq, gk, seg, *tab_s)
        if l + 1 < depth:
            w_in_l = outs[22]
        xp, yt_s = _prompt_ffn(l, xp, attn_y, u, a_pool, a_attn, mod_p, *tail_w, n, qt, kt, vt, caches)
        pool_p.append(u[:, s - POOL_HIST:, :])

        xs = _sample_ffn(l, xs, yt_s, u_s, state2d, ap_s, aa_s, mod_s, *tail_w)
        kh, vh = (a.reshape(n, N_GROUPS, HEADS_PER_GROUP, HEAD_DIM) for a in (k_s, v_s))
        for g in range(N_GROUPS):
            ks[g].append(kh[:, g:g + 1])
            vs[g].append(vh[:, g:g + 1])
        pool_s.append(jnp.concatenate([state_pool[l][:, 1:], u_s[:, None, :]], axis=1))

    st = lambda rows: jnp.stack(rows, axis=0)

    def window(a):
        return jnp.transpose(a.reshape(depth, b, HEADS_PER_GROUP, HEAD_DIM, -1), (0, 1, 4, 2, 3))

    kp, vp = windows[:N_GROUPS], windows[N_GROUPS:]
    return (xp, xs.reshape(n, 1, D_MODEL),
            window(kp[0]), window(vp[0]), window(kp[1]), window(vp[1]), window(kp[2]), window(vp[2]), st(pool_p),
            st(ks[0]), st(vs[0]), st(ks[1]), st(vs[1]), st(ks[2]), st(vs[2]), st(pool_s))
```

```python
import functools
import math

import jax
import jax.numpy as jnp
from jax import lax
from jax.experimental import pallas as pl
from jax.experimental.pallas import tpu as pltpu

F32 = jnp.float32
BF16 = jnp.bfloat16

D_MODEL = 1024
DEPTH = 4
PAST_LEN = 8192
POOL_WIDTH = 512
POOL_WINDOWS = (2, 4, 8, 16)
POOL_GROUP = 128
POOL_HIST = 15
HEAD_DIM = 64
HEADS_PER_GROUP = 4
GROUP_WIDTH = HEADS_PER_GROUP * HEAD_DIM
DILATIONS = (1, 4, 16)
WINDOWS = (128, 512, 2048)
N_GROUPS = 3
ATTN_WIDTH = N_GROUPS * GROUP_WIDTH
N_KEYS = 128
BLK = 128
D_FF = 4096
ROPE_THETA = 10000.0
EPS = 1e-6
Q_SCALE = 1.0 / math.sqrt(HEAD_DIM)
LOG2E = math.log2(math.e)
LN2 = math.log(2.0)

COL_U = 0
COL_Q = POOL_WIDTH
COL_K = COL_Q + ATTN_WIDTH
COL_V = COL_K + ATTN_WIDTH
COL_AP = COL_V + ATTN_WIDTH
COL_AA = COL_AP + D_MODEL
IN_WIDTH = COL_AA + D_MODEL

LANES = 128
PROMPT_TILE = 512
SUPER = BLK * DILATIONS[-1]
VMEM_LIMIT = 56 * 1024 * 1024


def _cparams(*sem):
    return pltpu.CompilerParams(dimension_semantics=sem, vmem_limit_bytes=VMEM_LIMIT)


def _const_spec(shape):
    nd = len(shape)
    return pl.BlockSpec(shape, lambda *_: (0,) * nd, pipeline_mode=pl.Buffered(1))


def _layer_spec(arr, layer):
    nd = arr.ndim - 1
    first = layer if arr.shape[0] > 1 else 0
    return pl.BlockSpec((None,) + arr.shape[1:], lambda *_: (first,) + (0,) * nd, pipeline_mode=pl.Buffered(1))


def _dot(a, b):
    return jnp.dot(a, b, preferred_element_type=F32)


def _rms_modulate(x, gain, shift, scale):
    ms = jnp.mean(x * x, axis=-1, keepdims=True)
    return (x * lax.rsqrt(ms + EPS) * gain) * (1.0 + scale) + shift


def _mod_kernel(cp_ref, cs_ref, w_ref, b_ref, op_ref, os_ref):
    w = w_ref[...].astype(BF16)

    def mod(c):
        s = c * (1.0 / (1.0 + jnp.exp(-c)))
        return _dot(s.astype(BF16), w) + b_ref[...]

    mp = mod(cp_ref[...])
    for bi in range(op_ref.shape[0]):
        op_ref[bi] = mp[bi:bi + 1, :]
    os_ref[...] = mod(cs_ref[...])


def _modulation(c_prompt, c_sample, w_ada, b_ada):
    b, n = c_prompt.shape[0], c_sample.shape[0]
    depth = w_ada.shape[0]
    tn = 1536
    return pl.pallas_call(
        _mod_kernel,
        grid=(depth, 6 * D_MODEL // tn),
        in_specs=[
            pl.BlockSpec((b, D_MODEL), lambda l, j: (0, 0)),
            pl.BlockSpec((n, D_MODEL), lambda l, j: (0, 0)),
            pl.BlockSpec((None, D_MODEL, tn), lambda l, j: (l, 0, j)),
            pl.BlockSpec((None, 1, tn), lambda l, j: (l, 0, j)),
        ],
        out_specs=[pl.BlockSpec((None, b, 1, tn), lambda l, j: (l, 0, 0, j)),
                   pl.BlockSpec((None, n, tn), lambda l, j: (l, 0, j))],
        out_shape=[jax.ShapeDtypeStruct((depth, b, 1, 6 * D_MODEL), F32),
                   jax.ShapeDtypeStruct((depth, n, 6 * D_MODEL), F32)],
        compiler_params=_cparams("parallel", "parallel"),
        name="adaln_mod",
    )(c_prompt, c_sample, w_ada, b_ada.reshape(depth, 1, 6 * D_MODEL))


def _head_norm_rope(z, gain_ref, seg_ref, cos_ref, sa_ref, sb_ref, out_scale, store):
    cos = cos_ref[...]
    sa = sa_ref[...]
    sb = sb_ref[...]
    for c in range(ATTN_WIDTH // GROUP_WIDTH):
        zc = z[:, c * GROUP_WIDTH:(c + 1) * GROUP_WIDTH]
        ms = _dot((zc * zc).astype(BF16), seg_ref[...])
        n = zc * lax.rsqrt(ms + EPS) * gain_ref[:, c * GROUP_WIDTH:(c + 1) * GROUP_WIDTH]
        for half in range(GROUP_WIDTH // LANES):
            xx = n[:, half * LANES:(half + 1) * LANES]
            r = xx * cos + pltpu.roll(xx, LANES - HEAD_DIM // 2, 1) * sa + pltpu.roll(xx, HEAD_DIM // 2, 1) * sb
            store(c * (GROUP_WIDTH // LANES) + half, r * out_scale if out_scale != 1.0 else r)


def _project(x_ref, mod_ref, n1_ref, w_ref, gq_ref, gk_ref, seg_ref, cos_ref, sa_ref, sb_ref,
             store_q, store_k, store_v, q_scale, after_qkv=None):
    x = x_ref[...]
    h = _rms_modulate(x, n1_ref[...], mod_ref[:, 0:D_MODEL], mod_ref[:, D_MODEL:2 * D_MODEL])
    hb = h.astype(BF16)
    _head_norm_rope(_dot(hb, w_ref[:, COL_Q:COL_K]), gq_ref, seg_ref, cos_ref, sa_ref, sb_ref, q_scale, store_q)
    _head_norm_rope(_dot(hb, w_ref[:, COL_K:COL_V]), gk_ref, seg_ref, cos_ref, sa_ref, sb_ref, 1.0, store_k)
    v = _dot(hb, w_ref[:, COL_V:COL_AP])
    for c in range(ATTN_WIDTH // LANES):
        store_v(c, v[:, c * LANES:(c + 1) * LANES])
    if after_qkv is not None:
        after_qkv()
    u = _dot(hb, w_ref[:, COL_U:COL_Q])
    a_pool = _dot(hb, w_ref[:, COL_AP:COL_AA])
    a_attn = _dot(hb, w_ref[:, COL_AA:IN_WIDTH])
    return u, a_pool, a_attn


def _chunk_store(ref):
    def store(c, val):
        ref[c] = val
    return store


def _lane_store(ref, rows):
    def store(c, val):
        ref[0:rows, c * LANES:(c + 1) * LANES] = val
    return store


def _prompt_proj_kernel(*refs, n_cast, n_alias, win_first, win_always):
    (x_ref, mod_ref, n1_ref, w_ref, gq_ref, gk_ref, seg_ref, cos_ref, sa_ref, sb_ref), refs = refs[:10], refs[10:]
    cast_in, refs = refs[:n_cast], refs[n_cast + n_alias:]
    (u_ref, ap_ref, aa_ref, q0, q1, q2, k0, k1, k2, v0, v1, v2,
     kw0, kw1, kw2, vw0, vw1, vw2), refs = refs[:18], refs[18:]
    cast_out, (qs_ref, ks_ref, vs_ref) = refs[:n_cast], refs[n_cast:]
    tm = x_ref.shape[0]
    for src, dst in zip(cast_in, cast_out):
        dst[...] = src[...].astype(BF16)

    def write_qkv():
        for src, dsts in ((qs_ref, (q0, q1, q2)), (ks_ref, (k0, k1, k2)), (vs_ref, (v0, v1, v2))):
            for g, dst in enumerate(dsts):
                d = DILATIONS[g]
                for r in range(d):
                    rows = pl.ds(r, tm // d, stride=d) if d > 1 else slice(None)
                    for half in range(GROUP_WIDTH // LANES):
                        c = g * (GROUP_WIDTH // LANES) + half
                        dst[r, :, half * LANES:(half + 1) * LANES] = src[c, rows, :].astype(BF16)
        for g in range(N_GROUPS):
            if win_always[g]:
                write_window(g)

    def write_window(g):
        for src, dst in ((ks_ref, (kw0, kw1, kw2)[g]), (vs_ref, (vw0, vw1, vw2)[g])):
            rows = dst.shape[1]
            for half in range(GROUP_WIDTH // LANES):
                c = g * (GROUP_WIDTH // LANES) + half
                dst[half * LANES:(half + 1) * LANES, :] = src[c, tm - rows:, :].T

    u, a_pool, a_attn = _project(x_ref, mod_ref, n1_ref, w_ref, gq_ref, gk_ref, seg_ref, cos_ref, sa_ref, sb_ref,
                                 _chunk_store(qs_ref), _chunk_store(ks_ref), _chunk_store(vs_ref),
                                 Q_SCALE * LOG2E,
                                 after_qkv=write_qkv)
    u_ref[...] = u
    ap_ref[...] = a_pool.astype(BF16)
    aa_ref[...] = a_attn.astype(BF16)

    i = pl.program_id(1)
    for g in range(N_GROUPS):
        if not win_always[g]:
            pl.when(i >= win_first[g])(functools.partial(write_window, g))


def _prompt_project(layer, depth, x, mod_p, n1, w_in, gq, gk, seg, cos, sa, sb, to_cast, windows):
    b, s, _ = x.shape
    tm = min(PROMPT_TILE, s)
    nt = s // tm
    steps = b * nt
    row = lambda width: pl.BlockSpec((None, tm, width), lambda bi, i: (bi, i, 0))
    in_specs = [
        row(D_MODEL),
        pl.BlockSpec((None, None, 1, 6 * D_MODEL), lambda bi, i: (layer, bi, 0, 0)),
        _layer_spec(n1, layer),
        _layer_spec(w_in, layer),
        _layer_spec(gq, layer),
        _layer_spec(gk, layer),
        _const_spec((GROUP_WIDTH, GROUP_WIDTH)),
        pl.BlockSpec((tm, LANES), lambda bi, i: (i, 0)),
        pl.BlockSpec((tm, LANES), lambda bi, i: (i, 0)),
        pl.BlockSpec((tm, LANES), lambda bi, i: (i, 0)),
    ]
    out_shape = [jax.ShapeDtypeStruct((b, s, POOL_WIDTH), F32),
                 jax.ShapeDtypeStruct((b, s, D_MODEL), BF16),
                 jax.ShapeDtypeStruct((b, s, D_MODEL), BF16)]
    out_specs = [row(POOL_WIDTH), row(D_MODEL), row(D_MODEL)]
    for _ in range(3):
        for d in DILATIONS:
            out_shape.append(jax.ShapeDtypeStruct((b, d, s // d, GROUP_WIDTH), BF16))
            out_specs.append(pl.BlockSpec((None, d, tm // d, GROUP_WIDTH), lambda bi, i: (bi, 0, i, 0)))
    win_first = []
    for _ in range(2):
        for w in WINDOWS:
            keep = min(w, s)
            out_shape.append(jax.ShapeDtypeStruct((depth, b, GROUP_WIDTH, keep), F32))
            first = (s - keep) // tm if keep >= tm else nt - 1
            win_first.append(first)
            out_specs.append(pl.BlockSpec(
                (None, None, GROUP_WIDTH, min(keep, tm)),
                lambda bi, i, first=first: (layer, bi, 0, jnp.maximum(i - first, 0))))
    for w, wl in to_cast:
        _, k, nn = w.shape
        assert k % (steps * 16) == 0
        in_specs.append(pl.BlockSpec((None, k // steps, nn), lambda bi, i, wl=wl: (wl, bi * nt + i, 0)))
        out_specs.append(pl.BlockSpec((None, k // steps, nn), lambda bi, i: (0, bi * nt + i, 0)))
        out_shape.append(jax.ShapeDtypeStruct((1, k, nn), BF16))
    windows = list(windows or ())
    n_in = len(in_specs)
    in_specs += [pl.BlockSpec(memory_space=pl.ANY)] * len(windows)
    return pl.pallas_call(
        functools.partial(_prompt_proj_kernel, n_cast=len(to_cast), n_alias=len(windows),
                          win_first=tuple(win_first[:N_GROUPS]),
                          win_always=tuple(2 * f <= nt for f in win_first[:N_GROUPS])),
        grid=(b, nt),
        in_specs=in_specs,
        out_specs=out_specs,
        out_shape=out_shape,
        input_output_aliases={n_in + k: 12 + k for k in range(len(windows))},
        scratch_shapes=[pltpu.VMEM((ATTN_WIDTH // LANES, tm, LANES), F32)] * 3,
        compiler_params=_cparams("parallel", "arbitrary"),
        name="prompt_project",
    )(x, mod_p, n1, w_in, gq, gk, seg, cos, sa, sb, *[w for w, _ in to_cast], *windows)


def _sample_proj_kernel(x_ref, mod_ref, n1_ref, w_ref, gq_ref, gk_ref, seg_ref, cos_ref, sa_ref, sb_ref,
                        u_ref, ap_ref, aa_ref, k_ref, v_ref, qt_ref, kt_ref, vt_ref, q_sc, k_sc, v_sc):
    n = x_ref.shape[0]
    for sc in (q_sc, k_sc, v_sc):
        sc[...] = jnp.zeros_like(sc)
    u, a_pool, a_attn = _project(x_ref, mod_ref, n1_ref, w_ref, gq_ref, gk_ref, seg_ref, cos_ref, sa_ref, sb_ref,
                                 _lane_store(q_sc, n), _lane_store(k_sc, n), _lane_store(v_sc, n), Q_SCALE)
    u_ref[...] = u
    ap_ref[...] = a_pool
    aa_ref[...] = a_attn
    k_ref[...] = k_sc[0:n, :]
    v_ref[...] = v_sc[0:n, :]
    for sc, dst in ((q_sc, qt_ref), (k_sc, kt_ref), (v_sc, vt_ref)):
        for c in range(ATTN_WIDTH // LANES):
            dst[c * LANES:(c + 1) * LANES, :] = sc[:, c * LANES:(c + 1) * LANES].T


def _sample_project(layer, x, mod_s, n1, w_in, gq, gk, seg, cos, sa, sb):
    n = x.shape[0]
    assert n <= LANES
    full = lambda shape: pl.BlockSpec(shape, lambda i: (0,) * len(shape))
    in_specs = [full(x.shape), _layer_spec(mod_s, layer), _layer_spec(n1, layer), _layer_spec(w_in, layer),
                _layer_spec(gq, layer), _layer_spec(gk, layer), full(seg.shape), full(cos.shape), full(sa.shape),
                full(sb.shape)]
    shapes = [(n, POOL_WIDTH), (n, D_MODEL), (n, D_MODEL), (n, ATTN_WIDTH), (n, ATTN_WIDTH)] + [(ATTN_WIDTH, LANES)] * 3
    return pl.pallas_call(
        _sample_proj_kernel,
        grid=(1,),
        in_specs=in_specs,
        out_specs=[full(sh) for sh in shapes],
        out_shape=[jax.ShapeDtypeStruct(sh, F32) for sh in shapes],
        scratch_shapes=[pltpu.VMEM((LANES, ATTN_WIDTH), F32)] * 3,
        compiler_params=_cparams("arbitrary"),
        name="sample_project",
    )(x, mod_s, n1, w_in, gq, gk, seg, cos, sa, sb)


def _head_lane_masks():
    lane = lax.broadcasted_iota(jnp.int32, (1, GROUP_WIDTH), 1)
    return [(lane // HEAD_DIM) == h for h in range(HEADS_PER_GROUP)]


def _prompt_attn_kernel(*refs):
    ins, o_ref = refs[:15], refs[15]
    nat_o, nat_m, nat_d, bias_sc = refs[16:20]
    t = pl.program_id(1)
    hm = _head_lane_masks()
    low_head = lax.broadcasted_iota(jnp.int32, (1, LANES), 1) < HEAD_DIM
    stacked = HEADS_PER_GROUP * BLK
    row = lax.broadcasted_iota(jnp.int32, (stacked, 2 * BLK), 0)
    col = lax.broadcasted_iota(jnp.int32, (stacked, 2 * BLK), 1)
    dist = BLK + (row % BLK) - col
    band = (dist >= 0) & (dist <= N_KEYS)
    bias_sc[0] = jnp.where(band, 0.0, -jnp.inf)
    bias_sc[1] = jnp.where(band & (col >= BLK), 0.0, -jnp.inf)

    for g in range(N_GROUPS):
        d = DILATIONS[g]
        q_ref, kc_ref, kp_ref, vc_ref, vp_ref = ins[5 * g:5 * g + 5]
        nb = kc_ref.shape[1] // BLK
        halves = GROUP_WIDTH // LANES

        for r in range(d):
            for j in range(nb):
                q = q_ref[r, j * BLK:(j + 1) * BLK, :]
                if j == 0:
                    keys = jnp.concatenate([kp_ref[r], kc_ref[r, 0:BLK, :]], axis=0)
                    vals = jnp.concatenate([vp_ref[r], vc_ref[r, 0:BLK, :]], axis=0)
                    bias = bias_sc[jnp.where(t > 0, 0, 1)]
                else:
                    keys = kc_ref[r, (j - 1) * BLK:(j + 1) * BLK, :]
                    vals = vc_ref[r, (j - 1) * BLK:(j + 1) * BLK, :]
                    bias = bias_sc[0]
                zero = jnp.zeros_like(q)
                qst = jnp.concatenate([jnp.where(m, q, zero) for m in hm], axis=0)
                s = lax.dot_general(qst, keys, (((1,), (1,)), ((), ())), preferred_element_type=F32) + bias
                m = jnp.max(s, axis=-1, keepdims=True)
                p = jnp.exp2(s - m)
                den = jnp.sum(p, axis=-1, keepdims=True)
                pb = p.astype(BF16)
                first = j * BLK * d + r
                rows = pl.ds(first, BLK, stride=d) if d > 1 else slice(first, first + BLK)
                for half in range(halves):
                    ra = slice(2 * half * BLK, (2 * half + 1) * BLK)
                    rb = slice((2 * half + 1) * BLK, (2 * half + 2) * BLK)
                    o = _dot(pb[2 * half * BLK:(2 * half + 2) * BLK], vals[:, half * LANES:(half + 1) * LANES])
                    nat_o[g * halves + half, rows, :] = jnp.where(low_head, o[0:BLK], o[BLK:2 * BLK])
                    nat_m[g * halves + half, rows, :] = jnp.where(low_head, m[ra], m[rb])
                    nat_d[g * halves + half, rows, :] = jnp.where(low_head, den[ra], den[rb])

    chunk = 256
    halves = GROUP_WIDTH // LANES

    def merge(ci, carry):
        rows = pl.ds(pl.multiple_of(ci * chunk, chunk), chunk)
        for half in range(halves):
            ms = [nat_m[g * halves + half, rows, :] for g in range(N_GROUPS)]
            top = jnp.maximum(jnp.maximum(ms[0], ms[1]), ms[2])
            es = [jnp.exp2(mg - top) for mg in ms]
            acc = sum(es[g] * nat_o[g * halves + half, rows, :] for g in range(N_GROUPS))
            tot = sum(es[g] * nat_d[g * halves + half, rows, :] for g in range(N_GROUPS))
            o_ref[rows, half * LANES:(half + 1) * LANES] = (acc / tot).astype(o_ref.dtype)
        return carry

    lax.fori_loop(0, o_ref.shape[0] // chunk, merge, 0)


def _prompt_attention(qd, kd, vd):
    b = qd[0].shape[0]
    s = qd[0].shape[2]
    sb = min(SUPER, s)
    nsb = s // sb
    ins, in_specs = [], []
    for g, d in enumerate(DILATIONS):
        rpc = sb // d
        nb = rpc // BLK
        cur = pl.BlockSpec((None, d, rpc, GROUP_WIDTH), lambda bi, t: (bi, 0, t, 0))
        prev = pl.BlockSpec((None, d, BLK, GROUP_WIDTH),
                            lambda bi, t, nb=nb: (bi, 0, jnp.maximum(t * nb - 1, 0), 0))
        ins += [qd[g], kd[g], kd[g], vd[g], vd[g]]
        in_specs += [cur, cur, prev, cur, prev]
    per_position = pltpu.VMEM((N_GROUPS * GROUP_WIDTH // LANES, sb, LANES), F32)
    scratch = [per_position, per_position, per_position, pltpu.VMEM((2, HEADS_PER_GROUP * BLK, 2 * BLK), F32)]
    return pl.pallas_call(
        _prompt_attn_kernel,
        grid=(b, nsb),
        in_specs=in_specs,
        out_specs=pl.BlockSpec((None, sb, GROUP_WIDTH), lambda bi, t: (bi, t, 0)),
        out_shape=jax.ShapeDtypeStruct((b, s, GROUP_WIDTH), BF16),
        scratch_shapes=scratch,
        compiler_params=_cparams("parallel", "parallel"),
        name="prompt_attention",
    )(*ins)


def _sigmoid(a):
    return 1.0 / (1.0 + jnp.exp(-a))


FF_CHUNK = 1024


def _merge(x, pool_p, attn_y, a_pool, a_attn, mod_ref, n2_ref, wpg_ref, ps_ref, wpb_ref, wab_ref, wo_ref):
    ys = [_dot(p.astype(BF16), wpg_ref[g]) for g, p in enumerate(pool_p)]
    pool_y = jnp.concatenate(ys, axis=-1) * ps_ref[...]
    pb = _dot(pool_y.astype(BF16), wpb_ref[...])
    ab = _dot(attn_y.astype(BF16), wab_ref[...])
    merged = _sigmoid(a_pool.astype(F32)) * pb + _sigmoid(a_attn.astype(F32)) * ab
    g1 = mod_ref[:, 2 * D_MODEL:3 * D_MODEL]
    x1 = x + g1 * _dot(merged.astype(BF16), wo_ref[...])
    h2 = _rms_modulate(x1, n2_ref[...], mod_ref[:, 3 * D_MODEL:4 * D_MODEL], mod_ref[:, 4 * D_MODEL:5 * D_MODEL])
    return x1, h2.astype(BF16)


def _ffn(h2b, wup_ref, wdn_ref):
    y = jnp.zeros((h2b.shape[0], D_MODEL), F32)
    for c in range(wup_ref.shape[1] // FF_CHUNK):
        hid = jnp.maximum(_dot(h2b, wup_ref[:, c * FF_CHUNK:(c + 1) * FF_CHUNK]), 0.0)
        y = y + _dot((hid * hid).astype(BF16), wdn_ref[c * FF_CHUNK:(c + 1) * FF_CHUNK, :])
    return y


def _prompt_ffn_kernel(x_ref, y_ref, u_ref, up_ref, ap_ref, aa_ref, mod_ref, n2_ref, wpg_ref, ps_ref,
                       wpb_ref, wab_ref, wo_ref, wup_ref, wdn_ref,
                       qt_ref, kt_ref, vt_ref, ck0, cv0, ck1, cv1, ck2, cv2,
                       o_ref, yt_ref, ext_ref):
    i = pl.program_id(1)
    step = pl.program_id(0) * pl.num_programs(1) + i
    _sample_attn_step(step, qt_ref, kt_ref, vt_ref, ((ck0, cv0), (ck1, cv1), (ck2, cv2)), yt_ref)
    tm = x_ref.shape[0]
    hist = up_ref.shape[0]
    ext_ref[0:hist, :] = jnp.where(i > 0, up_ref[...], 0.0)
    ext_ref[hist:, :] = u_ref[...]
    pos = (i * tm + lax.broadcasted_iota(jnp.int32, (tm, 1), 0)).astype(F32)
    pool_p = []
    for g, w in enumerate(POOL_WINDOWS):
        e = ext_ref[:, g * POOL_GROUP:(g + 1) * POOL_GROUP]
        acc = e
        span = 1
        while span < w:
            acc = acc + pltpu.roll(acc, span, 0)
            span *= 2
        cnt = jnp.minimum(pos + 1.0, float(w))
        pool_p.append(acc[hist:] / cnt - e[hist:])
    x1, h2b = _merge(x_ref[...], pool_p, y_ref[...], ap_ref[...], aa_ref[...], mod_ref, n2_ref, wpg_ref,
                     ps_ref, wpb_ref, wab_ref, wo_ref)
    o_ref[...] = x1 + mod_ref[:, 5 * D_MODEL:6 * D_MODEL] * _ffn(h2b, wup_ref, wdn_ref)


def _prompt_ffn(layer, x, attn_y, u, a_pool, a_attn, mod_p, n2, wpg, ps, wpb, wab, wo, wup, wdn,
                n, qt, kt, vt, cache_views):
    b, s, _ = x.shape
    tm = min(PROMPT_TILE, s)
    nt = s // tm
    assert n % (b * nt) == 0
    bt = n // (b * nt)
    hist = 16
    row = lambda width: pl.BlockSpec((None, tm, width), lambda bi, i: (bi, i, 0))
    by_head = (N_GROUPS, HEADS_PER_GROUP, HEAD_DIM, LANES)
    new = pl.BlockSpec(by_head, lambda bi, i: (0, 0, 0, 0))
    in_specs = [
        row(D_MODEL), row(GROUP_WIDTH), row(POOL_WIDTH),
        pl.BlockSpec((None, hist, POOL_WIDTH), lambda bi, i: (bi, jnp.maximum(i * (tm // hist) - 1, 0), 0)),
        row(D_MODEL), row(D_MODEL),
        pl.BlockSpec((None, None, 1, 6 * D_MODEL), lambda bi, i: (layer, bi, 0, 0)),
    ] + [_layer_spec(w, layer) for w in (n2, wpg, ps, wpb, wab, wo, wup, wdn)] + [new, new, new]
    for c in cache_views:
        in_specs.append(pl.BlockSpec((None, bt) + c.shape[2:], lambda bi, i: (layer, bi * nt + i, 0, 0, 0)))
    x_out, yt = pl.pallas_call(
        _prompt_ffn_kernel,
        grid=(b, nt),
        in_specs=in_specs,
        out_specs=[row(D_MODEL), pl.BlockSpec(by_head[1:], lambda bi, i: (0, 0, 0))],
        out_shape=[jax.ShapeDtypeStruct((b, s, D_MODEL), F32), jax.ShapeDtypeStruct(by_head[1:], F32)],
        scratch_shapes=[pltpu.VMEM((hist + tm, POOL_WIDTH), F32)],
        compiler_params=_cparams("arbitrary", "arbitrary"),
        name="prompt_merge_ffn",
    )(x, attn_y, u, u, a_pool, a_attn, mod_p, n2, wpg, ps, wpb, wab, wo, wup, wdn,
      qt.reshape(by_head), kt.reshape(by_head), vt.reshape(by_head), *cache_views)
    return x_out, yt.reshape(GROUP_WIDTH, LANES)


def _sample_ffn_kernel(x_ref, yt_ref, u_ref, st_ref, ap_ref, aa_ref, mod_ref, n2_ref, wpg_ref, ps_ref,
                       wpb_ref, wab_ref, wo_ref, wup_ref, wdn_ref, o_ref, x1_sc, h2b_sc, y_sc):
    c = pl.program_id(0)
    n = x_ref.shape[0]

    @pl.when(c == 0)
    def _():
        u = u_ref[...]
        attn_y = yt_ref[...].T[0:n, :]
        tail = jnp.zeros_like(u)
        sums = {}
        for back in range(1, POOL_HIST + 1):
            row = POOL_HIST - back
            tail = tail + st_ref[:, row * POOL_WIDTH:(row + 1) * POOL_WIDTH]
            if back + 1 in POOL_WINDOWS:
                sums[back + 1] = tail
        pool_p = []
        for g, w in enumerate(POOL_WINDOWS):
            cols = slice(g * POOL_GROUP, (g + 1) * POOL_GROUP)
            cnt = min(PAST_LEN + 1.0, float(w))
            pool_p.append((sums[w][:, cols] + u[:, cols]) / cnt - u[:, cols])
        x1, h2b = _merge(x_ref[...], pool_p, attn_y, ap_ref[...], aa_ref[...], mod_ref, n2_ref, wpg_ref,
                         ps_ref, wpb_ref, wab_ref, wo_ref)
        x1_sc[...] = x1
        h2b_sc[...] = h2b
        y_sc[...] = jnp.zeros_like(y_sc)

    y_sc[...] += _ffn(h2b_sc[...], wup_ref, wdn_ref)

    @pl.when(c == pl.num_programs(0) - 1)
    def _():
        o_ref[...] = x1_sc[...] + mod_ref[:, 5 * D_MODEL:6 * D_MODEL] * y_sc[...]


def _sample_ffn(layer, x, attn_yt, u, state2d, a_pool, a_attn, mod_s, n2, wpg, ps, wpb, wab, wo, wup, wdn):
    n = x.shape[0]
    full = lambda a: pl.BlockSpec(a.shape, lambda c: (0,) * a.ndim)
    front = (mod_s, n2, wpg, ps, wpb, wab, wo)
    in_specs = [full(a) for a in (x, attn_yt, u)] + [_layer_spec(state2d, layer)] + [full(a) for a in (a_pool, a_attn)]
    in_specs += [_layer_spec(a, layer) for a in front]
    up_layer = layer if wup.shape[0] > 1 else 0
    dn_layer = layer if wdn.shape[0] > 1 else 0
    in_specs += [pl.BlockSpec((None, D_MODEL, FF_CHUNK), lambda c: (up_layer, 0, c)),
                 pl.BlockSpec((None, FF_CHUNK, D_MODEL), lambda c: (dn_layer, c, 0))]
    return pl.pallas_call(
        _sample_ffn_kernel,
        grid=(D_FF // FF_CHUNK,),
        in_specs=in_specs,
        out_specs=pl.BlockSpec((n, D_MODEL), lambda c: (0, 0)),
        out_shape=jax.ShapeDtypeStruct((n, D_MODEL), F32),
        scratch_shapes=[pltpu.VMEM((n, D_MODEL), F32), pltpu.VMEM((n, D_MODEL), BF16), pltpu.VMEM((n, D_MODEL), F32)],
        compiler_params=_cparams("arbitrary"),
        name="sample_merge_ffn",
    )(x, attn_yt, u, state2d, a_pool, a_attn, *front, wup, wdn)


def _sample_attn_step(step, q_ref, k_ref, v_ref, caches, o_ref):
    bt = caches[0][0].shape[0]
    lane = lax.broadcasted_iota(jnp.int32, (HEAD_DIM, LANES), 1)

    @pl.when(step == 0)
    def _():
        o_ref[...] = jnp.zeros_like(o_ref)

    for b in range(bt):
        mine = lane == step * bt + b
        for h in range(HEADS_PER_GROUP):
            column = lambda ref, g: jnp.sum(jnp.where(mine, ref[g, h], 0.0), axis=1, keepdims=True)
            outs, lses = [], []
            for g, (ck, cv) in enumerate(caches):
                d = DILATIONS[g]
                q = column(q_ref, g)
                s = jnp.sum(ck[b, h] * q, axis=0, keepdims=True)
                if d > 1:
                    row = lax.broadcasted_iota(jnp.int32, s.shape, 1)
                    s = jnp.where(row % d == 0, s, -jnp.inf)
                s_new = jnp.sum(column(k_ref, g) * q, axis=0, keepdims=True)
                m = jnp.maximum(jnp.max(s, axis=1, keepdims=True), s_new)
                p = jnp.exp(s - m)
                p_new = jnp.exp(s_new - m)
                den = jnp.sum(p, axis=1, keepdims=True) + p_new
                acc = jnp.sum(cv[b, h] * p, axis=1, keepdims=True) + p_new * column(v_ref, g)
                outs.append(acc / den)
                lses.append(m + jnp.log(den))
            mx = jnp.maximum(jnp.maximum(lses[0], lses[1]), lses[2])
            es = [jnp.exp(l - mx) for l in lses]
            merged = (es[0] * outs[0] + es[1] * outs[1] + es[2] * outs[2]) / (es[0] + es[1] + es[2])
            o_ref[h] = jnp.where(mine, merged, o_ref[h])


def _rope_tables(pos):
    inv_freq = ROPE_THETA ** (-jnp.arange(0, HEAD_DIM, 2, dtype=F32) / HEAD_DIM)
    ang = pos[:, None] * inv_freq[None, :]
    cos, sin = jnp.cos(ang), jnp.sin(ang)
    zero = jnp.zeros_like(sin)
    reps = LANES // HEAD_DIM
    cos_t = jnp.tile(jnp.concatenate([cos, cos], axis=-1), (1, reps))
    sa_t = jnp.tile(jnp.concatenate([-sin, zero], axis=-1), (1, reps))
    sb_t = jnp.tile(jnp.concatenate([zero, sin], axis=-1), (1, reps))
    return cos_t, sa_t, sb_t


def _segment_matrix():
    lane = jnp.arange(GROUP_WIDTH)
    seg = (lane[:, None] // HEAD_DIM == lane[None, :] // HEAD_DIM).astype(F32) / HEAD_DIM
    return seg.astype(BF16)


def kernel(x_prompt, x_sample, cache_k_w128, cache_v_w128, cache_k_w512, cache_v_w512, cache_k_w2048,
           cache_v_w2048, state_pool, c_prompt, c_sample, norm1_g, norm2_g, w_ada, b_ada, w_in, q_norm_g,
           k_norm_g, w_pool_grp, pool_scale, w_pool_br, w_attn_br, w_out, w_up, w_down):
    depth = w_in.shape[0]
    b, s, _ = x_prompt.shape
    n = x_sample.shape[0]
    assert x_sample.shape[1] == 1 and s % min(SUPER, s) == 0 and s % min(PROMPT_TILE, s) == 0

    mod_p, mod_s = _modulation(c_prompt, c_sample, w_ada, b_ada)
    seg = _segment_matrix()
    tab_p = _rope_tables(jnp.arange(s, dtype=F32))
    tab_s = _rope_tables(PAST_LEN + jnp.arange(1, dtype=F32))

    caches = []
    for ck, cv, d in zip((cache_k_w128, cache_k_w512, cache_k_w2048),
                         (cache_v_w128, cache_v_w512, cache_v_w2048), DILATIONS):
        for c in (ck, cv):
            assert c.shape[2] == N_KEYS * d
            caches.append(jnp.transpose(c, (0, 1, 3, 4, 2)))
    state2d = state_pool.reshape(depth, n, POOL_HIST * POOL_WIDTH)

    wpg_b = w_pool_grp.astype(BF16)
    wab_b = w_attn_br.astype(BF16)
    w_in_l = w_in[0:1].astype(BF16)

    n1 = norm1_g.reshape(depth, 1, D_MODEL)
    n2 = norm2_g.reshape(depth, 1, D_MODEL)
    gq = jnp.tile(q_norm_g, (1, ATTN_WIDTH // HEAD_DIM)).reshape(depth, 1, ATTN_WIDTH)
    gk = jnp.tile(k_norm_g, (1, ATTN_WIDTH // HEAD_DIM)).reshape(depth, 1, ATTN_WIDTH)
    ps = pool_scale.reshape(depth, 1, POOL_WIDTH)

    xp = x_prompt
    xs = x_sample.reshape(n, D_MODEL)
    ks = [[] for _ in range(N_GROUPS)]
    vs = [[] for _ in range(N_GROUPS)]
    pool_p, pool_s = [], []
    windows = None
    for l in range(depth):
        to_cast = [(w_pool_br, l), (w_out, l), (w_up, l), (w_down, l)] + ([(w_in, l + 1)] if l + 1 < depth else [])
        outs = _prompt_project(l, depth, xp, mod_p, n1, w_in_l, gq, gk, seg, *tab_p, to_cast, windows)
        u, a_pool, a_attn = outs[0:3]
        qd, kd, vd = outs[3:6], outs[6:9], outs[9:12]
        windows = outs[12:18]
        wpb_l, wo_l, wup_l, wdn_l = outs[18:22]
        tail_w = (n2, wpg_b, ps, wpb_l, wab_b, wo_l, wup_l, wdn_l)
        attn_y = _prompt_attention(qd, kd, vd)
        u_s, ap_s, aa_s, k_s, v_s, qt, kt, vt = _sample_project(l, xs, mod_s, n1, w_in_l, gq, gk, seg, *tab_s)
        if l + 1 < depth:
            w_in_l = outs[22]
        xp, yt_s = _prompt_ffn(l, xp, attn_y, u, a_pool, a_attn, mod_p, *tail_w, n, qt, kt, vt, caches)
        pool_p.append(u[:, s - POOL_HIST:, :])

        xs = _sample_ffn(l, xs, yt_s, u_s, state2d, ap_s, aa_s, mod_s, *tail_w)
        kh, vh = (a.reshape(n, N_GROUPS, HEADS_PER_GROUP, HEAD_DIM) for a in (k_s, v_s))
        for g in range(N_GROUPS):
            ks[g].append(kh[:, g:g + 1])
            vs[g].append(vh[:, g:g + 1])
        pool_s.append(jnp.concatenate([state_pool[l][:, 1:], u_s[:, None, :]], axis=1))

    st = lambda rows: jnp.stack(rows, axis=0)

    def window(a):
        return jnp.transpose(a.reshape(depth, b, HEADS_PER_GROUP, HEAD_DIM, -1), (0, 1, 4, 2, 3))

    kp, vp = windows[:N_GROUPS], windows[N_GROUPS:]
    return (xp, xs.reshape(n, 1, D_MODEL),
            window(kp[0]), window(vp[0]), window(kp[1]), window(vp[1]), window(kp[2]), window(vp[2]), st(pool_p),
            st(ks[0]), st(vs[0]), st(ks[1]), st(vs[1]), st(ks[2]), st(vs[2]), st(pool_s))
```

```python
import functools
import math

import jax
import jax.numpy as jnp
from jax import lax
from jax.experimental import pallas as pl
from jax.experimental.pallas import tpu as pltpu

F32 = jnp.float32
BF16 = jnp.bfloat16

D_MODEL = 1024
DEPTH = 4
PAST_LEN = 8192
POOL_WIDTH = 512
POOL_WINDOWS = (2, 4, 8, 16)
POOL_GROUP = 128
POOL_HIST = 15
HEAD_DIM = 64
HEADS_PER_GROUP = 4
GROUP_WIDTH = HEADS_PER_GROUP * HEAD_DIM
DILATIONS = (1, 4, 16)
WINDOWS = (128, 512, 2048)
N_GROUPS = 3
ATTN_WIDTH = N_GROUPS * GROUP_WIDTH
N_KEYS = 128
BLK = 128
D_FF = 4096
ROPE_THETA = 10000.0
EPS = 1e-6
Q_SCALE = 1.0 / math.sqrt(HEAD_DIM)
LOG2E = math.log2(math.e)
LN2 = math.log(2.0)

COL_U = 0
COL_Q = POOL_WIDTH
COL_K = COL_Q + ATTN_WIDTH
COL_V = COL_K + ATTN_WIDTH
COL_AP = COL_V + ATTN_WIDTH
COL_AA = COL_AP + D_MODEL
IN_WIDTH = COL_AA + D_MODEL

LANES = 128
PROMPT_TILE = 512
SUPER = BLK * DILATIONS[-1]
VMEM_LIMIT = 56 * 1024 * 1024


def _cparams(*sem):
    return pltpu.CompilerParams(dimension_semantics=sem, vmem_limit_bytes=VMEM_LIMIT)


def _const_spec(shape):
    nd = len(shape)
    return pl.BlockSpec(shape, lambda *_: (0,) * nd, pipeline_mode=pl.Buffered(1))


def _layer_spec(arr, layer):
    nd = arr.ndim - 1
    first = layer if arr.shape[0] > 1 else 0
    return pl.BlockSpec((None,) + arr.shape[1:], lambda *_: (first,) + (0,) * nd, pipeline_mode=pl.Buffered(1))


def _dot(a, b):
    return jnp.dot(a, b, preferred_element_type=F32)


def _rms_modulate(x, gain, shift, scale):
    ms = jnp.mean(x * x, axis=-1, keepdims=True)
    return (x * lax.rsqrt(ms + EPS) * gain) * (1.0 + scale) + shift


def _mod_kernel(cp_ref, cs_ref, w_ref, b_ref, op_ref, os_ref):
    w = w_ref[...].astype(BF16)

    def mod(c):
        s = c * (1.0 / (1.0 + jnp.exp(-c)))
        return _dot(s.astype(BF16), w) + b_ref[...]

    mp = mod(cp_ref[...])
    for bi in range(op_ref.shape[0]):
        op_ref[bi] = mp[bi:bi + 1, :]
    os_ref[...] = mod(cs_ref[...])


def _modulation(c_prompt, c_sample, w_ada, b_ada):
    b, n = c_prompt.shape[0], c_sample.shape[0]
    depth = w_ada.shape[0]
    tn = 1536
    return pl.pallas_call(
        _mod_kernel,
        grid=(depth, 6 * D_MODEL // tn),
        in_specs=[
            pl.BlockSpec((b, D_MODEL), lambda l, j: (0, 0)),
            pl.BlockSpec((n, D_MODEL), lambda l, j: (0, 0)),
            pl.BlockSpec((None, D_MODEL, tn), lambda l, j: (l, 0, j)),
            pl.BlockSpec((None, 1, tn), lambda l, j: (l, 0, j)),
        ],
        out_specs=[pl.BlockSpec((None, b, 1, tn), lambda l, j: (l, 0, 0, j)),
                   pl.BlockSpec((None, n, tn), lambda l, j: (l, 0, j))],
        out_shape=[jax.ShapeDtypeStruct((depth, b, 1, 6 * D_MODEL), F32),
                   jax.ShapeDtypeStruct((depth, n, 6 * D_MODEL), F32)],
        compiler_params=_cparams("parallel", "parallel"),
        name="adaln_mod",
    )(c_prompt, c_sample, w_ada, b_ada.reshape(depth, 1, 6 * D_MODEL))


def _head_norm_rope(z, gain_ref, seg_ref, cos_ref, sa_ref, sb_ref, out_scale, store):
    cos = cos_ref[...]
    sa = sa_ref[...]
    sb = sb_ref[...]
    for c in range(ATTN_WIDTH // GROUP_WIDTH):
        zc = z[:, c * GROUP_WIDTH:(c + 1) * GROUP_WIDTH]
        ms = _dot((zc * zc).astype(BF16), seg_ref[...])
        n = zc * lax.rsqrt(ms + EPS) * gain_ref[:, c * GROUP_WIDTH:(c + 1) * GROUP_WIDTH]
        for half in range(GROUP_WIDTH // LANES):
            xx = n[:, half * LANES:(half + 1) * LANES]
            r = xx * cos + pltpu.roll(xx, LANES - HEAD_DIM // 2, 1) * sa + pltpu.roll(xx, HEAD_DIM // 2, 1) * sb
            store(c * (GROUP_WIDTH // LANES) + half, r * out_scale if out_scale != 1.0 else r)


def _project(x_ref, mod_ref, n1_ref, w_ref, gq_ref, gk_ref, seg_ref, cos_ref, sa_ref, sb_ref,
             store_q, store_k, store_v, q_scale, after_qkv=None):
    x = x_ref[...]
    h = _rms_modulate(x, n1_ref[...], mod_ref[:, 0:D_MODEL], mod_ref[:, D_MODEL:2 * D_MODEL])
    hb = h.astype(BF16)
    _head_norm_rope(_dot(hb, w_ref[:, COL_Q:COL_K]), gq_ref, seg_ref, cos_ref, sa_ref, sb_ref, q_scale, store_q)
    _head_norm_rope(_dot(hb, w_ref[:, COL_K:COL_V]), gk_ref, seg_ref, cos_ref, sa_ref, sb_ref, 1.0, store_k)
    v = _dot(hb, w_ref[:, COL_V:COL_AP])
    for c in range(ATTN_WIDTH // LANES):
        store_v(c, v[:, c * LANES:(c + 1) * LANES])
    if after_qkv is not None:
        after_qkv()
    u = _dot(hb, w_ref[:, COL_U:COL_Q])
    a_pool = _dot(hb, w_ref[:, COL_AP:COL_AA])
    a_attn = _dot(hb, w_ref[:, COL_AA:IN_WIDTH])
    return u, a_pool, a_attn


def _chunk_store(ref):
    def store(c, val):
        ref[c] = val
    return store


def _lane_store(ref, rows):
    def store(c, val):
        ref[0:rows, c * LANES:(c + 1) * LANES] = val
    return store


def _prompt_proj_kernel(*refs, n_cast, n_alias, win_first, win_always):
    (x_ref, mod_ref, n1_ref, w_ref, gq_ref, gk_ref, seg_ref, cos_ref, sa_ref, sb_ref), refs = refs[:10], refs[10:]
    cast_in, refs = refs[:n_cast], refs[n_cast + n_alias:]
    (u_ref, ap_ref, aa_ref, q0, q1, q2, k0, k1, k2, v0, v1, v2,
     kw0, kw1, kw2, vw0, vw1, vw2), refs = refs[:18], refs[18:]
    cast_out, (qs_ref, ks_ref, vs_ref) = refs[:n_cast], refs[n_cast:]
    tm = x_ref.shape[0]
    for src, dst in zip(cast_in, cast_out):
        dst[...] = src[...].astype(BF16)

    def write_qkv():
        for src, dsts in ((qs_ref, (q0, q1, q2)), (ks_ref, (k0, k1, k2)), (vs_ref, (v0, v1, v2))):
            for g, dst in enumerate(dsts):
                d = DILATIONS[g]
                for r in range(d):
                    rows = pl.ds(r, tm // d, stride=d) if d > 1 else slice(None)
                    for half in range(GROUP_WIDTH // LANES):
                        c = g * (GROUP_WIDTH // LANES) + half
                        dst[r, :, half * LANES:(half + 1) * LANES] = src[c, rows, :].astype(BF16)
        for g in range(N_GROUPS):
            if win_always[g]:
                write_window(g)

    def write_window(g):
        for src, dst in ((ks_ref, (kw0, kw1, kw2)[g]), (vs_ref, (vw0, vw1, vw2)[g])):
            rows = dst.shape[1]
            for half in range(GROUP_WIDTH // LANES):
                c = g * (GROUP_WIDTH // LANES) + half
                dst[half * LANES:(half + 1) * LANES, :] = src[c, tm - rows:, :].T

    u, a_pool, a_attn = _project(x_ref, mod_ref, n1_ref, w_ref, gq_ref, gk_ref, seg_ref, cos_ref, sa_ref, sb_ref,
                                 _chunk_store(qs_ref), _chunk_store(ks_ref), _chunk_store(vs_ref),
                                 Q_SCALE * LOG2E,
                                 after_qkv=write_qkv)
    u_ref[...] = u
    ap_ref[...] = a_pool.astype(BF16)
    aa_ref[...] = a_attn.astype(BF16)

    i = pl.program_id(1)
    for g in range(N_GROUPS):
        if not win_always[g]:
            pl.when(i >= win_first[g])(functools.partial(write_window, g))


def _prompt_project(layer, depth, x, mod_p, n1, w_in, gq, gk, seg, cos, sa, sb, to_cast, windows):
    b, s, _ = x.shape
    tm = min(PROMPT_TILE, s)
    nt = s // tm
    steps = b * nt
    row = lambda width: pl.BlockSpec((None, tm, width), lambda bi, i: (bi, i, 0))
    in_specs = [
        row(D_MODEL),
        pl.BlockSpec((None, None, 1, 6 * D_MODEL), lambda bi, i: (layer, bi, 0, 0)),
        _layer_spec(n1, layer),
        _layer_spec(w_in, layer),
        _layer_spec(gq, layer),
        _layer_spec(gk, layer),
        _const_spec((GROUP_WIDTH, GROUP_WIDTH)),
        pl.BlockSpec((tm, LANES), lambda bi, i: (i, 0)),
        pl.BlockSpec((tm, LANES), lambda bi, i: (i, 0)),
        pl.BlockSpec((tm, LANES), lambda bi, i: (i, 0)),
    ]
    out_shape = [jax.ShapeDtypeStruct((b, s, POOL_WIDTH), F32),
                 jax.ShapeDtypeStruct((b, s, D_MODEL), BF16),
                 jax.ShapeDtypeStruct((b, s, D_MODEL), BF16)]
    out_specs = [row(POOL_WIDTH), row(D_MODEL), row(D_MODEL)]
    for _ in range(3):
        for d in DILATIONS:
            out_shape.append(jax.ShapeDtypeStruct((b, d, s // d, GROUP_WIDTH), BF16))
            out_specs.append(pl.BlockSpec((None, d, tm // d, GROUP_WIDTH), lambda bi, i: (bi, 0, i, 0)))
    win_first = []
    for _ in range(2):
        for w in WINDOWS:
            keep = min(w, s)
            out_shape.append(jax.ShapeDtypeStruct((depth, b, GROUP_WIDTH, keep), F32))
            first = (s - keep) // tm if keep >= tm else nt - 1
            win_first.append(first)
            out_specs.append(pl.BlockSpec(
                (None, None, GROUP_WIDTH, min(keep, tm)),
                lambda bi, i, first=first: (layer, bi, 0, jnp.maximum(i - first, 0))))
    for w, wl in to_cast:
        _, k, nn = w.shape
        assert k % (steps * 16) == 0
        in_specs.append(pl.BlockSpec((None, k // steps, nn), lambda bi, i, wl=wl: (wl, bi * nt + i, 0)))
        out_specs.append(pl.BlockSpec((None, k // steps, nn), lambda bi, i: (0, bi * nt + i, 0)))
        out_shape.append(jax.ShapeDtypeStruct((1, k, nn), BF16))
    windows = list(windows or ())
    n_in = len(in_specs)
    in_specs += [pl.BlockSpec(memory_space=pl.ANY)] * len(windows)
    return pl.pallas_call(
        functools.partial(_prompt_proj_kernel, n_cast=len(to_cast), n_alias=len(windows),
                          win_first=tuple(win_first[:N_GROUPS]),
                          win_always=tuple(2 * f <= nt for f in win_first[:N_GROUPS])),
        grid=(b, nt),
        in_specs=in_specs,
        out_specs=out_specs,
        out_shape=out_shape,
        input_output_aliases={n_in + k: 12 + k for k in range(len(windows))},
        scratch_shapes=[pltpu.VMEM((ATTN_WIDTH // LANES, tm, LANES), F32)] * 3,
        compiler_params=_cparams("parallel", "arbitrary"),
        name="prompt_project",
    )(x, mod_p, n1, w_in, gq, gk, seg, cos, sa, sb, *[w for w, _ in to_cast], *windows)


def _sample_proj_kernel(x_ref, mod_ref, n1_ref, w_ref, gq_ref, gk_ref, seg_ref, cos_ref, sa_ref, sb_ref,
                        u_ref, ap_ref, aa_ref, k_ref, v_ref, qt_ref, kt_ref, vt_ref, q_sc, k_sc, v_sc):
    n = x_ref.shape[0]
    for sc in (q_sc, k_sc, v_sc):
        sc[...] = jnp.zeros_like(sc)
    u, a_pool, a_attn = _project(x_ref, mod_ref, n1_ref, w_ref, gq_ref, gk_ref, seg_ref, cos_ref, sa_ref, sb_ref,
                                 _lane_store(q_sc, n), _lane_store(k_sc, n), _lane_store(v_sc, n), Q_SCALE)
    u_ref[...] = u
    ap_ref[...] = a_pool
    aa_ref[...] = a_attn
    k_ref[...] = k_sc[0:n, :]
    v_ref[...] = v_sc[0:n, :]
    for sc, dst in ((q_sc, qt_ref), (k_sc, kt_ref), (v_sc, vt_ref)):
        for c in range(ATTN_WIDTH // LANES):
            dst[c * LANES:(c + 1) * LANES, :] = sc[:, c * LANES:(c + 1) * LANES].T


def _sample_project(layer, x, mod_s, n1, w_in, gq, gk, seg, cos, sa, sb):
    n = x.shape[0]
    assert n <= LANES
    full = lambda shape: pl.BlockSpec(shape, lambda i: (0,) * len(shape))
    in_specs = [full(x.shape), _layer_spec(mod_s, layer), _layer_spec(n1, layer), _layer_spec(w_in, layer),
                _layer_spec(gq, layer), _layer_spec(gk, layer), full(seg.shape), full(cos.shape), full(sa.shape),
                full(sb.shape)]
    shapes = [(n, POOL_WIDTH), (n, D_MODEL), (n, D_MODEL), (n, ATTN_WIDTH), (n, ATTN_WIDTH)] + [(ATTN_WIDTH, LANES)] * 3
    return pl.pallas_call(
        _sample_proj_kernel,
        grid=(1,),
        in_specs=in_specs,
        out_specs=[full(sh) for sh in shapes],
        out_shape=[jax.ShapeDtypeStruct(sh, F32) for sh in shapes],
        scratch_shapes=[pltpu.VMEM((LANES, ATTN_WIDTH), F32)] * 3,
        compiler_params=_cparams("arbitrary"),
        name="sample_project",
    )(x, mod_s, n1, w_in, gq, gk, seg, cos, sa, sb)


def _head_lane_masks():
    lane = lax.broadcasted_iota(jnp.int32, (1, GROUP_WIDTH), 1)
    return [(lane // HEAD_DIM) == h for h in range(HEADS_PER_GROUP)]


def _prompt_attn_kernel(*refs):
    ins, o_ref = refs[:15], refs[15]
    nat_o, nat_m, nat_d, bias_sc = refs[16:20]
    t = pl.program_id(1)
    hm = _head_lane_masks()
    low_head = lax.broadcasted_iota(jnp.int32, (1, LANES), 1) < HEAD_DIM
    stacked = HEADS_PER_GROUP * BLK
    row = lax.broadcasted_iota(jnp.int32, (stacked, 2 * BLK), 0)
    col = lax.broadcasted_iota(jnp.int32, (stacked, 2 * BLK), 1)
    dist = BLK + (row % BLK) - col
    band = (dist >= 0) & (dist <= N_KEYS)
    bias_sc[0] = jnp.where(band, 0.0, -jnp.inf)
    bias_sc[1] = jnp.where(band & (col >= BLK), 0.0, -jnp.inf)

    for g in range(N_GROUPS):
        d = DILATIONS[g]
        q_ref, kc_ref, kp_ref, vc_ref, vp_ref = ins[5 * g:5 * g + 5]
        nb = kc_ref.shape[1] // BLK
        halves = GROUP_WIDTH // LANES

        for r in range(d):
            for j in range(nb):
                q = q_ref[r, j * BLK:(j + 1) * BLK, :]
                if j == 0:
                    keys = jnp.concatenate([kp_ref[r], kc_ref[r, 0:BLK, :]], axis=0)
                    vals = jnp.concatenate([vp_ref[r], vc_ref[r, 0:BLK, :]], axis=0)
                    bias = bias_sc[jnp.where(t > 0, 0, 1)]
                else:
                    keys = kc_ref[r, (j - 1) * BLK:(j + 1) * BLK, :]
                    vals = vc_ref[r, (j - 1) * BLK:(j + 1) * BLK, :]
                    bias = bias_sc[0]
                zero = jnp.zeros_like(q)
                qst = jnp.concatenate([jnp.where(m, q, zero) for m in hm], axis=0)
                s = lax.dot_general(qst, keys, (((1,), (1,)), ((), ())), preferred_element_type=F32) + bias
                m = jnp.max(s, axis=-1, keepdims=True)
                p = jnp.exp2(s - m)
                den = jnp.sum(p, axis=-1, keepdims=True)
                pb = p.astype(BF16)
                first = j * BLK * d + r
                rows = pl.ds(first, BLK, stride=d) if d > 1 else slice(first, first + BLK)
                for half in range(halves):
                    ra = slice(2 * half * BLK, (2 * half + 1) * BLK)
                    rb = slice((2 * half + 1) * BLK, (2 * half + 2) * BLK)
                    o = _dot(pb[2 * half * BLK:(2 * half + 2) * BLK], vals[:, half * LANES:(half + 1) * LANES])
                    nat_o[g * halves + half, rows, :] = jnp.where(low_head, o[0:BLK], o[BLK:2 * BLK])
                    nat_m[g * halves + half, rows, :] = jnp.where(low_head, m[ra], m[rb])
                    nat_d[g * halves + half, rows, :] = jnp.where(low_head, den[ra], den[rb])

    chunk = 256
    halves = GROUP_WIDTH // LANES

    def merge(ci, carry):
        rows = pl.ds(pl.multiple_of(ci * chunk, chunk), chunk)
        for half in range(halves):
            ms = [nat_m[g * halves + half, rows, :] for g in range(N_GROUPS)]
            top = jnp.maximum(jnp.maximum(ms[0], ms[1]), ms[2])
            es = [jnp.exp2(mg - top) for mg in ms]
            acc = sum(es[g] * nat_o[g * halves + half, rows, :] for g in range(N_GROUPS))
            tot = sum(es[g] * nat_d[g * halves + half, rows, :] for g in range(N_GROUPS))
            o_ref[rows, half * LANES:(half + 1) * LANES] = (acc / tot).astype(o_ref.dtype)
        return carry

    lax.fori_loop(0, o_ref.shape[0] // chunk, merge, 0)


def _prompt_attention(qd, kd, vd):
    b = qd[0].shape[0]
    s = qd[0].shape[2]
    sb = min(SUPER, s)
    nsb = s // sb
    ins, in_specs = [], []
    for g, d in enumerate(DILATIONS):
        rpc = sb // d
        nb = rpc // BLK
        cur = pl.BlockSpec((None, d, rpc, GROUP_WIDTH), lambda bi, t: (bi, 0, t, 0))
        prev = pl.BlockSpec((None, d, BLK, GROUP_WIDTH),
                            lambda bi, t, nb=nb: (bi, 0, jnp.maximum(t * nb - 1, 0), 0))
        ins += [qd[g], kd[g], kd[g], vd[g], vd[g]]
        in_specs += [cur, cur, prev, cur, prev]
    per_position = pltpu.VMEM((N_GROUPS * GROUP_WIDTH // LANES, sb, LANES), F32)
    scratch = [per_position, per_position, per_position, pltpu.VMEM((2, HEADS_PER_GROUP * BLK, 2 * BLK), F32)]
    return pl.pallas_call(
        _prompt_attn_kernel,
        grid=(b, nsb),
        in_specs=in_specs,
        out_specs=pl.BlockSpec((None, sb, GROUP_WIDTH), lambda bi, t: (bi, t, 0)),
        out_shape=jax.ShapeDtypeStruct((b, s, GROUP_WIDTH), BF16),
        scratch_shapes=scratch,
        compiler_params=_cparams("parallel", "parallel"),
        name="prompt_attention",
    )(*ins)


def _sigmoid(a):
    return 1.0 / (1.0 + jnp.exp(-a))


FF_CHUNK = 1024
FF_DOWN_CHUNK = 2048
MERGE_COLS = 256


def _merge(x, pool_p, attn_y, a_pool, a_attn, mod_ref, n2_ref, wpg_ref, ps_ref, wpb_ref, wab_ref, wo_ref):
    ys = [_dot(p.astype(BF16), wpg_ref[g]) for g, p in enumerate(pool_p)]
    pool_y = jnp.concatenate(ys, axis=-1) * ps_ref[...]
    pool_yb = pool_y.astype(BF16)
    attn_yb = attn_y.astype(BF16)
    merged = []
    for c in range(D_MODEL // MERGE_COLS):
        cols = slice(c * MERGE_COLS, (c + 1) * MERGE_COLS)
        pb = _dot(pool_yb, wpb_ref[:, cols])
        ab = _dot(attn_yb, wab_ref[:, cols])
        gated = _sigmoid(a_pool[:, cols].astype(F32)) * pb + _sigmoid(a_attn[:, cols].astype(F32)) * ab
        merged.append(gated.astype(BF16))
    g1 = mod_ref[:, 2 * D_MODEL:3 * D_MODEL]
    x1 = x + g1 * _dot(jnp.concatenate(merged, axis=-1), wo_ref[...])
    h2 = _rms_modulate(x1, n2_ref[...], mod_ref[:, 3 * D_MODEL:4 * D_MODEL], mod_ref[:, 4 * D_MODEL:5 * D_MODEL])
    return x1, h2.astype(BF16)


def _ffn(h2b, wup_ref, wdn_ref):
    width = wup_ref.shape[1]
    down = min(FF_DOWN_CHUNK, width)
    y = None
    for d0 in range(0, width, down):
        acts = []
        for c0 in range(d0, d0 + down, FF_CHUNK):
            hid = jnp.maximum(_dot(h2b, wup_ref[:, c0:c0 + FF_CHUNK]), 0.0)
            acts.append((hid * hid).astype(BF16))
        part = _dot(acts[0] if len(acts) == 1 else jnp.concatenate(acts, axis=-1), wdn_ref[d0:d0 + down, :])
        y = part if y is None else y + part
    return y


def _prompt_ffn_kernel(x_ref, y_ref, u_ref, up_ref, ap_ref, aa_ref, mod_ref, n2_ref, wpg_ref, ps_ref,
                       wpb_ref, wab_ref, wo_ref, wup_ref, wdn_ref,
                       qt_ref, kt_ref, vt_ref, ck0, cv0, ck1, cv1, ck2, cv2,
                       o_ref, yt_ref, ext_ref):
    i = pl.program_id(1)
    step = pl.program_id(0) * pl.num_programs(1) + i
    _sample_attn_step(step, qt_ref, kt_ref, vt_ref, ((ck0, cv0), (ck1, cv1), (ck2, cv2)), yt_ref)
    tm = x_ref.shape[0]
    hist = up_ref.shape[0]
    ext_ref[0:hist, :] = jnp.where(i > 0, up_ref[...], 0.0)
    ext_ref[hist:, :] = u_ref[...]
    pos = (i * tm + lax.broadcasted_iota(jnp.int32, (tm, 1), 0)).astype(F32)
    pool_p = []
    for g, w in enumerate(POOL_WINDOWS):
        e = ext_ref[:, g * POOL_GROUP:(g + 1) * POOL_GROUP]
        acc = e
        span = 1
        while span < w:
            acc = acc + pltpu.roll(acc, span, 0)
            span *= 2
        cnt = jnp.minimum(pos + 1.0, float(w))
        pool_p.append(acc[hist:] / cnt - e[hist:])
    x1, h2b = _merge(x_ref[...], pool_p, y_ref[...], ap_ref[...], aa_ref[...], mod_ref, n2_ref, wpg_ref,
                     ps_ref, wpb_ref, wab_ref, wo_ref)
    o_ref[...] = x1 + mod_ref[:, 5 * D_MODEL:6 * D_MODEL] * _ffn(h2b, wup_ref, wdn_ref)


def _prompt_ffn(layer, x, attn_y, u, a_pool, a_attn, mod_p, n2, wpg, ps, wpb, wab, wo, wup, wdn,
                n, qt, kt, vt, cache_views):
    b, s, _ = x.shape
    tm = min(PROMPT_TILE, s)
    nt = s // tm
    assert n % (b * nt) == 0
    bt = n // (b * nt)
    hist = 16
    row = lambda width: pl.BlockSpec((None, tm, width), lambda bi, i: (bi, i, 0))
    by_head = (N_GROUPS, HEADS_PER_GROUP, HEAD_DIM, LANES)
    new = pl.BlockSpec(by_head, lambda bi, i: (0, 0, 0, 0))
    in_specs = [
        row(D_MODEL), row(GROUP_WIDTH), row(POOL_WIDTH),
        pl.BlockSpec((None, hist, POOL_WIDTH), lambda bi, i: (bi, jnp.maximum(i * (tm // hist) - 1, 0), 0)),
        row(D_MODEL), row(D_MODEL),
        pl.BlockSpec((None, None, 1, 6 * D_MODEL), lambda bi, i: (layer, bi, 0, 0)),
    ] + [_layer_spec(w, layer) for w in (n2, wpg, ps, wpb, wab, wo, wup, wdn)] + [new, new, new]
    for c in cache_views:
        in_specs.append(pl.BlockSpec((None, bt) + c.shape[2:], lambda bi, i: (layer, bi * nt + i, 0, 0, 0)))
    x_out, yt = pl.pallas_call(
        _prompt_ffn_kernel,
        grid=(b, nt),
        in_specs=in_specs,
        out_specs=[row(D_MODEL), pl.BlockSpec(by_head[1:], lambda bi, i: (0, 0, 0))],
        out_shape=[jax.ShapeDtypeStruct((b, s, D_MODEL), F32), jax.ShapeDtypeStruct(by_head[1:], F32)],
        scratch_shapes=[pltpu.VMEM((hist + tm, POOL_WIDTH), F32)],
        compiler_params=_cparams("arbitrary", "arbitrary"),
        name="prompt_merge_ffn",
    )(x, attn_y, u, u, a_pool, a_attn, mod_p, n2, wpg, ps, wpb, wab, wo, wup, wdn,
      qt.reshape(by_head), kt.reshape(by_head), vt.reshape(by_head), *cache_views)
    return x_out, yt.reshape(GROUP_WIDTH, LANES)


def _sample_ffn_kernel(x_ref, yt_ref, u_ref, st_ref, ap_ref, aa_ref, mod_ref, n2_ref, wpg_ref, ps_ref,
                       wpb_ref, wab_ref, wo_ref, wup_ref, wdn_ref, o_ref, x1_sc, h2b_sc, y_sc):
    c = pl.program_id(0)
    n = x_ref.shape[0]

    @pl.when(c == 0)
    def _():
        u = u_ref[...]
        attn_y = yt_ref[...].T[0:n, :]
        tail = jnp.zeros_like(u)
        sums = {}
        for back in range(1, POOL_HIST + 1):
            row = POOL_HIST - back
            tail = tail + st_ref[:, row * POOL_WIDTH:(row + 1) * POOL_WIDTH]
            if back + 1 in POOL_WINDOWS:
                sums[back + 1] = tail
        pool_p = []
        for g, w in enumerate(POOL_WINDOWS):
            cols = slice(g * POOL_GROUP, (g + 1) * POOL_GROUP)
            cnt = min(PAST_LEN + 1.0, float(w))
            pool_p.append((sums[w][:, cols] + u[:, cols]) / cnt - u[:, cols])
        x1, h2b = _merge(x_ref[...], pool_p, attn_y, ap_ref[...], aa_ref[...], mod_ref, n2_ref, wpg_ref,
                         ps_ref, wpb_ref, wab_ref, wo_ref)
        x1_sc[...] = x1
        h2b_sc[...] = h2b
        y_sc[...] = jnp.zeros_like(y_sc)

    y_sc[...] += _ffn(h2b_sc[...], wup_ref, wdn_ref)

    @pl.when(c == pl.num_programs(0) - 1)
    def _():
        o_ref[...] = x1_sc[...] + mod_ref[:, 5 * D_MODEL:6 * D_MODEL] * y_sc[...]


def _sample_ffn(layer, x, attn_yt, u, state2d, a_pool, a_attn, mod_s, n2, wpg, ps, wpb, wab, wo, wup, wdn):
    n = x.shape[0]
    full = lambda a: pl.BlockSpec(a.shape, lambda c: (0,) * a.ndim)
    front = (mod_s, n2, wpg, ps, wpb, wab, wo)
    in_specs = [full(a) for a in (x, attn_yt, u)] + [_layer_spec(state2d, layer)] + [full(a) for a in (a_pool, a_attn)]
    in_specs += [_layer_spec(a, layer) for a in front]
    up_layer = layer if wup.shape[0] > 1 else 0
    dn_layer = layer if wdn.shape[0] > 1 else 0
    in_specs += [pl.BlockSpec((None, D_MODEL, FF_CHUNK), lambda c: (up_layer, 0, c)),
                 pl.BlockSpec((None, FF_CHUNK, D_MODEL), lambda c: (dn_layer, c, 0))]
    return pl.pallas_call(
        _sample_ffn_kernel,
        grid=(D_FF // FF_CHUNK,),
        in_specs=in_specs,
        out_specs=pl.BlockSpec((n, D_MODEL), lambda c: (0, 0)),
        out_shape=jax.ShapeDtypeStruct((n, D_MODEL), F32),
        scratch_shapes=[pltpu.VMEM((n, D_MODEL), F32), pltpu.VMEM((n, D_MODEL), BF16), pltpu.VMEM((n, D_MODEL), F32)],
        compiler_params=_cparams("arbitrary"),
        name="sample_merge_ffn",
    )(x, attn_yt, u, state2d, a_pool, a_attn, *front, wup, wdn)


def _sample_attn_step(step, q_ref, k_ref, v_ref, caches, o_ref):
    bt = caches[0][0].shape[0]
    lane = lax.broadcasted_iota(jnp.int32, (HEAD_DIM, LANES), 1)

    @pl.when(step == 0)
    def _():
        o_ref[...] = jnp.zeros_like(o_ref)

    for b in range(bt):
        mine = lane == step * bt + b
        for h in range(HEADS_PER_GROUP):
            column = lambda ref, g: jnp.sum(jnp.where(mine, ref[g, h], 0.0), axis=1, keepdims=True)
            outs, lses = [], []
            for g, (ck, cv) in enumerate(caches):
                d = DILATIONS[g]
                q = column(q_ref, g)
                s = jnp.sum(ck[b, h] * q, axis=0, keepdims=True)
                if d > 1:
                    row = lax.broadcasted_iota(jnp.int32, s.shape, 1)
                    s = jnp.where(row % d == 0, s, -jnp.inf)
                s_new = jnp.sum(column(k_ref, g) * q, axis=0, keepdims=True)
                m = jnp.maximum(jnp.max(s, axis=1, keepdims=True), s_new)
                p = jnp.exp(s - m)
                p_new = jnp.exp(s_new - m)
                den = jnp.sum(p, axis=1, keepdims=True) + p_new
                acc = jnp.sum(cv[b, h] * p, axis=1, keepdims=True) + p_new * column(v_ref, g)
                outs.append(acc / den)
                lses.append(m + jnp.log(den))
            mx = jnp.maximum(jnp.maximum(lses[0], lses[1]), lses[2])
            es = [jnp.exp(l - mx) for l in lses]
            merged = (es[0] * outs[0] + es[1] * outs[1] + es[2] * outs[2]) / (es[0] + es[1] + es[2])
            o_ref[h] = jnp.where(mine, merged, o_ref[h])


def _rope_tables(pos):
    inv_freq = ROPE_THETA ** (-jnp.arange(0, HEAD_DIM, 2, dtype=F32) / HEAD_DIM)
    ang = pos[:, None] * inv_freq[None, :]
    cos, sin = jnp.cos(ang), jnp.sin(ang)
    zero = jnp.zeros_like(sin)
    reps = LANES // HEAD_DIM
    cos_t = jnp.tile(jnp.concatenate([cos, cos], axis=-1), (1, reps))
    sa_t = jnp.tile(jnp.concatenate([-sin, zero], axis=-1), (1, reps))
    sb_t = jnp.tile(jnp.concatenate([zero, sin], axis=-1), (1, reps))
    return cos_t, sa_t, sb_t


def _segment_matrix():
    lane = jnp.arange(GROUP_WIDTH)
    seg = (lane[:, None] // HEAD_DIM == lane[None, :] // HEAD_DIM).astype(F32) / HEAD_DIM
    return seg.astype(BF16)


def kernel(x_prompt, x_sample, cache_k_w128, cache_v_w128, cache_k_w512, cache_v_w512, cache_k_w2048,
           cache_v_w2048, state_pool, c_prompt, c_sample, norm1_g, norm2_g, w_ada, b_ada, w_in, q_norm_g,
           k_norm_g, w_pool_grp, pool_scale, w_pool_br, w_attn_br, w_out, w_up, w_down):
    depth = w_in.shape[0]
    b, s, _ = x_prompt.shape
    n = x_sample.shape[0]
    assert x_sample.shape[1] == 1 and s % min(SUPER, s) == 0 and s % min(PROMPT_TILE, s) == 0

    mod_p, mod_s = _modulation(c_prompt, c_sample, w_ada, b_ada)
    seg = _segment_matrix()
    tab_p = _rope_tables(jnp.arange(s, dtype=F32))
    tab_s = _rope_tables(PAST_LEN + jnp.arange(1, dtype=F32))

    caches = []
    for ck, cv, d in zip((cache_k_w128, cache_k_w512, cache_k_w2048),
                         (cache_v_w128, cache_v_w512, cache_v_w2048), DILATIONS):
        for c in (ck, cv):
            assert c.shape[2] == N_KEYS * d
            caches.append(jnp.transpose(c, (0, 1, 3, 4, 2)))
    state2d = state_pool.reshape(depth, n, POOL_HIST * POOL_WIDTH)

    wpg_b = w_pool_grp.astype(BF16)
    wab_b = w_attn_br.astype(BF16)
    w_in_l = w_in[0:1].astype(BF16)

    n1 = norm1_g.reshape(depth, 1, D_MODEL)
    n2 = norm2_g.reshape(depth, 1, D_MODEL)
    gq = jnp.tile(q_norm_g, (1, ATTN_WIDTH // HEAD_DIM)).reshape(depth, 1, ATTN_WIDTH)
    gk = jnp.tile(k_norm_g, (1, ATTN_WIDTH // HEAD_DIM)).reshape(depth, 1, ATTN_WIDTH)
    ps = pool_scale.reshape(depth, 1, POOL_WIDTH)

    xp = x_prompt
    xs = x_sample.reshape(n, D_MODEL)
    ks = [[] for _ in range(N_GROUPS)]
    vs = [[] for _ in range(N_GROUPS)]
    pool_p, pool_s = [], []
    windows = None
    for l in range(depth):
        to_cast = [(w_pool_br, l), (w_out, l), (w_up, l), (w_down, l)] + ([(w_in, l + 1)] if l + 1 < depth else [])
        outs = _prompt_project(l, depth, xp, mod_p, n1, w_in_l, gq, gk, seg, *tab_p, to_cast, windows)
        u, a_pool, a_attn = outs[0:3]
        qd, kd, vd = outs[3:6], outs[6:9], outs[9:12]
        windows = outs[12:18]
        wpb_l, wo_l, wup_l, wdn_l = outs[18:22]
        tail_w = (n2, wpg_b, ps, wpb_l, wab_b, wo_l, wup_l, wdn_l)
        attn_y = _prompt_attention(qd, kd, vd)
        u_s, ap_s, aa_s, k_s, v_s, qt, kt, vt = _sample_project(l, xs, mod_s, n1, w_in_l, gq, gk, seg, *tab_s)
        if l + 1 < depth:
            w_in_l = outs[22]
        xp, yt_s = _prompt_ffn(l, xp, attn_y, u, a_pool, a_attn, mod_p, *tail_w, n, qt, kt, vt, caches)
        pool_p.append(u[:, s - POOL_HIST:, :])

        xs = _sample_ffn(l, xs, yt_s, u_s, state2d, ap_s, aa_s, mod_s, *tail_w)
        kh, vh = (a.reshape(n, N_GROUPS, HEADS_PER_GROUP, HEAD_DIM) for a in (k_s, v_s))
        for g in range(N_GROUPS):
            ks[g].append(kh[:, g:g + 1])
            vs[g].append(vh[:, g:g + 1])
        pool_s.append(jnp.concatenate([state_pool[l][:, 1:], u_s[:, None, :]], axis=1))

    st = lambda rows: jnp.stack(rows, axis=0)

    def window(a):
        return jnp.transpose(a.reshape(depth, b, HEADS_PER_GROUP, HEAD_DIM, -1), (0, 1, 4, 2, 3))

    kp, vp = windows[:N_GROUPS], windows[N_GROUPS:]
    return (xp, xs.reshape(n, 1, D_MODEL),
            window(kp[0]), window(vp[0]), window(kp[1]), window(vp[1]), window(kp[2]), window(vp[2]), st(pool_p),
            st(ks[0]), st(vs[0]), st(ks[1]), st(vs[1]), st(ks[2]), st(vs[2]), st(pool_s))
```

```python
import functools
import math

import jax
import jax.numpy as jnp
from jax import lax
from jax.experimental import pallas as pl
from jax.experimental.pallas import tpu as pltpu

F32 = jnp.float32
BF16 = jnp.bfloat16

D_MODEL = 1024
DEPTH = 4
PAST_LEN = 8192
POOL_WIDTH = 512
POOL_WINDOWS = (2, 4, 8, 16)
POOL_GROUP = 128
POOL_HIST = 15
HEAD_DIM = 64
HEADS_PER_GROUP = 4
GROUP_WIDTH = HEADS_PER_GROUP * HEAD_DIM
DILATIONS = (1, 4, 16)
WINDOWS = (128, 512, 2048)
N_GROUPS = 3
ATTN_WIDTH = N_GROUPS * GROUP_WIDTH
N_KEYS = 128
BLK = 128
D_FF = 4096
ROPE_THETA = 10000.0
EPS = 1e-6
Q_SCALE = 1.0 / math.sqrt(HEAD_DIM)
LOG2E = math.log2(math.e)
LN2 = math.log(2.0)

COL_U = 0
COL_Q = POOL_WIDTH
COL_K = COL_Q + ATTN_WIDTH
COL_V = COL_K + ATTN_WIDTH
COL_AP = COL_V + ATTN_WIDTH
COL_AA = COL_AP + D_MODEL
IN_WIDTH = COL_AA + D_MODEL

LANES = 128
PROMPT_TILE = 512
SUPER = BLK * DILATIONS[-1]
VMEM_LIMIT = 56 * 1024 * 1024


def _cparams(*sem):
    return pltpu.CompilerParams(dimension_semantics=sem, vmem_limit_bytes=VMEM_LIMIT)


def _layer_spec(arr, layer):
    nd = arr.ndim - 1
    first = layer if arr.shape[0] > 1 else 0
    return pl.BlockSpec((None,) + arr.shape[1:], lambda *_: (first,) + (0,) * nd, pipeline_mode=pl.Buffered(1))


def _dot(a, b):
    return jnp.dot(a, b, preferred_element_type=F32)


def _rms_modulate(x, gain, shift, scale):
    ms = jnp.mean(x * x, axis=-1, keepdims=True)
    return (x * lax.rsqrt(ms + EPS) * gain) * (1.0 + scale) + shift


def _mod_kernel(cp_ref, cs_ref, w_ref, b_ref, op_ref, os_ref):
    w = w_ref[...].astype(BF16)

    def mod(c):
        s = c * (1.0 / (1.0 + jnp.exp(-c)))
        return _dot(s.astype(BF16), w) + b_ref[...]

    mp = mod(cp_ref[...])
    for bi in range(op_ref.shape[0]):
        op_ref[bi] = mp[bi:bi + 1, :]
    os_ref[...] = mod(cs_ref[...])


def _modulation(c_prompt, c_sample, w_ada, b_ada):
    b, n = c_prompt.shape[0], c_sample.shape[0]
    depth = w_ada.shape[0]
    tn = 1536
    return pl.pallas_call(
        _mod_kernel,
        grid=(depth, 6 * D_MODEL // tn),
        in_specs=[
            pl.BlockSpec((b, D_MODEL), lambda l, j: (0, 0)),
            pl.BlockSpec((n, D_MODEL), lambda l, j: (0, 0)),
            pl.BlockSpec((None, D_MODEL, tn), lambda l, j: (l, 0, j)),
            pl.BlockSpec((None, 1, tn), lambda l, j: (l, 0, j)),
        ],
        out_specs=[pl.BlockSpec((None, b, 1, tn), lambda l, j: (l, 0, 0, j)),
                   pl.BlockSpec((None, n, tn), lambda l, j: (l, 0, j))],
        out_shape=[jax.ShapeDtypeStruct((depth, b, 1, 6 * D_MODEL), F32),
                   jax.ShapeDtypeStruct((depth, n, 6 * D_MODEL), F32)],
        compiler_params=_cparams("parallel", "parallel"),
        name="adaln_mod",
    )(c_prompt, c_sample, w_ada, b_ada.reshape(depth, 1, 6 * D_MODEL))


def _head_norm_rope(z, gain_ref, cos_ref, sa_ref, sb_ref, out_scale, store):
    cos = cos_ref[...]
    sa = sa_ref[...]
    sb = sb_ref[...]
    low = lax.broadcasted_iota(jnp.int32, (1, LANES), 1) < HEAD_DIM
    for c in range(ATTN_WIDTH // LANES):
        zc = z[:, c * LANES:(c + 1) * LANES]
        sq = zc * zc
        lo = jnp.sum(jnp.where(low, sq, 0.0), axis=-1, keepdims=True)
        hi = jnp.sum(jnp.where(low, 0.0, sq), axis=-1, keepdims=True)
        ms = jnp.where(low, lo, hi) * (1.0 / HEAD_DIM)
        xx = zc * lax.rsqrt(ms + EPS) * gain_ref[:, c * LANES:(c + 1) * LANES]
        r = xx * cos + pltpu.roll(xx, LANES - HEAD_DIM // 2, 1) * sa + pltpu.roll(xx, HEAD_DIM // 2, 1) * sb
        store(c, r * out_scale if out_scale != 1.0 else r)


def _project(x_ref, mod_ref, n1_ref, w_ref, gq_ref, gk_ref, cos_ref, sa_ref, sb_ref,
             store_q, store_k, store_v, q_scale, after_qkv=None):
    x = x_ref[...]
    h = _rms_modulate(x, n1_ref[...], mod_ref[:, 0:D_MODEL], mod_ref[:, D_MODEL:2 * D_MODEL])
    hb = h.astype(BF16)
    _head_norm_rope(_dot(hb, w_ref[:, COL_Q:COL_K]), gq_ref, cos_ref, sa_ref, sb_ref, q_scale, store_q)
    _head_norm_rope(_dot(hb, w_ref[:, COL_K:COL_V]), gk_ref, cos_ref, sa_ref, sb_ref, 1.0, store_k)
    v = _dot(hb, w_ref[:, COL_V:COL_AP])
    for c in range(ATTN_WIDTH // LANES):
        store_v(c, v[:, c * LANES:(c + 1) * LANES])
    if after_qkv is not None:
        after_qkv()
    u = _dot(hb, w_ref[:, COL_U:COL_Q])
    a_pool = _dot(hb, w_ref[:, COL_AP:COL_AA])
    a_attn = _dot(hb, w_ref[:, COL_AA:IN_WIDTH])
    return u, a_pool, a_attn


def _chunk_store(ref):
    def store(c, val):
        ref[c] = val
    return store


def _lane_store(ref, rows):
    def store(c, val):
        ref[0:rows, c * LANES:(c + 1) * LANES] = val
    return store


def _prompt_proj_kernel(*refs, n_cast, n_alias, win_first, win_always):
    (x_ref, mod_ref, n1_ref, w_ref, gq_ref, gk_ref, cos_ref, sa_ref, sb_ref), refs = refs[:9], refs[9:]
    cast_in, refs = refs[:n_cast], refs[n_cast + n_alias:]
    (u_ref, ap_ref, aa_ref, q0, q1, q2, k0, k1, k2, v0, v1, v2,
     kw0, kw1, kw2, vw0, vw1, vw2), refs = refs[:18], refs[18:]
    cast_out, (qs_ref, ks_ref, vs_ref) = refs[:n_cast], refs[n_cast:]
    tm = x_ref.shape[0]
    for src, dst in zip(cast_in, cast_out):
        dst[...] = src[...].astype(BF16)

    def write_qkv():
        for src, dsts in ((qs_ref, (q0, q1, q2)), (ks_ref, (k0, k1, k2)), (vs_ref, (v0, v1, v2))):
            for g, dst in enumerate(dsts):
                d = DILATIONS[g]
                for r in range(d):
                    rows = pl.ds(r, tm // d, stride=d) if d > 1 else slice(None)
                    for half in range(GROUP_WIDTH // LANES):
                        c = g * (GROUP_WIDTH // LANES) + half
                        dst[r, :, half * LANES:(half + 1) * LANES] = src[c, rows, :].astype(BF16)
        for g in range(N_GROUPS):
            if win_always[g]:
                write_window(g)

    def write_window(g):
        for src, dst in ((ks_ref, (kw0, kw1, kw2)[g]), (vs_ref, (vw0, vw1, vw2)[g])):
            rows = dst.shape[1]
            for half in range(GROUP_WIDTH // LANES):
                c = g * (GROUP_WIDTH // LANES) + half
                dst[half * LANES:(half + 1) * LANES, :] = src[c, tm - rows:, :].T

    u, a_pool, a_attn = _project(x_ref, mod_ref, n1_ref, w_ref, gq_ref, gk_ref, cos_ref, sa_ref, sb_ref,
                                 _chunk_store(qs_ref), _chunk_store(ks_ref), _chunk_store(vs_ref),
                                 Q_SCALE * LOG2E,
                                 after_qkv=write_qkv)
    u_ref[...] = u
    ap_ref[...] = a_pool.astype(BF16)
    aa_ref[...] = a_attn.astype(BF16)

    i = pl.program_id(1)
    for g in range(N_GROUPS):
        if not win_always[g]:
            pl.when(i >= win_first[g])(functools.partial(write_window, g))


def _prompt_project(layer, depth, x, mod_p, n1, w_in, gq, gk, cos, sa, sb, to_cast, windows):
    b, s, _ = x.shape
    tm = min(PROMPT_TILE, s)
    nt = s // tm
    steps = b * nt
    row = lambda width: pl.BlockSpec((None, tm, width), lambda bi, i: (bi, i, 0))
    in_specs = [
        row(D_MODEL),
        pl.BlockSpec((None, None, 1, 6 * D_MODEL), lambda bi, i: (layer, bi, 0, 0)),
        _layer_spec(n1, layer),
        _layer_spec(w_in, layer),
        _layer_spec(gq, layer),
        _layer_spec(gk, layer),
        pl.BlockSpec((tm, LANES), lambda bi, i: (i, 0)),
        pl.BlockSpec((tm, LANES), lambda bi, i: (i, 0)),
        pl.BlockSpec((tm, LANES), lambda bi, i: (i, 0)),
    ]
    out_shape = [jax.ShapeDtypeStruct((b, s, POOL_WIDTH), F32),
                 jax.ShapeDtypeStruct((b, s, D_MODEL), BF16),
                 jax.ShapeDtypeStruct((b, s, D_MODEL), BF16)]
    out_specs = [row(POOL_WIDTH), row(D_MODEL), row(D_MODEL)]
    for _ in range(3):
        for d in DILATIONS:
            out_shape.append(jax.ShapeDtypeStruct((b, d, s // d, GROUP_WIDTH), BF16))
            out_specs.append(pl.BlockSpec((None, d, tm // d, GROUP_WIDTH), lambda bi, i: (bi, 0, i, 0)))
    win_first = []
    for _ in range(2):
        for w in WINDOWS:
            keep = min(w, s)
            out_shape.append(jax.ShapeDtypeStruct((depth, b, GROUP_WIDTH, keep), F32))
            first = (s - keep) // tm if keep >= tm else nt - 1
            win_first.append(first)
            out_specs.append(pl.BlockSpec(
                (None, None, GROUP_WIDTH, min(keep, tm)),
                lambda bi, i, first=first: (layer, bi, 0, jnp.maximum(i - first, 0))))
    for w, wl in to_cast:
        _, k, nn = w.shape
        assert k % (steps * 16) == 0
        in_specs.append(pl.BlockSpec((None, k // steps, nn), lambda bi, i, wl=wl: (wl, bi * nt + i, 0)))
        out_specs.append(pl.BlockSpec((None, k // steps, nn), lambda bi, i: (0, bi * nt + i, 0)))
        out_shape.append(jax.ShapeDtypeStruct((1, k, nn), BF16))
    windows = list(windows or ())
    n_in = len(in_specs)
    in_specs += [pl.BlockSpec(memory_space=pl.ANY)] * len(windows)
    return pl.pallas_call(
        functools.partial(_prompt_proj_kernel, n_cast=len(to_cast), n_alias=len(windows),
                          win_first=tuple(win_first[:N_GROUPS]),
                          win_always=tuple(2 * f <= nt for f in win_first[:N_GROUPS])),
        grid=(b, nt),
        in_specs=in_specs,
        out_specs=out_specs,
        out_shape=out_shape,
        input_output_aliases={n_in + k: 12 + k for k in range(len(windows))},
        scratch_shapes=[pltpu.VMEM((ATTN_WIDTH // LANES, tm, LANES), F32)] * 3,
        compiler_params=_cparams("parallel", "arbitrary"),
        name="prompt_project",
    )(x, mod_p, n1, w_in, gq, gk, cos, sa, sb, *[w for w, _ in to_cast], *windows)


def _sample_proj_kernel(x_ref, mod_ref, n1_ref, w_ref, gq_ref, gk_ref, cos_ref, sa_ref, sb_ref,
                        u_ref, ap_ref, aa_ref, k_ref, v_ref, qt_ref, kt_ref, vt_ref, q_sc, k_sc, v_sc):
    n = x_ref.shape[0]
    for sc in (q_sc, k_sc, v_sc):
        sc[...] = jnp.zeros_like(sc)
    u, a_pool, a_attn = _project(x_ref, mod_ref, n1_ref, w_ref, gq_ref, gk_ref, cos_ref, sa_ref, sb_ref,
                                 _lane_store(q_sc, n), _lane_store(k_sc, n), _lane_store(v_sc, n), Q_SCALE)
    u_ref[...] = u
    ap_ref[...] = a_pool
    aa_ref[...] = a_attn
    k_ref[...] = k_sc[0:n, :]
    v_ref[...] = v_sc[0:n, :]
    for sc, dst in ((q_sc, qt_ref), (k_sc, kt_ref), (v_sc, vt_ref)):
        for c in range(ATTN_WIDTH // LANES):
            dst[c * LANES:(c + 1) * LANES, :] = sc[:, c * LANES:(c + 1) * LANES].T


def _sample_project(layer, x, mod_s, n1, w_in, gq, gk, cos, sa, sb):
    n = x.shape[0]
    assert n <= LANES
    full = lambda shape: pl.BlockSpec(shape, lambda i: (0,) * len(shape))
    in_specs = [full(x.shape), _layer_spec(mod_s, layer), _layer_spec(n1, layer), _layer_spec(w_in, layer),
                _layer_spec(gq, layer), _layer_spec(gk, layer), full(cos.shape), full(sa.shape), full(sb.shape)]
    shapes = [(n, POOL_WIDTH), (n, D_MODEL), (n, D_MODEL), (n, ATTN_WIDTH), (n, ATTN_WIDTH)] + [(ATTN_WIDTH, LANES)] * 3
    return pl.pallas_call(
        _sample_proj_kernel,
        grid=(1,),
        in_specs=in_specs,
        out_specs=[full(sh) for sh in shapes],
        out_shape=[jax.ShapeDtypeStruct(sh, F32) for sh in shapes],
        scratch_shapes=[pltpu.VMEM((LANES, ATTN_WIDTH), F32)] * 3,
        compiler_params=_cparams("arbitrary"),
        name="sample_project",
    )(x, mod_s, n1, w_in, gq, gk, cos, sa, sb)


def _head_lane_masks():
    lane = lax.broadcasted_iota(jnp.int32, (1, GROUP_WIDTH), 1)
    return [(lane // HEAD_DIM) == h for h in range(HEADS_PER_GROUP)]


def _prompt_attn_kernel(*refs):
    ins, o_ref = refs[:15], refs[15]
    nat_o, nat_m, nat_d, bias_sc = refs[16:20]
    t = pl.program_id(1)
    hm = _head_lane_masks()
    low_head = lax.broadcasted_iota(jnp.int32, (1, LANES), 1) < HEAD_DIM
    stacked = HEADS_PER_GROUP * BLK
    row = lax.broadcasted_iota(jnp.int32, (stacked, 2 * BLK), 0)
    col = lax.broadcasted_iota(jnp.int32, (stacked, 2 * BLK), 1)
    dist = BLK + (row % BLK) - col
    band = (dist >= 0) & (dist <= N_KEYS)
    bias_sc[0] = jnp.where(band, 0.0, -jnp.inf)
    bias_sc[1] = jnp.where(band & (col >= BLK), 0.0, -jnp.inf)

    for g in range(N_GROUPS):
        d = DILATIONS[g]
        q_ref, kc_ref, kp_ref, vc_ref, vp_ref = ins[5 * g:5 * g + 5]
        nb = kc_ref.shape[1] // BLK
        halves = GROUP_WIDTH // LANES

        for r in range(d):
            for j in range(nb):
                q = q_ref[r, j * BLK:(j + 1) * BLK, :]
                if j == 0:
                    keys = jnp.concatenate([kp_ref[r], kc_ref[r, 0:BLK, :]], axis=0)
                    vals = jnp.concatenate([vp_ref[r], vc_ref[r, 0:BLK, :]], axis=0)
                    bias = bias_sc[jnp.where(t > 0, 0, 1)]
                else:
                    keys = kc_ref[r, (j - 1) * BLK:(j + 1) * BLK, :]
                    vals = vc_ref[r, (j - 1) * BLK:(j + 1) * BLK, :]
                    bias = bias_sc[0]
                zero = jnp.zeros_like(q)
                qst = jnp.concatenate([jnp.where(m, q, zero) for m in hm], axis=0)
                s = lax.dot_general(qst, keys, (((1,), (1,)), ((), ())), preferred_element_type=F32) + bias
                m = jnp.max(s, axis=-1, keepdims=True)
                p = jnp.exp2(s - m)
                den = jnp.sum(p, axis=-1, keepdims=True)
                pb = p.astype(BF16)
                first = j * BLK * d + r
                rows = pl.ds(first, BLK, stride=d) if d > 1 else slice(first, first + BLK)
                for half in range(halves):
                    ra = slice(2 * half * BLK, (2 * half + 1) * BLK)
                    rb = slice((2 * half + 1) * BLK, (2 * half + 2) * BLK)
                    o = _dot(pb[2 * half * BLK:(2 * half + 2) * BLK], vals[:, half * LANES:(half + 1) * LANES])
                    nat_o[g * halves + half, rows, :] = jnp.where(low_head, o[0:BLK], o[BLK:2 * BLK])
                    nat_m[g * halves + half, rows, :] = jnp.where(low_head, m[ra], m[rb])
                    nat_d[g * halves + half, rows, :] = jnp.where(low_head, den[ra], den[rb])

    chunk = 256
    halves = GROUP_WIDTH // LANES

    def merge(ci, carry):
        rows = pl.ds(pl.multiple_of(ci * chunk, chunk), chunk)
        for half in range(halves):
            ms = [nat_m[g * halves + half, rows, :] for g in range(N_GROUPS)]
            top = jnp.maximum(jnp.maximum(ms[0], ms[1]), ms[2])
            es = [jnp.exp2(mg - top) for mg in ms]
            acc = sum(es[g] * nat_o[g * halves + half, rows, :] for g in range(N_GROUPS))
            tot = sum(es[g] * nat_d[g * halves + half, rows, :] for g in range(N_GROUPS))
            o_ref[rows, half * LANES:(half + 1) * LANES] = (acc / tot).astype(o_ref.dtype)
        return carry

    lax.fori_loop(0, o_ref.shape[0] // chunk, merge, 0)


def _prompt_attention(qd, kd, vd):
    b = qd[0].shape[0]
    s = qd[0].shape[2]
    sb = min(SUPER, s)
    nsb = s // sb
    ins, in_specs = [], []
    for g, d in enumerate(DILATIONS):
        rpc = sb // d
        nb = rpc // BLK
        cur = pl.BlockSpec((None, d, rpc, GROUP_WIDTH), lambda bi, t: (bi, 0, t, 0))
        prev = pl.BlockSpec((None, d, BLK, GROUP_WIDTH),
                            lambda bi, t, nb=nb: (bi, 0, jnp.maximum(t * nb - 1, 0), 0))
        ins += [qd[g], kd[g], kd[g], vd[g], vd[g]]
        in_specs += [cur, cur, prev, cur, prev]
    per_position = pltpu.VMEM((N_GROUPS * GROUP_WIDTH // LANES, sb, LANES), F32)
    scratch = [per_position, per_position, per_position, pltpu.VMEM((2, HEADS_PER_GROUP * BLK, 2 * BLK), F32)]
    return pl.pallas_call(
        _prompt_attn_kernel,
        grid=(b, nsb),
        in_specs=in_specs,
        out_specs=pl.BlockSpec((None, sb, GROUP_WIDTH), lambda bi, t: (bi, t, 0)),
        out_shape=jax.ShapeDtypeStruct((b, s, GROUP_WIDTH), BF16),
        scratch_shapes=scratch,
        compiler_params=_cparams("parallel", "parallel"),
        name="prompt_attention",
    )(*ins)


def _sigmoid(a):
    return 1.0 / (1.0 + jnp.exp(-a))


FF_CHUNK = 1024
FF_DOWN_CHUNK = 2048
MERGE_COLS = 256


def _merge(x, pool_p, attn_y, a_pool, a_attn, mod_ref, n2_ref, wpg_ref, ps_ref, wpb_ref, wab_ref, wo_ref):
    ys = [_dot(p.astype(BF16), wpg_ref[g]) for g, p in enumerate(pool_p)]
    pool_y = jnp.concatenate(ys, axis=-1) * ps_ref[...]
    pool_yb = pool_y.astype(BF16)
    attn_yb = attn_y.astype(BF16)
    merged = []
    for c in range(D_MODEL // MERGE_COLS):
        cols = slice(c * MERGE_COLS, (c + 1) * MERGE_COLS)
        pb = _dot(pool_yb, wpb_ref[:, cols])
        ab = _dot(attn_yb, wab_ref[:, cols])
        gated = _sigmoid(a_pool[:, cols].astype(F32)) * pb + _sigmoid(a_attn[:, cols].astype(F32)) * ab
        merged.append(gated.astype(BF16))
    g1 = mod_ref[:, 2 * D_MODEL:3 * D_MODEL]
    x1 = x + g1 * _dot(jnp.concatenate(merged, axis=-1), wo_ref[...])
    h2 = _rms_modulate(x1, n2_ref[...], mod_ref[:, 3 * D_MODEL:4 * D_MODEL], mod_ref[:, 4 * D_MODEL:5 * D_MODEL])
    return x1, h2.astype(BF16)


def _ffn(h2b, wup_ref, wdn_ref):
    width = wup_ref.shape[1]
    down = min(FF_DOWN_CHUNK, width)
    y = None
    for d0 in range(0, width, down):
        acts = []
        for c0 in range(d0, d0 + down, FF_CHUNK):
            hid = jnp.maximum(_dot(h2b, wup_ref[:, c0:c0 + FF_CHUNK]), 0.0)
            acts.append((hid * hid).astype(BF16))
        part = _dot(acts[0] if len(acts) == 1 else jnp.concatenate(acts, axis=-1), wdn_ref[d0:d0 + down, :])
        y = part if y is None else y + part
    return y


def _prompt_ffn_kernel(x_ref, y_ref, u_ref, up_ref, ap_ref, aa_ref, mod_ref, n2_ref, wpg_ref, ps_ref,
                       wpb_ref, wab_ref, wo_ref, wup_ref, wdn_ref,
                       qt_ref, kt_ref, vt_ref, ck0, cv0, ck1, cv1, ck2, cv2,
                       o_ref, yt_ref, ext_ref):
    i = pl.program_id(1)
    step = pl.program_id(0) * pl.num_programs(1) + i
    _sample_attn_step(step, qt_ref, kt_ref, vt_ref, ((ck0, cv0), (ck1, cv1), (ck2, cv2)), yt_ref)
    tm = x_ref.shape[0]
    hist = up_ref.shape[0]
    ext_ref[0:hist, :] = jnp.where(i > 0, up_ref[...], 0.0)
    ext_ref[hist:, :] = u_ref[...]
    pos = (i * tm + lax.broadcasted_iota(jnp.int32, (tm, 1), 0)).astype(F32)
    pool_p = []
    for g, w in enumerate(POOL_WINDOWS):
        e = ext_ref[:, g * POOL_GROUP:(g + 1) * POOL_GROUP]
        acc = e
        span = 1
        while span < w:
            acc = acc + pltpu.roll(acc, span, 0)
            span *= 2
        cnt = jnp.minimum(pos + 1.0, float(w))
        pool_p.append(acc[hist:] / cnt - e[hist:])
    x1, h2b = _merge(x_ref[...], pool_p, y_ref[...], ap_ref[...], aa_ref[...], mod_ref, n2_ref, wpg_ref,
                     ps_ref, wpb_ref, wab_ref, wo_ref)
    o_ref[...] = x1 + mod_ref[:, 5 * D_MODEL:6 * D_MODEL] * _ffn(h2b, wup_ref, wdn_ref)


def _prompt_ffn(layer, x, attn_y, u, a_pool, a_attn, mod_p, n2, wpg, ps, wpb, wab, wo, wup, wdn,
                n, qt, kt, vt, cache_views):
    b, s, _ = x.shape
    tm = min(PROMPT_TILE, s)
    nt = s // tm
    assert n % (b * nt) == 0
    bt = n // (b * nt)
    hist = 16
    row = lambda width: pl.BlockSpec((None, tm, width), lambda bi, i: (bi, i, 0))
    by_head = (N_GROUPS, HEADS_PER_GROUP, HEAD_DIM, LANES)
    new = pl.BlockSpec(by_head, lambda bi, i: (0, 0, 0, 0))
    in_specs = [
        row(D_MODEL), row(GROUP_WIDTH), row(POOL_WIDTH),
        pl.BlockSpec((None, hist, POOL_WIDTH), lambda bi, i: (bi, jnp.maximum(i * (tm // hist) - 1, 0), 0)),
        row(D_MODEL), row(D_MODEL),
        pl.BlockSpec((None, None, 1, 6 * D_MODEL), lambda bi, i: (layer, bi, 0, 0)),
    ] + [_layer_spec(w, layer) for w in (n2, wpg, ps, wpb, wab, wo, wup, wdn)] + [new, new, new]
    for c in cache_views:
        in_specs.append(pl.BlockSpec((None, bt) + c.shape[2:], lambda bi, i: (layer, bi * nt + i, 0, 0, 0)))
    x_out, yt = pl.pallas_call(
        _prompt_ffn_kernel,
        grid=(b, nt),
        in_specs=in_specs,
        out_specs=[row(D_MODEL), pl.BlockSpec(by_head[1:], lambda bi, i: (0, 0, 0))],
        out_shape=[jax.ShapeDtypeStruct((b, s, D_MODEL), F32), jax.ShapeDtypeStruct(by_head[1:], F32)],
        scratch_shapes=[pltpu.VMEM((hist + tm, POOL_WIDTH), F32)],
        compiler_params=_cparams("arbitrary", "arbitrary"),
        name="prompt_merge_ffn",
    )(x, attn_y, u, u, a_pool, a_attn, mod_p, n2, wpg, ps, wpb, wab, wo, wup, wdn,
      qt.reshape(by_head), kt.reshape(by_head), vt.reshape(by_head), *cache_views)
    return x_out, yt.reshape(GROUP_WIDTH, LANES)


def _sample_ffn_kernel(x_ref, yt_ref, u_ref, st_ref, ap_ref, aa_ref, mod_ref, n2_ref, wpg_ref, ps_ref,
                       wpb_ref, wab_ref, wo_ref, wup_ref, wdn_ref, o_ref, x1_sc, h2b_sc, y_sc):
    c = pl.program_id(0)
    n = x_ref.shape[0]

    @pl.when(c == 0)
    def _():
        u = u_ref[...]
        attn_y = yt_ref[...].T[0:n, :]
        tail = jnp.zeros_like(u)
        sums = {}
        for back in range(1, POOL_HIST + 1):
            row = POOL_HIST - back
            tail = tail + st_ref[:, row * POOL_WIDTH:(row + 1) * POOL_WIDTH]
            if back + 1 in POOL_WINDOWS:
                sums[back + 1] = tail
        pool_p = []
        for g, w in enumerate(POOL_WINDOWS):
            cols = slice(g * POOL_GROUP, (g + 1) * POOL_GROUP)
            cnt = min(PAST_LEN + 1.0, float(w))
            pool_p.append((sums[w][:, cols] + u[:, cols]) / cnt - u[:, cols])
        x1, h2b = _merge(x_ref[...], pool_p, attn_y, ap_ref[...], aa_ref[...], mod_ref, n2_ref, wpg_ref,
                         ps_ref, wpb_ref, wab_ref, wo_ref)
        x1_sc[...] = x1
        h2b_sc[...] = h2b
        y_sc[...] = jnp.zeros_like(y_sc)

    y_sc[...] += _ffn(h2b_sc[...], wup_ref, wdn_ref)

    @pl.when(c == pl.num_programs(0) - 1)
    def _():
        o_ref[...] = x1_sc[...] + mod_ref[:, 5 * D_MODEL:6 * D_MODEL] * y_sc[...]


def _sample_ffn(layer, x, attn_yt, u, state2d, a_pool, a_attn, mod_s, n2, wpg, ps, wpb, wab, wo, wup, wdn):
    n = x.shape[0]
    full = lambda a: pl.BlockSpec(a.shape, lambda c: (0,) * a.ndim)
    front = (mod_s, n2, wpg, ps, wpb, wab, wo)
    in_specs = [full(a) for a in (x, attn_yt, u)] + [_layer_spec(state2d, layer)] + [full(a) for a in (a_pool, a_attn)]
    in_specs += [_layer_spec(a, layer) for a in front]
    up_layer = layer if wup.shape[0] > 1 else 0
    dn_layer = layer if wdn.shape[0] > 1 else 0
    in_specs += [pl.BlockSpec((None, D_MODEL, FF_CHUNK), lambda c: (up_layer, 0, c)),
                 pl.BlockSpec((None, FF_CHUNK, D_MODEL), lambda c: (dn_layer, c, 0))]
    return pl.pallas_call(
        _sample_ffn_kernel,
        grid=(D_FF // FF_CHUNK,),
        in_specs=in_specs,
        out_specs=pl.BlockSpec((n, D_MODEL), lambda c: (0, 0)),
        out_shape=jax.ShapeDtypeStruct((n, D_MODEL), F32),
        scratch_shapes=[pltpu.VMEM((n, D_MODEL), F32), pltpu.VMEM((n, D_MODEL), BF16), pltpu.VMEM((n, D_MODEL), F32)],
        compiler_params=_cparams("arbitrary"),
        name="sample_merge_ffn",
    )(x, attn_yt, u, state2d, a_pool, a_attn, *front, wup, wdn)


def _sample_attn_step(step, q_ref, k_ref, v_ref, caches, o_ref):
    bt = caches[0][0].shape[0]
    lane = lax.broadcasted_iota(jnp.int32, (HEAD_DIM, LANES), 1)

    @pl.when(step == 0)
    def _():
        o_ref[...] = jnp.zeros_like(o_ref)

    for b in range(bt):
        mine = lane == step * bt + b
        for h in range(HEADS_PER_GROUP):
            column = lambda ref, g: jnp.sum(jnp.where(mine, ref[g, h], 0.0), axis=1, keepdims=True)
            outs, lses = [], []
            for g, (ck, cv) in enumerate(caches):
                d = DILATIONS[g]
                q = column(q_ref, g)
                s = jnp.sum(ck[b, h] * q, axis=0, keepdims=True)
                if d > 1:
                    row = lax.broadcasted_iota(jnp.int32, s.shape, 1)
                    s = jnp.where(row % d == 0, s, -jnp.inf)
                s_new = jnp.sum(column(k_ref, g) * q, axis=0, keepdims=True)
                m = jnp.maximum(jnp.max(s, axis=1, keepdims=True), s_new)
                p = jnp.exp(s - m)
                p_new = jnp.exp(s_new - m)
                den = jnp.sum(p, axis=1, keepdims=True) + p_new
                acc = jnp.sum(cv[b, h] * p, axis=1, keepdims=True) + p_new * column(v_ref, g)
                outs.append(acc / den)
                lses.append(m + jnp.log(den))
            mx = jnp.maximum(jnp.maximum(lses[0], lses[1]), lses[2])
            es = [jnp.exp(l - mx) for l in lses]
            merged = (es[0] * outs[0] + es[1] * outs[1] + es[2] * outs[2]) / (es[0] + es[1] + es[2])
            o_ref[h] = jnp.where(mine, merged, o_ref[h])


def _rope_tables(pos):
    inv_freq = ROPE_THETA ** (-jnp.arange(0, HEAD_DIM, 2, dtype=F32) / HEAD_DIM)
    ang = pos[:, None] * inv_freq[None, :]
    cos, sin = jnp.cos(ang), jnp.sin(ang)
    zero = jnp.zeros_like(sin)
    reps = LANES // HEAD_DIM
    cos_t = jnp.tile(jnp.concatenate([cos, cos], axis=-1), (1, reps))
    sa_t = jnp.tile(jnp.concatenate([-sin, zero], axis=-1), (1, reps))
    sb_t = jnp.tile(jnp.concatenate([zero, sin], axis=-1), (1, reps))
    return cos_t, sa_t, sb_t


def kernel(x_prompt, x_sample, cache_k_w128, cache_v_w128, cache_k_w512, cache_v_w512, cache_k_w2048,
           cache_v_w2048, state_pool, c_prompt, c_sample, norm1_g, norm2_g, w_ada, b_ada, w_in, q_norm_g,
           k_norm_g, w_pool_grp, pool_scale, w_pool_br, w_attn_br, w_out, w_up, w_down):
    depth = w_in.shape[0]
    b, s, _ = x_prompt.shape
    n = x_sample.shape[0]
    assert x_sample.shape[1] == 1 and s % min(SUPER, s) == 0 and s % min(PROMPT_TILE, s) == 0

    mod_p, mod_s = _modulation(c_prompt, c_sample, w_ada, b_ada)
    tab_p = _rope_tables(jnp.arange(s, dtype=F32))
    tab_s = _rope_tables(PAST_LEN + jnp.arange(1, dtype=F32))

    caches = []
    for ck, cv, d in zip((cache_k_w128, cache_k_w512, cache_k_w2048),
                         (cache_v_w128, cache_v_w512, cache_v_w2048), DILATIONS):
        for c in (ck, cv):
            assert c.shape[2] == N_KEYS * d
            caches.append(jnp.transpose(c, (0, 1, 3, 4, 2)))
    state2d = state_pool.reshape(depth, n, POOL_HIST * POOL_WIDTH)

    wpg_b = w_pool_grp.astype(BF16)
    wab_b = w_attn_br.astype(BF16)
    w_in_l = w_in[0:1].astype(BF16)

    n1 = norm1_g.reshape(depth, 1, D_MODEL)
    n2 = norm2_g.reshape(depth, 1, D_MODEL)
    gq = jnp.tile(q_norm_g, (1, ATTN_WIDTH // HEAD_DIM)).reshape(depth, 1, ATTN_WIDTH)
    gk = jnp.tile(k_norm_g, (1, ATTN_WIDTH // HEAD_DIM)).reshape(depth, 1, ATTN_WIDTH)
    ps = pool_scale.reshape(depth, 1, POOL_WIDTH)

    xp = x_prompt
    xs = x_sample.reshape(n, D_MODEL)
    ks = [[] for _ in range(N_GROUPS)]
    vs = [[] for _ in range(N_GROUPS)]
    pool_p, pool_s = [], []
    windows = None
    for l in range(depth):
        to_cast = [(w_pool_br, l), (w_out, l), (w_up, l), (w_down, l)] + ([(w_in, l + 1)] if l + 1 < depth else [])
        outs = _prompt_project(l, depth, xp, mod_p, n1, w_in_l, gq, gk, *tab_p, to_cast, windows)
        u, a_pool, a_attn = outs[0:3]
        qd, kd, vd = outs[3:6], outs[6:9], outs[9:12]
        windows = outs[12:18]
        wpb_l, wo_l, wup_l, wdn_l = outs[18:22]
        tail_w = (n2, wpg_b, ps, wpb_l, wab_b, wo_l, wup_l, wdn_l)
        attn_y = _prompt_attention(qd, kd, vd)
        u_s, ap_s, aa_s, k_s, v_s, qt, kt, vt = _sample_project(l, xs, mod_s, n1, w_in_l, gq, gk, *tab_s)
        if l + 1 < depth:
            w_in_l = outs[22]
        xp, yt_s = _prompt_ffn(l, xp, attn_y, u, a_pool, a_attn, mod_p, *tail_w, n, qt, kt, vt, caches)
        pool_p.append(u[:, s - POOL_HIST:, :])

        xs = _sample_ffn(l, xs, yt_s, u_s, state2d, ap_s, aa_s, mod_s, *tail_w)
        kh, vh = (a.reshape(n, N_GROUPS, HEADS_PER_GROUP, HEAD_DIM) for a in (k_s, v_s))
        for g in range(N_GROUPS):
            ks[g].append(kh[:, g:g + 1])
            vs[g].append(vh[:, g:g + 1])
        pool_s.append(jnp.concatenate([state_pool[l][:, 1:], u_s[:, None, :]], axis=1))

    st = lambda rows: jnp.stack(rows, axis=0)

    def window(a):
        return jnp.transpose(a.reshape(depth, b, HEADS_PER_GROUP, HEAD_DIM, -1), (0, 1, 4, 2, 3))

    kp, vp = windows[:N_GROUPS], windows[N_GROUPS:]
    return (xp, xs.reshape(n, 1, D_MODEL),
            window(kp[0]), window(vp[0]), window(kp[1]), window(vp[1]), window(kp[2]), window(vp[2]), st(pool_p),
            st(ks[0]), st(vs[0]), st(ks[1]), st(vs[1]), st(ks[2]), st(vs[2]), st(pool_s))
```

```python
import functools
import math

import jax
import jax.numpy as jnp
from jax import lax
from jax.experimental import pallas as pl
from jax.experimental.pallas import tpu as pltpu

F32 = jnp.float32
BF16 = jnp.bfloat16

D_MODEL = 1024
DEPTH = 4
PAST_LEN = 8192
POOL_WIDTH = 512
POOL_WINDOWS = (2, 4, 8, 16)
POOL_GROUP = 128
POOL_HIST = 15
HEAD_DIM = 64
HEADS_PER_GROUP = 4
GROUP_WIDTH = HEADS_PER_GROUP * HEAD_DIM
DILATIONS = (1, 4, 16)
WINDOWS = (128, 512, 2048)
N_GROUPS = 3
ATTN_WIDTH = N_GROUPS * GROUP_WIDTH
N_KEYS = 128
BLK = 128
D_FF = 4096
ROPE_THETA = 10000.0
EPS = 1e-6
Q_SCALE = 1.0 / math.sqrt(HEAD_DIM)
LOG2E = math.log2(math.e)
LN2 = math.log(2.0)

COL_U = 0
COL_Q = POOL_WIDTH
COL_K = COL_Q + ATTN_WIDTH
COL_V = COL_K + ATTN_WIDTH
COL_AP = COL_V + ATTN_WIDTH
COL_AA = COL_AP + D_MODEL
IN_WIDTH = COL_AA + D_MODEL

LANES = 128
PROMPT_TILE = 512
SUPER = BLK * DILATIONS[-1]
VMEM_LIMIT = 56 * 1024 * 1024


def _cparams(*sem):
    return pltpu.CompilerParams(dimension_semantics=sem, vmem_limit_bytes=VMEM_LIMIT)


def _layer_spec(arr, layer):
    nd = arr.ndim - 1
    first = layer if arr.shape[0] > 1 else 0
    return pl.BlockSpec((None,) + arr.shape[1:], lambda *_: (first,) + (0,) * nd, pipeline_mode=pl.Buffered(1))


def _dot(a, b):
    return jnp.dot(a, b, preferred_element_type=F32)


def _rms_modulate(x, gain, shift, scale):
    ms = jnp.mean(x * x, axis=-1, keepdims=True)
    row_gain = gain * (1.0 + scale)
    return x * lax.rsqrt(ms + EPS) * row_gain + shift


def _mod_kernel(cp_ref, cs_ref, w_ref, b_ref, op_ref, os_ref):
    w = w_ref[...].astype(BF16)

    def mod(c):
        s = c * (1.0 / (1.0 + jnp.exp(-c)))
        return _dot(s.astype(BF16), w) + b_ref[...]

    mp = mod(cp_ref[...])
    for bi in range(op_ref.shape[0]):
        op_ref[bi] = mp[bi:bi + 1, :]
    os_ref[...] = mod(cs_ref[...])


def _modulation(c_prompt, c_sample, w_ada, b_ada):
    b, n = c_prompt.shape[0], c_sample.shape[0]
    depth = w_ada.shape[0]
    tn = 1536
    return pl.pallas_call(
        _mod_kernel,
        grid=(depth, 6 * D_MODEL // tn),
        in_specs=[
            pl.BlockSpec((b, D_MODEL), lambda l, j: (0, 0)),
            pl.BlockSpec((n, D_MODEL), lambda l, j: (0, 0)),
            pl.BlockSpec((None, D_MODEL, tn), lambda l, j: (l, 0, j)),
            pl.BlockSpec((None, 1, tn), lambda l, j: (l, 0, j)),
        ],
        out_specs=[pl.BlockSpec((None, b, 1, tn), lambda l, j: (l, 0, 0, j)),
                   pl.BlockSpec((None, n, tn), lambda l, j: (l, 0, j))],
        out_shape=[jax.ShapeDtypeStruct((depth, b, 1, 6 * D_MODEL), F32),
                   jax.ShapeDtypeStruct((depth, n, 6 * D_MODEL), F32)],
        compiler_params=_cparams("parallel", "parallel"),
        name="adaln_mod",
    )(c_prompt, c_sample, w_ada, b_ada.reshape(depth, 1, 6 * D_MODEL))


def _head_norm_rope(z, gain_ref, cos_ref, sa_ref, sb_ref, out_scale, store):
    cos = cos_ref[...]
    sa = sa_ref[...]
    sb = sb_ref[...]
    low = lax.broadcasted_iota(jnp.int32, (1, LANES), 1) < HEAD_DIM
    for c in range(ATTN_WIDTH // LANES):
        zc = z[:, c * LANES:(c + 1) * LANES]
        sq = zc * zc
        lo = jnp.sum(jnp.where(low, sq, 0.0), axis=-1, keepdims=True)
        hi = jnp.sum(jnp.where(low, 0.0, sq), axis=-1, keepdims=True)
        ms = jnp.where(low, lo, hi) * (1.0 / HEAD_DIM)
        xx = zc * lax.rsqrt(ms + EPS) * gain_ref[:, c * LANES:(c + 1) * LANES]
        r = xx * cos + pltpu.roll(xx, LANES - HEAD_DIM // 2, 1) * sa + pltpu.roll(xx, HEAD_DIM // 2, 1) * sb
        store(c, r * out_scale if out_scale != 1.0 else r)


def _project(x_ref, mod_ref, n1_ref, w_ref, gq_ref, gk_ref, cos_ref, sa_ref, sb_ref,
             store_q, store_k, store_v, q_scale, after_qkv=None):
    x = x_ref[...]
    h = _rms_modulate(x, n1_ref[...], mod_ref[:, 0:D_MODEL], mod_ref[:, D_MODEL:2 * D_MODEL])
    hb = h.astype(BF16)
    _head_norm_rope(_dot(hb, w_ref[:, COL_Q:COL_K]), gq_ref, cos_ref, sa_ref, sb_ref, q_scale, store_q)
    _head_norm_rope(_dot(hb, w_ref[:, COL_K:COL_V]), gk_ref, cos_ref, sa_ref, sb_ref, 1.0, store_k)
    v = _dot(hb, w_ref[:, COL_V:COL_AP])
    for c in range(ATTN_WIDTH // LANES):
        store_v(c, v[:, c * LANES:(c + 1) * LANES])
    if after_qkv is not None:
        after_qkv()
    u = _dot(hb, w_ref[:, COL_U:COL_Q])
    a_pool = _dot(hb, w_ref[:, COL_AP:COL_AA])
    a_attn = _dot(hb, w_ref[:, COL_AA:IN_WIDTH])
    return u, a_pool, a_attn


def _chunk_store(ref):
    def store(c, val):
        ref[c] = val
    return store


def _lane_store(ref, rows):
    def store(c, val):
        ref[0:rows, c * LANES:(c + 1) * LANES] = val
    return store


def _prompt_proj_kernel(*refs, n_cast, n_alias, win_first, win_always):
    (x_ref, mod_ref, n1_ref, w_ref, gq_ref, gk_ref, cos_ref, sa_ref, sb_ref), refs = refs[:9], refs[9:]
    cast_in, refs = refs[:n_cast], refs[n_cast + n_alias:]
    (u_ref, ap_ref, aa_ref, q0, q1, q2, k0, k1, k2, v0, v1, v2,
     kw0, kw1, kw2, vw0, vw1, vw2), refs = refs[:18], refs[18:]
    cast_out, (qs_ref, ks_ref, vs_ref) = refs[:n_cast], refs[n_cast:]
    tm = x_ref.shape[0]
    for src, dst in zip(cast_in, cast_out):
        dst[...] = src[...].astype(BF16)

    def write_qkv():
        for src, dsts in ((qs_ref, (q0, q1, q2)), (ks_ref, (k0, k1, k2)), (vs_ref, (v0, v1, v2))):
            for g, dst in enumerate(dsts):
                d = DILATIONS[g]
                for r in range(d):
                    rows = pl.ds(r, tm // d, stride=d) if d > 1 else slice(None)
                    for half in range(GROUP_WIDTH // LANES):
                        c = g * (GROUP_WIDTH // LANES) + half
                        dst[r, :, half * LANES:(half + 1) * LANES] = src[c, rows, :].astype(BF16)
        for g in range(N_GROUPS):
            if win_always[g]:
                write_window(g)

    def write_window(g):
        for src, dst in ((ks_ref, (kw0, kw1, kw2)[g]), (vs_ref, (vw0, vw1, vw2)[g])):
            rows = dst.shape[1]
            for half in range(GROUP_WIDTH // LANES):
                c = g * (GROUP_WIDTH // LANES) + half
                dst[half * LANES:(half + 1) * LANES, :] = src[c, tm - rows:, :].T

    u, a_pool, a_attn = _project(x_ref, mod_ref, n1_ref, w_ref, gq_ref, gk_ref, cos_ref, sa_ref, sb_ref,
                                 _chunk_store(qs_ref), _chunk_store(ks_ref), _chunk_store(vs_ref),
                                 Q_SCALE * LOG2E,
                                 after_qkv=write_qkv)
    u_ref[...] = u
    ap_ref[...] = a_pool.astype(BF16)
    aa_ref[...] = a_attn.astype(BF16)

    i = pl.program_id(1)
    for g in range(N_GROUPS):
        if not win_always[g]:
            pl.when(i >= win_first[g])(functools.partial(write_window, g))


def _prompt_project(layer, depth, x, mod_p, n1, w_in, gq, gk, cos, sa, sb, to_cast, windows):
    b, s, _ = x.shape
    tm = min(PROMPT_TILE, s)
    nt = s // tm
    steps = b * nt
    row = lambda width: pl.BlockSpec((None, tm, width), lambda bi, i: (bi, i, 0))
    in_specs = [
        row(D_MODEL),
        pl.BlockSpec((None, None, 1, 6 * D_MODEL), lambda bi, i: (layer, bi, 0, 0)),
        _layer_spec(n1, layer),
        _layer_spec(w_in, layer),
        _layer_spec(gq, layer),
        _layer_spec(gk, layer),
        pl.BlockSpec((tm, LANES), lambda bi, i: (i, 0)),
        pl.BlockSpec((tm, LANES), lambda bi, i: (i, 0)),
        pl.BlockSpec((tm, LANES), lambda bi, i: (i, 0)),
    ]
    out_shape = [jax.ShapeDtypeStruct((b, s, POOL_WIDTH), F32),
                 jax.ShapeDtypeStruct((b, s, D_MODEL), BF16),
                 jax.ShapeDtypeStruct((b, s, D_MODEL), BF16)]
    out_specs = [row(POOL_WIDTH), row(D_MODEL), row(D_MODEL)]
    for _ in range(3):
        for d in DILATIONS:
            out_shape.append(jax.ShapeDtypeStruct((b, d, s // d, GROUP_WIDTH), BF16))
            out_specs.append(pl.BlockSpec((None, d, tm // d, GROUP_WIDTH), lambda bi, i: (bi, 0, i, 0)))
    win_first = []
    for _ in range(2):
        for w in WINDOWS:
            keep = min(w, s)
            out_shape.append(jax.ShapeDtypeStruct((depth, b, GROUP_WIDTH, keep), F32))
            first = (s - keep) // tm if keep >= tm else nt - 1
            win_first.append(first)
            out_specs.append(pl.BlockSpec(
                (None, None, GROUP_WIDTH, min(keep, tm)),
                lambda bi, i, first=first: (layer, bi, 0, jnp.maximum(i - first, 0))))
    for w, wl in to_cast:
        _, k, nn = w.shape
        assert k % (steps * 16) == 0
        in_specs.append(pl.BlockSpec((None, k // steps, nn), lambda bi, i, wl=wl: (wl, bi * nt + i, 0)))
        out_specs.append(pl.BlockSpec((None, k // steps, nn), lambda bi, i: (0, bi * nt + i, 0)))
        out_shape.append(jax.ShapeDtypeStruct((1, k, nn), BF16))
    windows = list(windows or ())
    n_in = len(in_specs)
    in_specs += [pl.BlockSpec(memory_space=pl.ANY)] * len(windows)
    return pl.pallas_call(
        functools.partial(_prompt_proj_kernel, n_cast=len(to_cast), n_alias=len(windows),
                          win_first=tuple(win_first[:N_GROUPS]),
                          win_always=tuple(2 * f <= nt for f in win_first[:N_GROUPS])),
        grid=(b, nt),
        in_specs=in_specs,
        out_specs=out_specs,
        out_shape=out_shape,
        input_output_aliases={n_in + k: 12 + k for k in range(len(windows))},
        scratch_shapes=[pltpu.VMEM((ATTN_WIDTH // LANES, tm, LANES), F32)] * 3,
        compiler_params=_cparams("parallel", "arbitrary"),
        name="prompt_project",
    )(x, mod_p, n1, w_in, gq, gk, cos, sa, sb, *[w for w, _ in to_cast], *windows)


def _sample_proj_kernel(x_ref, mod_ref, n1_ref, w_ref, gq_ref, gk_ref, cos_ref, sa_ref, sb_ref,
                        u_ref, ap_ref, aa_ref, k_ref, v_ref, qt_ref, kt_ref, vt_ref, q_sc, k_sc, v_sc):
    n = x_ref.shape[0]
    for sc in (q_sc, k_sc, v_sc):
        sc[...] = jnp.zeros_like(sc)
    u, a_pool, a_attn = _project(x_ref, mod_ref, n1_ref, w_ref, gq_ref, gk_ref, cos_ref, sa_ref, sb_ref,
                                 _lane_store(q_sc, n), _lane_store(k_sc, n), _lane_store(v_sc, n), Q_SCALE)
    u_ref[...] = u
    ap_ref[...] = a_pool
    aa_ref[...] = a_attn
    k_ref[...] = k_sc[0:n, :]
    v_ref[...] = v_sc[0:n, :]
    for sc, dst in ((q_sc, qt_ref), (k_sc, kt_ref), (v_sc, vt_ref)):
        for c in range(ATTN_WIDTH // LANES):
            dst[c * LANES:(c + 1) * LANES, :] = sc[:, c * LANES:(c + 1) * LANES].T


def _sample_project(layer, x, mod_s, n1, w_in, gq, gk, cos, sa, sb):
    n = x.shape[0]
    assert n <= LANES
    full = lambda shape: pl.BlockSpec(shape, lambda i: (0,) * len(shape))
    in_specs = [full(x.shape), _layer_spec(mod_s, layer), _layer_spec(n1, layer), _layer_spec(w_in, layer),
                _layer_spec(gq, layer), _layer_spec(gk, layer), full(cos.shape), full(sa.shape), full(sb.shape)]
    shapes = [(n, POOL_WIDTH), (n, D_MODEL), (n, D_MODEL), (n, ATTN_WIDTH), (n, ATTN_WIDTH)] + [(ATTN_WIDTH, LANES)] * 3
    return pl.pallas_call(
        _sample_proj_kernel,
        grid=(1,),
        in_specs=in_specs,
        out_specs=[full(sh) for sh in shapes],
        out_shape=[jax.ShapeDtypeStruct(sh, F32) for sh in shapes],
        scratch_shapes=[pltpu.VMEM((LANES, ATTN_WIDTH), F32)] * 3,
        compiler_params=_cparams("arbitrary"),
        name="sample_project",
    )(x, mod_s, n1, w_in, gq, gk, cos, sa, sb)


def _head_lane_masks():
    lane = lax.broadcasted_iota(jnp.int32, (1, GROUP_WIDTH), 1)
    return [(lane // HEAD_DIM) == h for h in range(HEADS_PER_GROUP)]


def _prompt_attn_kernel(*refs):
    ins, o_ref = refs[:15], refs[15]
    nat_o, nat_m, nat_d, bias_sc = refs[16:20]
    t = pl.program_id(1)
    hm = _head_lane_masks()
    low_head = lax.broadcasted_iota(jnp.int32, (1, LANES), 1) < HEAD_DIM
    stacked = HEADS_PER_GROUP * BLK
    row = lax.broadcasted_iota(jnp.int32, (stacked, 2 * BLK), 0)
    col = lax.broadcasted_iota(jnp.int32, (stacked, 2 * BLK), 1)
    dist = BLK + (row % BLK) - col
    band = (dist >= 0) & (dist <= N_KEYS)
    bias_sc[0] = jnp.where(band, 0.0, -jnp.inf)
    bias_sc[1] = jnp.where(band & (col >= BLK), 0.0, -jnp.inf)

    for g in range(N_GROUPS):
        d = DILATIONS[g]
        q_ref, kc_ref, kp_ref, vc_ref, vp_ref = ins[5 * g:5 * g + 5]
        nb = kc_ref.shape[1] // BLK
        halves = GROUP_WIDTH // LANES

        for r in range(d):
            for j in range(nb):
                q = q_ref[r, j * BLK:(j + 1) * BLK, :]
                if j == 0:
                    keys = jnp.concatenate([kp_ref[r], kc_ref[r, 0:BLK, :]], axis=0)
                    vals = jnp.concatenate([vp_ref[r], vc_ref[r, 0:BLK, :]], axis=0)
                    bias = bias_sc[jnp.where(t > 0, 0, 1)]
                else:
                    keys = kc_ref[r, (j - 1) * BLK:(j + 1) * BLK, :]
                    vals = vc_ref[r, (j - 1) * BLK:(j + 1) * BLK, :]
                    bias = bias_sc[0]
                zero = jnp.zeros_like(q)
                qst = jnp.concatenate([jnp.where(m, q, zero) for m in hm], axis=0)
                s = lax.dot_general(qst, keys, (((1,), (1,)), ((), ())), preferred_element_type=F32) + bias
                m = jnp.max(s, axis=-1, keepdims=True)
                p = jnp.exp2(s - m)
                den = jnp.sum(p, axis=-1, keepdims=True)
                pb = p.astype(BF16)
                first = j * BLK * d + r
                rows = pl.ds(first, BLK, stride=d) if d > 1 else slice(first, first + BLK)
                for half in range(halves):
                    ra = slice(2 * half * BLK, (2 * half + 1) * BLK)
                    rb = slice((2 * half + 1) * BLK, (2 * half + 2) * BLK)
                    o = _dot(pb[2 * half * BLK:(2 * half + 2) * BLK], vals[:, half * LANES:(half + 1) * LANES])
                    nat_o[g * halves + half, rows, :] = jnp.where(low_head, o[0:BLK], o[BLK:2 * BLK])
                    nat_m[g * halves + half, rows, :] = jnp.where(low_head, m[ra], m[rb])
                    nat_d[g * halves + half, rows, :] = jnp.where(low_head, den[ra], den[rb])

    chunk = 256
    halves = GROUP_WIDTH // LANES

    def merge(ci, carry):
        rows = pl.ds(pl.multiple_of(ci * chunk, chunk), chunk)
        for half in range(halves):
            ms = [nat_m[g * halves + half, rows, :] for g in range(N_GROUPS)]
            top = jnp.maximum(jnp.maximum(ms[0], ms[1]), ms[2])
            es = [jnp.exp2(mg - top) for mg in ms]
            acc = sum(es[g] * nat_o[g * halves + half, rows, :] for g in range(N_GROUPS))
            tot = sum(es[g] * nat_d[g * halves + half, rows, :] for g in range(N_GROUPS))
            o_ref[rows, half * LANES:(half + 1) * LANES] = (acc / tot).astype(o_ref.dtype)
        return carry

    lax.fori_loop(0, o_ref.shape[0] // chunk, merge, 0)


def _prompt_attention(qd, kd, vd):
    b = qd[0].shape[0]
    s = qd[0].shape[2]
    sb = min(SUPER, s)
    nsb = s // sb
    ins, in_specs = [], []
    for g, d in enumerate(DILATIONS):
        rpc = sb // d
        nb = rpc // BLK
        cur = pl.BlockSpec((None, d, rpc, GROUP_WIDTH), lambda bi, t: (bi, 0, t, 0))
        prev = pl.BlockSpec((None, d, BLK, GROUP_WIDTH),
                            lambda bi, t, nb=nb: (bi, 0, jnp.maximum(t * nb - 1, 0), 0))
        ins += [qd[g], kd[g], kd[g], vd[g], vd[g]]
        in_specs += [cur, cur, prev, cur, prev]
    per_position = pltpu.VMEM((N_GROUPS * GROUP_WIDTH // LANES, sb, LANES), F32)
    scratch = [per_position, per_position, per_position, pltpu.VMEM((2, HEADS_PER_GROUP * BLK, 2 * BLK), F32)]
    return pl.pallas_call(
        _prompt_attn_kernel,
        grid=(b, nsb),
        in_specs=in_specs,
        out_specs=pl.BlockSpec((None, sb, GROUP_WIDTH), lambda bi, t: (bi, t, 0)),
        out_shape=jax.ShapeDtypeStruct((b, s, GROUP_WIDTH), BF16),
        scratch_shapes=scratch,
        compiler_params=_cparams("parallel", "parallel"),
        name="prompt_attention",
    )(*ins)


def _sigmoid(a):
    return 1.0 / (1.0 + jnp.exp(-a))


FF_CHUNK = 1024
FF_DOWN_CHUNK = 2048
MERGE_COLS = 256


def _merge(x, pool_p, attn_y, a_pool, a_attn, mod_ref, n2_ref, wpg_ref, ps_ref, wpb_ref, wab_ref, wo_ref):
    ys = [_dot(p.astype(BF16), wpg_ref[g]) for g, p in enumerate(pool_p)]
    pool_y = jnp.concatenate(ys, axis=-1) * ps_ref[...]
    pool_yb = pool_y.astype(BF16)
    attn_yb = attn_y.astype(BF16)
    merged = []
    for c in range(D_MODEL // MERGE_COLS):
        cols = slice(c * MERGE_COLS, (c + 1) * MERGE_COLS)
        pb = _dot(pool_yb, wpb_ref[:, cols])
        ab = _dot(attn_yb, wab_ref[:, cols])
        gated = _sigmoid(a_pool[:, cols].astype(F32)) * pb + _sigmoid(a_attn[:, cols].astype(F32)) * ab
        merged.append(gated.astype(BF16))
    g1 = mod_ref[:, 2 * D_MODEL:3 * D_MODEL]
    x1 = x + g1 * _dot(jnp.concatenate(merged, axis=-1), wo_ref[...])
    h2 = _rms_modulate(x1, n2_ref[...], mod_ref[:, 3 * D_MODEL:4 * D_MODEL], mod_ref[:, 4 * D_MODEL:5 * D_MODEL])
    return x1, h2.astype(BF16)


def _ffn(h2b, wup_ref, wdn_ref):
    width = wup_ref.shape[1]
    down = min(FF_DOWN_CHUNK, width)
    y = None
    for d0 in range(0, width, down):
        acts = []
        for c0 in range(d0, d0 + down, FF_CHUNK):
            hid = jnp.maximum(_dot(h2b, wup_ref[:, c0:c0 + FF_CHUNK]), 0.0)
            acts.append((hid * hid).astype(BF16))
        part = _dot(acts[0] if len(acts) == 1 else jnp.concatenate(acts, axis=-1), wdn_ref[d0:d0 + down, :])
        y = part if y is None else y + part
    return y


def _prompt_ffn_kernel(x_ref, y_ref, u_ref, up_ref, ap_ref, aa_ref, mod_ref, n2_ref, wpg_ref, ps_ref,
                       wpb_ref, wab_ref, wo_ref, wup_ref, wdn_ref,
                       qt_ref, kt_ref, vt_ref, ck0, cv0, ck1, cv1, ck2, cv2,
                       o_ref, yt_ref, ext_ref):
    i = pl.program_id(1)
    step = pl.program_id(0) * pl.num_programs(1) + i
    _sample_attn_step(step, qt_ref, kt_ref, vt_ref, ((ck0, cv0), (ck1, cv1), (ck2, cv2)), yt_ref)
    tm = x_ref.shape[0]
    hist = up_ref.shape[0]
    ext_ref[0:hist, :] = jnp.where(i > 0, up_ref[...], 0.0)
    ext_ref[hist:, :] = u_ref[...]
    pos = (i * tm + lax.broadcasted_iota(jnp.int32, (tm, 1), 0)).astype(F32)
    pool_p = []
    for g, w in enumerate(POOL_WINDOWS):
        e = ext_ref[:, g * POOL_GROUP:(g + 1) * POOL_GROUP]
        acc = e
        span = 1
        while span < w:
            acc = acc + pltpu.roll(acc, span, 0)
            span *= 2
        cnt = jnp.minimum(pos + 1.0, float(w))
        pool_p.append(acc[hist:] / cnt - e[hist:])
    x1, h2b = _merge(x_ref[...], pool_p, y_ref[...], ap_ref[...], aa_ref[...], mod_ref, n2_ref, wpg_ref,
                     ps_ref, wpb_ref, wab_ref, wo_ref)
    o_ref[...] = x1 + mod_ref[:, 5 * D_MODEL:6 * D_MODEL] * _ffn(h2b, wup_ref, wdn_ref)


def _prompt_ffn(layer, x, attn_y, u, a_pool, a_attn, mod_p, n2, wpg, ps, wpb, wab, wo, wup, wdn,
                n, qt, kt, vt, cache_views):
    b, s, _ = x.shape
    tm = min(PROMPT_TILE, s)
    nt = s // tm
    assert n % (b * nt) == 0
    bt = n // (b * nt)
    hist = 16
    row = lambda width: pl.BlockSpec((None, tm, width), lambda bi, i: (bi, i, 0))
    by_head = (N_GROUPS, HEADS_PER_GROUP, HEAD_DIM, LANES)
    new = pl.BlockSpec(by_head, lambda bi, i: (0, 0, 0, 0))
    in_specs = [
        row(D_MODEL), row(GROUP_WIDTH), row(POOL_WIDTH),
        pl.BlockSpec((None, hist, POOL_WIDTH), lambda bi, i: (bi, jnp.maximum(i * (tm // hist) - 1, 0), 0)),
        row(D_MODEL), row(D_MODEL),
        pl.BlockSpec((None, None, 1, 6 * D_MODEL), lambda bi, i: (layer, bi, 0, 0)),
    ] + [_layer_spec(w, layer) for w in (n2, wpg, ps, wpb, wab, wo, wup, wdn)] + [new, new, new]
    for c in cache_views:
        in_specs.append(pl.BlockSpec((None, bt) + c.shape[2:], lambda bi, i: (layer, bi * nt + i, 0, 0, 0)))
    x_out, yt = pl.pallas_call(
        _prompt_ffn_kernel,
        grid=(b, nt),
        in_specs=in_specs,
        out_specs=[row(D_MODEL), pl.BlockSpec(by_head[1:], lambda bi, i: (0, 0, 0))],
        out_shape=[jax.ShapeDtypeStruct((b, s, D_MODEL), F32), jax.ShapeDtypeStruct(by_head[1:], F32)],
        scratch_shapes=[pltpu.VMEM((hist + tm, POOL_WIDTH), F32)],
        compiler_params=_cparams("arbitrary", "arbitrary"),
        name="prompt_merge_ffn",
    )(x, attn_y, u, u, a_pool, a_attn, mod_p, n2, wpg, ps, wpb, wab, wo, wup, wdn,
      qt.reshape(by_head), kt.reshape(by_head), vt.reshape(by_head), *cache_views)
    return x_out, yt.reshape(GROUP_WIDTH, LANES)


def _sample_ffn_kernel(x_ref, yt_ref, u_ref, st_ref, ap_ref, aa_ref, mod_ref, n2_ref, wpg_ref, ps_ref,
                       wpb_ref, wab_ref, wo_ref, wup_ref, wdn_ref, o_ref, x1_sc, h2b_sc, y_sc):
    c = pl.program_id(0)
    n = x_ref.shape[0]

    @pl.when(c == 0)
    def _():
        u = u_ref[...]
        attn_y = yt_ref[...].T[0:n, :]
        tail = jnp.zeros_like(u)
        sums = {}
        for back in range(1, POOL_HIST + 1):
            row = POOL_HIST - back
            tail = tail + st_ref[:, row * POOL_WIDTH:(row + 1) * POOL_WIDTH]
            if back + 1 in POOL_WINDOWS:
                sums[back + 1] = tail
        pool_p = []
        for g, w in enumerate(POOL_WINDOWS):
            cols = slice(g * POOL_GROUP, (g + 1) * POOL_GROUP)
            cnt = min(PAST_LEN + 1.0, float(w))
            pool_p.append((sums[w][:, cols] + u[:, cols]) / cnt - u[:, cols])
        x1, h2b = _merge(x_ref[...], pool_p, attn_y, ap_ref[...], aa_ref[...], mod_ref, n2_ref, wpg_ref,
                         ps_ref, wpb_ref, wab_ref, wo_ref)
        x1_sc[...] = x1
        h2b_sc[...] = h2b
        y_sc[...] = jnp.zeros_like(y_sc)

    y_sc[...] += _ffn(h2b_sc[...], wup_ref, wdn_ref)

    @pl.when(c == pl.num_programs(0) - 1)
    def _():
        o_ref[...] = x1_sc[...] + mod_ref[:, 5 * D_MODEL:6 * D_MODEL] * y_sc[...]


def _sample_ffn(layer, x, attn_yt, u, state2d, a_pool, a_attn, mod_s, n2, wpg, ps, wpb, wab, wo, wup, wdn):
    n = x.shape[0]
    full = lambda a: pl.BlockSpec(a.shape, lambda c: (0,) * a.ndim)
    front = (mod_s, n2, wpg, ps, wpb, wab, wo)
    in_specs = [full(a) for a in (x, attn_yt, u)] + [_layer_spec(state2d, layer)] + [full(a) for a in (a_pool, a_attn)]
    in_specs += [_layer_spec(a, layer) for a in front]
    up_layer = layer if wup.shape[0] > 1 else 0
    dn_layer = layer if wdn.shape[0] > 1 else 0
    in_specs += [pl.BlockSpec((None, D_MODEL, FF_CHUNK), lambda c: (up_layer, 0, c)),
                 pl.BlockSpec((None, FF_CHUNK, D_MODEL), lambda c: (dn_layer, c, 0))]
    return pl.pallas_call(
        _sample_ffn_kernel,
        grid=(D_FF // FF_CHUNK,),
        in_specs=in_specs,
        out_specs=pl.BlockSpec((n, D_MODEL), lambda c: (0, 0)),
        out_shape=jax.ShapeDtypeStruct((n, D_MODEL), F32),
        scratch_shapes=[pltpu.VMEM((n, D_MODEL), F32), pltpu.VMEM((n, D_MODEL), BF16), pltpu.VMEM((n, D_MODEL), F32)],
        compiler_params=_cparams("arbitrary"),
        name="sample_merge_ffn",
    )(x, attn_yt, u, state2d, a_pool, a_attn, *front, wup, wdn)


def _sample_attn_step(step, q_ref, k_ref, v_ref, caches, o_ref):
    bt = caches[0][0].shape[0]
    lane = lax.broadcasted_iota(jnp.int32, (HEAD_DIM, LANES), 1)

    @pl.when(step == 0)
    def _():
        o_ref[...] = jnp.zeros_like(o_ref)

    for b in range(bt):
        mine = lane == step * bt + b
        for h in range(HEADS_PER_GROUP):
            column = lambda ref, g: jnp.sum(jnp.where(mine, ref[g, h], 0.0), axis=1, keepdims=True)
            outs, lses = [], []
            for g, (ck, cv) in enumerate(caches):
                d = DILATIONS[g]
                q = column(q_ref, g)
                s = jnp.sum(ck[b, h] * q, axis=0, keepdims=True)
                if d > 1:
                    row = lax.broadcasted_iota(jnp.int32, s.shape, 1)
                    s = jnp.where(row % d == 0, s, -jnp.inf)
                s_new = jnp.sum(column(k_ref, g) * q, axis=0, keepdims=True)
                m = jnp.maximum(jnp.max(s, axis=1, keepdims=True), s_new)
                p = jnp.exp(s - m)
                p_new = jnp.exp(s_new - m)
                den = jnp.sum(p, axis=1, keepdims=True) + p_new
                acc = jnp.sum(cv[b, h] * p, axis=1, keepdims=True) + p_new * column(v_ref, g)
                outs.append(acc / den)
                lses.append(m + jnp.log(den))
            mx = jnp.maximum(jnp.maximum(lses[0], lses[1]), lses[2])
            es = [jnp.exp(l - mx) for l in lses]
            merged = (es[0] * outs[0] + es[1] * outs[1] + es[2] * outs[2]) / (es[0] + es[1] + es[2])
            o_ref[h] = jnp.where(mine, merged, o_ref[h])


def _rope_tables(pos):
    inv_freq = ROPE_THETA ** (-jnp.arange(0, HEAD_DIM, 2, dtype=F32) / HEAD_DIM)
    ang = pos[:, None] * inv_freq[None, :]
    cos, sin = jnp.cos(ang), jnp.sin(ang)
    zero = jnp.zeros_like(sin)
    reps = LANES // HEAD_DIM
    cos_t = jnp.tile(jnp.concatenate([cos, cos], axis=-1), (1, reps))
    sa_t = jnp.tile(jnp.concatenate([-sin, zero], axis=-1), (1, reps))
    sb_t = jnp.tile(jnp.concatenate([zero, sin], axis=-1), (1, reps))
    return cos_t, sa_t, sb_t


def kernel(x_prompt, x_sample, cache_k_w128, cache_v_w128, cache_k_w512, cache_v_w512, cache_k_w2048,
           cache_v_w2048, state_pool, c_prompt, c_sample, norm1_g, norm2_g, w_ada, b_ada, w_in, q_norm_g,
           k_norm_g, w_pool_grp, pool_scale, w_pool_br, w_attn_br, w_out, w_up, w_down):
    depth = w_in.shape[0]
    b, s, _ = x_prompt.shape
    n = x_sample.shape[0]
    assert x_sample.shape[1] == 1 and s % min(SUPER, s) == 0 and s % min(PROMPT_TILE, s) == 0

    mod_p, mod_s = _modulation(c_prompt, c_sample, w_ada, b_ada)
    tab_p = _rope_tables(jnp.arange(s, dtype=F32))
    tab_s = _rope_tables(PAST_LEN + jnp.arange(1, dtype=F32))

    caches = []
    for ck, cv, d in zip((cache_k_w128, cache_k_w512, cache_k_w2048),
                         (cache_v_w128, cache_v_w512, cache_v_w2048), DILATIONS):
        for c in (ck, cv):
            assert c.shape[2] == N_KEYS * d
            caches.append(jnp.transpose(c, (0, 1, 3, 4, 2)))
    state2d = state_pool.reshape(depth, n, POOL_HIST * POOL_WIDTH)

    wpg_b = w_pool_grp.astype(BF16)
    wab_b = w_attn_br.astype(BF16)
    w_in_l = w_in[0:1].astype(BF16)

    n1 = norm1_g.reshape(depth, 1, D_MODEL)
    n2 = norm2_g.reshape(depth, 1, D_MODEL)
    gq = jnp.tile(q_norm_g, (1, ATTN_WIDTH // HEAD_DIM)).reshape(depth, 1, ATTN_WIDTH)
    gk = jnp.tile(k_norm_g, (1, ATTN_WIDTH // HEAD_DIM)).reshape(depth, 1, ATTN_WIDTH)
    ps = pool_scale.reshape(depth, 1, POOL_WIDTH)

    xp = x_prompt
    xs = x_sample.reshape(n, D_MODEL)
    ks = [[] for _ in range(N_GROUPS)]
    vs = [[] for _ in range(N_GROUPS)]
    pool_p, pool_s = [], []
    windows = None
    for l in range(depth):
        to_cast = [(w_pool_br, l), (w_out, l), (w_up, l), (w_down, l)] + ([(w_in, l + 1)] if l + 1 < depth else [])
        outs = _prompt_project(l, depth, xp, mod_p, n1, w_in_l, gq, gk, *tab_p, to_cast, windows)
        u, a_pool, a_attn = outs[0:3]
        qd, kd, vd = outs[3:6], outs[6:9], outs[9:12]
        windows = outs[12:18]
        wpb_l, wo_l, wup_l, wdn_l = outs[18:22]
        tail_w = (n2, wpg_b, ps, wpb_l, wab_b, wo_l, wup_l, wdn_l)
        attn_y = _prompt_attention(qd, kd, vd)
        u_s, ap_s, aa_s, k_s, v_s, qt, kt, vt = _sample_project(l, xs, mod_s, n1, w_in_l, gq, gk, *tab_s)
        if l + 1 < depth:
            w_in_l = outs[22]
        xp, yt_s = _prompt_ffn(l, xp, attn_y, u, a_pool, a_attn, mod_p, *tail_w, n, qt, kt, vt, caches)
        pool_p.append(u[:, s - POOL_HIST:, :])

        xs = _sample_ffn(l, xs, yt_s, u_s, state2d, ap_s, aa_s, mod_s, *tail_w)
        kh, vh = (a.reshape(n, N_GROUPS, HEADS_PER_GROUP, HEAD_DIM) for a in (k_s, v_s))
        for g in range(N_GROUPS):
            ks[g].append(kh[:, g:g + 1])
            vs[g].append(vh[:, g:g + 1])
        pool_s.append(jnp.concatenate([state_pool[l][:, 1:], u_s[:, None, :]], axis=1))

    st = lambda rows: jnp.stack(rows, axis=0)

    def window(a):
        return jnp.transpose(a.reshape(depth, b, HEADS_PER_GROUP, HEAD_DIM, -1), (0, 1, 4, 2, 3))

    kp, vp = windows[:N_GROUPS], windows[N_GROUPS:]
    return (xp, xs.reshape(n, 1, D_MODEL),
            window(kp[0]), window(vp[0]), window(kp[1]), window(vp[1]), window(kp[2]), window(vp[2]), st(pool_p),
            st(ks[0]), st(vs[0]), st(ks[1]), st(vs[1]), st(ks[2]), st(vs[2]), st(pool_s))
```
